```python
import math
import jax, jax.numpy as jnp
from jax import lax
import numpy as np

D_MODEL = 2048
BATCH = 8
SEQ = 2048
DEPTH = 1
DEC_BATCH = 32
DEC_SEQ = 4
PAST_LEN = 16384
PAGE_SIZE = 128

HEAD_DIM = 128
ROPE_DIM = HEAD_DIM // 4
ROPE_THETA = 500000.0
A_HEADS = 8
A_KV_HEADS = 2
IDX_HEADS = 16
IDX_DIM = 64
IDX_ROPE_DIM = IDX_DIM // 4
DSA_TOPK = 256
B_HEADS = 8
B_KV_HEADS = 2
CMP_BLOCK = 32
CMP_STRIDE = 16
SLC_BLOCK = 64
SLC_TOPN = 16
SLC_LOCAL = 2
WINDOW = 512
N_GROUPS = 4
EXPERTS_PER_GROUP = 4
N_EXPERTS = N_GROUPS * EXPERTS_PER_GROUP
TOPK_IN_GROUP = 2
EXPERT_FF = 512
Q_BLOCK = 128
RMS_EPS = 1e-6
NEG = -1e30
FORCE_BONUS = 1e4

IN_SPLITS = (
    ("q_a", A_HEADS * HEAD_DIM),
    ("kv_a", A_KV_HEADS * 2 * HEAD_DIM),
    ("q_i", IDX_HEADS * IDX_DIM),
    ("k_i", IDX_DIM),
    ("w_i", IDX_HEADS),
    ("q_b", B_HEADS * HEAD_DIM),
    ("kv_c", B_KV_HEADS * 2 * HEAD_DIM),
    ("kv_s", B_KV_HEADS * 2 * HEAD_DIM),
    ("kv_w", B_KV_HEADS * 2 * HEAD_DIM),
    ("g_n", B_HEADS * 3),
    ("g_m", 2 * D_MODEL),
)
IN_WIDTH = sum(s for _, s in IN_SPLITS)

kernel_name = "hybrid_dsa_nsa_hmoe_step"


def rmsnorm(x, g):
    xf = x.astype(jnp.float32)
    r = lax.rsqrt(jnp.mean(xf * xf, axis=-1, keepdims=True) + RMS_EPS)
    return (xf * r).astype(x.dtype) * g


def rope(x, pos, rot_dim):
    half = rot_dim // 2
    inv = ROPE_THETA ** (-jnp.arange(half, dtype=jnp.float32) / half)
    ang = pos.astype(jnp.float32)[:, None] * inv[None, :]
    cos = jnp.cos(ang)[:, None, :].astype(x.dtype)
    sin = jnp.sin(ang)[:, None, :].astype(x.dtype)
    x1, x2, rest = x[..., :half], x[..., half:rot_dim], x[..., rot_dim:]
    return jnp.concatenate([x1 * cos - x2 * sin, x2 * cos + x1 * sin, rest], axis=-1)


def masked_softmax(s, mask):
    s = jnp.where(mask, s.astype(jnp.float32), NEG)
    m = jnp.max(s, axis=-1, keepdims=True)
    p = jnp.where(mask, jnp.exp(s - m), 0.0)
    return p / jnp.maximum(jnp.sum(p, axis=-1, keepdims=True), 1e-30)


def mixer_inputs(xn, pos, w_in):
    lead = xn.shape[:-1]
    offs = np.cumsum([s for _, s in IN_SPLITS])[:-1].tolist()
    parts = dict(zip([n for n, _ in IN_SPLITS], jnp.split(xn @ w_in, offs, axis=-1)))

    def kv(a, g):
        return a.reshape(*lead, g, 2, HEAD_DIM)

    def rot_k(a):
        return jnp.stack([rope(a[..., 0, :], pos, ROPE_DIM), a[..., 1, :]], axis=-2)

    return {
        "q_a": rope(parts["q_a"].reshape(*lead, A_HEADS, HEAD_DIM), pos, ROPE_DIM),
        "kv_a": rot_k(kv(parts["kv_a"], A_KV_HEADS)),
        "q_i": rope(parts["q_i"].reshape(*lead, IDX_HEADS, IDX_DIM), pos, IDX_ROPE_DIM),
        "k_i": rope(parts["k_i"][..., None, :], pos, IDX_ROPE_DIM)[..., 0, :],
        "w_i": parts["w_i"],
        "q_b": rope(parts["q_b"].reshape(*lead, B_HEADS, HEAD_DIM), pos, ROPE_DIM),
        "kv_c": kv(parts["kv_c"], B_KV_HEADS),
        "kv_s": rot_k(kv(parts["kv_s"], B_KV_HEADS)),
        "kv_w": rot_k(kv(parts["kv_w"], B_KV_HEADS)),
        "g_n": parts["g_n"].reshape(*lead, B_HEADS, 3),
        "g_m": parts["g_m"],
    }


def dsa_attend(q, qi, wi, qpos, kidx, fetch_kv, L):
    tq = q.shape[0]
    r = A_HEADS // A_KV_HEADS
    rel = jax.nn.relu(jnp.einsum('thi,si->ths', qi, kidx).astype(jnp.float32))
    score = jnp.einsum('ths,th->ts', rel, wi.astype(jnp.float32)) * (IDX_DIM * IDX_HEADS) ** -0.5
    score = jnp.where(jnp.arange(L)[None, :] <= qpos[:, None], score, NEG)
    _, sel = lax.top_k(score, min(DSA_TOPK, L // 4))
    valid = sel <= qpos[:, None]
    kv = fetch_kv(sel[..., None], jnp.arange(A_KV_HEADS))
    qg = q.reshape(tq, A_KV_HEADS, r, HEAD_DIM)
    s = jnp.einsum('tgrd,tkgd->tgrk', qg, kv[..., 0, :]) * HEAD_DIM ** -0.5
    p = masked_softmax(s, valid[:, None, None, :])
    o = jnp.einsum('tgrk,tkgd->tgrd', p.astype(kv.dtype), kv[..., 1, :])
    return o.reshape(tq, A_HEADS * HEAD_DIM)


def cmp_ends(L):
    n = (L - CMP_BLOCK) // CMP_STRIDE + 1
    return jnp.arange(n) * CMP_STRIDE + CMP_BLOCK - 1


def compress(kv, w_phi, pe_phi):
    ends = cmp_ends(kv.shape[0])
    idx = ends[:, None] - (CMP_BLOCK - 1) + jnp.arange(CMP_BLOCK)[None, :]
    rows = kv[idx] + pe_phi[None, :, None]
    comp = jnp.einsum('nlgcd,clde->ngce', rows, w_phi)
    return rope(comp[:, :, 0], ends, ROPE_DIM), comp[:, :, 1]


def cover_matrix(n_cmp, n_slc):
    cs = np.arange(n_cmp) * CMP_STRIDE
    ss = np.arange(n_slc) * SLC_BLOCK
    m = (cs[:, None] < ss[None, :] + SLC_BLOCK) & (cs[:, None] + CMP_BLOCK > ss[None, :])
    return jnp.asarray(m, dtype=jnp.float32)


def nsa_attend(q, gates, qpos, kc, vc, c_end, fetch_slc, L, win_kv, win_pos):
    tq = q.shape[0]
    r = B_HEADS // B_KV_HEADS
    scale = HEAD_DIM ** -0.5
    qg = q.reshape(tq, B_KV_HEADS, r, HEAD_DIM)
    s = jnp.einsum('tgrd,ngd->tgrn', qg, kc) * scale
    p_c = masked_softmax(s, (c_end[None, :] <= qpos[:, None])[:, None, None, :])
    o_c = jnp.einsum('tgrn,ngd->tgrd', p_c.astype(vc.dtype), vc)
    n_slc = -(-L // SLC_BLOCK)
    p_s = jnp.einsum('tgn,nj->tgj', jnp.sum(p_c, axis=2), cover_matrix(kc.shape[0], n_slc))
    blk = jnp.arange(n_slc)[None, :]
    cur = (qpos // SLC_BLOCK)[:, None]
    forced = ((blk == 0) | ((blk <= cur) & (blk > cur - SLC_LOCAL))).astype(jnp.float32)
    score = jnp.where((blk <= cur)[:, None, :], p_s + FORCE_BONUS * forced[:, None, :], NEG)
    n_sel = min(SLC_TOPN, n_slc)
    _, sel = lax.top_k(score, n_sel)
    tok = (sel[..., None] * SLC_BLOCK + jnp.arange(SLC_BLOCK)).reshape(tq, B_KV_HEADS, n_sel * SLC_BLOCK)
    valid = tok <= qpos[:, None, None]
    kv_sel = fetch_slc(tok, jnp.arange(B_KV_HEADS)[None, :, None])
    s = jnp.einsum('tgrd,tgkd->tgrk', qg, kv_sel[..., 0, :]) * scale
    p = masked_softmax(s, valid[:, :, None, :])
    o_s = jnp.einsum('tgrk,tgkd->tgrd', p.astype(kv_sel.dtype), kv_sel[..., 1, :])
    s = jnp.einsum('tgrd,kgd->tgrk', qg, win_kv[:, :, 0]) * scale
    wmask = (win_pos[None, :] <= qpos[:, None]) & (win_pos[None, :] > qpos[:, None] - WINDOW) & (win_pos[None, :] >= 0)
    p = masked_softmax(s, wmask[:, None, None, :])
    o_w = jnp.einsum('tgrk,kgd->tgrd', p.astype(win_kv.dtype), win_kv[:, :, 1])
    g = jax.nn.sigmoid(gates.astype(jnp.float32)).astype(q.dtype).reshape(tq, B_KV_HEADS, r, 3)
    o = g[..., 0:1] * o_c + g[..., 1:2] * o_s + g[..., 2:3] * o_w
    return o.reshape(tq, B_HEADS * HEAD_DIM)


def paged_fetch(pool, pt, new, idx, g):
    shape = jnp.broadcast_shapes(idx.shape, g.shape)
    ic = jnp.clip(idx, 0, PAST_LEN - 1)
    past_rows = pool[pt[ic // PAGE_SIZE], ic % PAGE_SIZE, g]
    new_rows = new[jnp.clip(idx - PAST_LEN, 0, DEC_SEQ - 1), g]
    past = jnp.broadcast_to(idx < PAST_LEN, shape)[..., None, None]
    return jnp.where(past, past_rows, new_rows)


def prompt_mix(m, w_phi, pe_phi):
    nqb = SEQ // Q_BLOCK
    kc, vc = jax.vmap(lambda kv: compress(kv, w_phi, pe_phi))(m["kv_c"])
    c_end = cmp_ends(SEQ)
    kvw_pad = jnp.pad(m["kv_w"], ((0, 0), (WINDOW, 0), (0, 0), (0, 0), (0, 0)))

    def item(i):
        b = i // nqb
        q0 = (i % nqb) * Q_BLOCK
        qpos = q0 + jnp.arange(Q_BLOCK)

        def blk(a):
            return lax.dynamic_slice_in_dim(a[b], q0, Q_BLOCK, axis=0)

        kv_a_b = m["kv_a"][b]
        kv_s_b = m["kv_s"][b]
        o_a = dsa_attend(blk(m["q_a"]), blk(m["q_i"]), blk(m["w_i"]), qpos, m["k_i"][b],
                         lambda idx, g: kv_a_b[jnp.clip(idx, 0, SEQ - 1), g], SEQ)
        win = lax.dynamic_slice_in_dim(kvw_pad[b], q0, WINDOW + Q_BLOCK, axis=0)
        win_pos = q0 - WINDOW + jnp.arange(WINDOW + Q_BLOCK)
        o_b = nsa_attend(blk(m["q_b"]), blk(m["g_n"]), qpos, kc[b], vc[b], c_end,
                         lambda idx, g: kv_s_b[jnp.clip(idx, 0, SEQ - 1), g], SEQ, win, win_pos)
        return o_a, o_b

    o_a, o_b = lax.map(item, jnp.arange(BATCH * nqb))
    return o_a.reshape(BATCH, SEQ, -1), o_b.reshape(BATCH, SEQ, -1)


def sample_mix(m, pool_kv_a, pool_kidx, pool_kv_c, pool_kv_s, win_state, page_table, w_phi, pe_phi):
    L = PAST_LEN + DEC_SEQ
    wb = win_state.shape[1]
    qpos = PAST_LEN + jnp.arange(DEC_SEQ)
    win_pos = PAST_LEN - wb + jnp.arange(wb + DEC_SEQ)
    c_end = cmp_ends(L)

    def item(a):
        pt, q_a, q_i, w_i, kv_a, k_i, q_b, g_n, kv_c, kv_s, kv_w, ws = a
        kidx = jnp.concatenate([pool_kidx[pt].reshape(PAST_LEN, IDX_DIM), k_i], axis=0)
        o_a = dsa_attend(q_a, q_i, w_i, qpos, kidx,
                         lambda idx, g: paged_fetch(pool_kv_a, pt, kv_a, idx, g), L)
        kvc_full = jnp.concatenate([pool_kv_c[pt].reshape(PAST_LEN, B_KV_HEADS, 2, HEAD_DIM), kv_c], axis=0)
        kc, vc = compress(kvc_full, w_phi, pe_phi)
        win = jnp.concatenate([ws, kv_w], axis=0)
        o_b = nsa_attend(q_b, g_n, qpos, kc, vc, c_end,
                         lambda idx, g: paged_fetch(pool_kv_s, pt, kv_s, idx, g), L, win, win_pos)
        return o_a, o_b, win[DEC_SEQ:]

    xs = (page_table, m["q_a"], m["q_i"], m["w_i"], m["kv_a"], m["k_i"], m["q_b"], m["g_n"],
          m["kv_c"], m["kv_s"], m["kv_w"], win_state)
    return lax.map(item, xs)


def merge_branches(h, o_a, o_b, g_m, w_o_a, w_o_b, w_out):
    g_a, g_b = jnp.split(jax.nn.sigmoid(g_m.astype(jnp.float32)).astype(h.dtype), 2, axis=-1)
    return h + (g_a * (o_a @ w_o_a) + g_b * (o_b @ w_o_b)) @ w_out


def hier_moe(h, gain, w_rg, b_rg, w_re, b_re, w_gate, w_up, w_down):
    xn = rmsnorm(h, gain).reshape(-1, D_MODEL)
    n = xn.shape[0]
    rows = jnp.arange(n)
    lg = (xn @ w_rg).astype(jnp.float32) + b_rg.astype(jnp.float32)
    p_group = jax.nn.softmax(lg, axis=-1)
    g_sel = jnp.argmax(lg, axis=-1)
    le = ((xn @ w_re).astype(jnp.float32) + b_re.astype(jnp.float32)).reshape(n, N_GROUPS, EXPERTS_PER_GROUP)
    p_exp = jax.nn.softmax(le[rows, g_sel], axis=-1)
    top_p, top_i = lax.top_k(p_exp, TOPK_IN_GROUP)
    wts = p_group[rows, g_sel][:, None] * top_p / jnp.sum(top_p, axis=-1, keepdims=True)
    expert = g_sel[:, None] * EXPERTS_PER_GROUP + top_i
    comb = jnp.einsum('nk,nke->ne', wts, jax.nn.one_hot(expert, N_EXPERTS, dtype=jnp.float32))
    hid = jax.nn.silu(jnp.einsum('nd,edf->nef', xn, w_gate)) * jnp.einsum('nd,edf->nef', xn, w_up)
    out = jnp.einsum('nef,efd->nd', hid * comb[..., None].astype(hid.dtype), w_down)
    return h + out.reshape(h.shape)


def setup_inputs(seed: int = 0) -> dict:
    key = jax.random.key(seed)
    ks = iter(jax.random.split(key, 32))

    def nrm(shape, scale):
        return jax.random.normal(next(ks), shape, jnp.float32) * scale

    n_pages = PAST_LEN // PAGE_SIZE
    used = DEC_BATCH * n_pages
    n_pool = used + max(1, used // 4)
    page_table = jax.random.permutation(next(ks), n_pool)[:used].reshape(DEC_BATCH, n_pages).astype(jnp.int32)
    wb = min(WINDOW, PAST_LEN)
    return {
        "x_prompt": nrm((BATCH, SEQ, D_MODEL), 1.0),
        "x_sample": nrm((DEC_BATCH, DEC_SEQ, D_MODEL), 1.0),
        "cache_kv_a": nrm((DEPTH, n_pool, PAGE_SIZE, A_KV_HEADS, 2, HEAD_DIM), 1.0),
        "cache_kidx": nrm((DEPTH, n_pool, PAGE_SIZE, IDX_DIM), 1.0),
        "cache_kv_cmp": nrm((DEPTH, n_pool, PAGE_SIZE, B_KV_HEADS, 2, HEAD_DIM), 1.0),
        "cache_kv_slc": nrm((DEPTH, n_pool, PAGE_SIZE, B_KV_HEADS, 2, HEAD_DIM), 1.0),
        "state_kv_win": nrm((DEPTH, DEC_BATCH, wb, B_KV_HEADS, 2, HEAD_DIM), 1.0),
        "page_table": page_table,
        "norm_mix": 1.0 + nrm((DEPTH, D_MODEL), 0.01),
        "w_in": nrm((DEPTH, D_MODEL, IN_WIDTH), D_MODEL ** -0.5),
        "w_phi": nrm((DEPTH, 2, CMP_BLOCK, HEAD_DIM, HEAD_DIM), (CMP_BLOCK * HEAD_DIM) ** -0.5),
        "pe_phi": nrm((DEPTH, CMP_BLOCK, 2, HEAD_DIM), 0.1),
        "w_o_a": nrm((DEPTH, A_HEADS * HEAD_DIM, D_MODEL), (A_HEADS * HEAD_DIM) ** -0.5),
        "w_o_b": nrm((DEPTH, B_HEADS * HEAD_DIM, D_MODEL), (B_HEADS * HEAD_DIM) ** -0.5),
        "w_out": nrm((DEPTH, D_MODEL, D_MODEL), D_MODEL ** -0.5),
        "norm_ffn": 1.0 + nrm((DEPTH, D_MODEL), 0.01),
        "w_router_group": nrm((DEPTH, D_MODEL, N_GROUPS), D_MODEL ** -0.5),
        "b_router_group": nrm((DEPTH, N_GROUPS), 0.01),
        "w_router_expert": nrm((DEPTH, D_MODEL, N_EXPERTS), D_MODEL ** -0.5),
        "b_router_expert": nrm((DEPTH, N_EXPERTS), 0.01),
        "w_gate": nrm((DEPTH, N_EXPERTS, D_MODEL, EXPERT_FF), D_MODEL ** -0.5),
        "w_up": nrm((DEPTH, N_EXPERTS, D_MODEL, EXPERT_FF), D_MODEL ** -0.5),
        "w_down": nrm((DEPTH, N_EXPERTS, EXPERT_FF, D_MODEL), EXPERT_FF ** -0.5),
        "norm_final": 1.0 + nrm((D_MODEL,), 0.01),
    }


def reference(x_prompt, x_sample, cache_kv_a, cache_kidx, cache_kv_cmp, cache_kv_slc, state_kv_win, page_table,
              norm_mix, w_in, w_phi, pe_phi, w_o_a, w_o_b, w_out, norm_ffn, w_router_group, b_router_group,
              w_router_expert, b_router_expert, w_gate, w_up, w_down, norm_final):
    pos_p = jnp.arange(SEQ)
    pos_s = PAST_LEN + jnp.arange(DEC_SEQ)
    wb_p = min(WINDOW, SEQ)
    h_p, h_s = x_prompt, x_sample
    new_p, new_s = [], []
    for l in range(DEPTH):
        m_p = mixer_inputs(rmsnorm(h_p, norm_mix[l]), pos_p, w_in[l])
        m_s = mixer_inputs(rmsnorm(h_s, norm_mix[l]), pos_s, w_in[l])
        o_a_p, o_b_p = prompt_mix(m_p, w_phi[l], pe_phi[l])
        o_a_s, o_b_s, win_s = sample_mix(m_s, cache_kv_a[l], cache_kidx[l], cache_kv_cmp[l], cache_kv_slc[l],
                                         state_kv_win[l], page_table, w_phi[l], pe_phi[l])
        h_p = merge_branches(h_p, o_a_p, o_b_p, m_p["g_m"], w_o_a[l], w_o_b[l], w_out[l])
        h_s = merge_branches(h_s, o_a_s, o_b_s, m_s["g_m"], w_o_a[l], w_o_b[l], w_out[l])
        h_p = hier_moe(h_p, norm_ffn[l], w_router_group[l], b_router_group[l], w_router_expert[l],
                       b_router_expert[l], w_gate[l], w_up[l], w_down[l])
        h_s = hier_moe(h_s, norm_ffn[l], w_router_group[l], b_router_group[l], w_router_expert[l],
                       b_router_expert[l], w_gate[l], w_up[l], w_down[l])
        new_p.append((m_p["kv_a"], m_p["k_i"], m_p["kv_c"], m_p["kv_s"], m_p["kv_w"][:, SEQ - wb_p:]))
        new_s.append((m_s["kv_a"], m_s["k_i"], m_s["kv_c"], m_s["kv_s"], win_s))
    y_prompt = rmsnorm(h_p, norm_final)
    y_sample = rmsnorm(h_s, norm_final)

    def stk(lst, j):
        return jnp.stack([t[j] for t in lst], axis=0)

    return (y_prompt, y_sample,
            stk(new_p, 0), stk(new_p, 1), stk(new_p, 2), stk(new_p, 3), stk(new_p, 4),
            stk(new_s, 0), stk(new_s, 1), stk(new_s, 2), stk(new_s, 3), stk(new_s, 4))
```

```python
import functools

import numpy as np
import jax
import jax.numpy as jnp
from jax import lax
from jax.experimental import pallas as pl
from jax.experimental.pallas import tpu as pltpu

F32 = jnp.float32
BF16 = jnp.bfloat16
I32 = jnp.int32

LANES = 128
VMEM_LIMIT = 56 * 1024 * 1024

D_MODEL = 2048
HEAD_DIM = 128
ROPE_DIM = HEAD_DIM // 4
ROPE_THETA = 500000.0
A_HEADS = 8
A_KV_HEADS = 2
IDX_HEADS = 16
IDX_DIM = 64
IDX_ROPE_DIM = IDX_DIM // 4
DSA_TOPK = 256
B_HEADS = 8
B_KV_HEADS = 2
GQA = 4
CMP_BLOCK = 32
CMP_STRIDE = 16
SLC_BLOCK = 64
SLC_TOPN = 16
SLC_LOCAL = 2
WINDOW = 512
PAGE = 128
N_GROUPS = 4
EXPERTS_PER_GROUP = 4
N_EXPERTS = 16
EXPERT_FF = 512
Q_BLOCK = 128
RMS_EPS = 1e-6
NEG = -1e30
FORCE_BONUS = 1e4
ATT_SCALE = HEAD_DIM ** -0.5
IDX_SCALE = (IDX_DIM * IDX_HEADS) ** -0.5
PAGES_PER_STEP = 16

C_GM, C_QA, C_QI, C_QB, C_KVA, C_KVC, C_KVS, C_KVW, C_KIW, C_GN = 0, 32, 40, 48, 56, 60, 64, 68, 72, 73
N_CHUNKS = 76
NP = N_CHUNKS * LANES
K_NONE, K_ROPE_HEAD, K_ROPE_IDX, K_SIGMOID, K_ROPE_IDX_LOW = 0, 1, 2, 3, 4

NT_DIMS = (((1,), (1,)), ((), ()))


def _dot(a, b):
    return jnp.dot(a, b, preferred_element_type=F32)


def _dot_nt(a, b):
    return lax.dot_general(a, b, NT_DIMS, preferred_element_type=F32)


def _split3(x):
    hi = x.astype(BF16)
    r1 = x - hi.astype(F32)
    mid = r1.astype(BF16)
    lo = (r1 - mid.astype(F32)).astype(BF16)
    return hi, mid, lo


def _dot3_exact_rhs(x, rhs_bf16):
    hi, mid, lo = _split3(x)
    return _dot(hi, rhs_bf16) + _dot(mid, rhs_bf16) + _dot(lo, rhs_bf16)


def _dot3_exact_lhs(lhs_bf16, x):
    hi, mid, lo = _split3(x)
    return _dot(lhs_bf16, hi) + _dot(lhs_bf16, mid) + _dot(lhs_bf16, lo)


def _iota(shape, dim):
    return lax.broadcasted_iota(I32, shape, dim)


def _count(mask):
    return jnp.sum(jnp.where(mask, 1.0, 0.0), axis=-1, keepdims=True)


def _topk_mask(score, k, idx_bits):
    s = score + 0.0
    b = pltpu.bitcast(s, I32)
    key = b ^ ((b >> 31) & jnp.int32(0x7FFFFFFF))
    rows = key.shape[0]
    imin = jnp.int32(-2 ** 31)
    kf = jnp.float32(k)

    def value_bit(i, tu):
        cand = tu | lax.shift_left(jnp.int32(1), 31 - i)
        cnt = _count(key >= (cand ^ imin))
        return jnp.where(cnt >= kf, cand, tu)

    tu = lax.fori_loop(0, 32, value_bit, jnp.zeros((rows, 1), I32))
    thr = tu ^ imin
    gt = key > thr
    eq = key == thr
    need = kf - _count(gt)
    idx = _iota(key.shape, 1)

    def index_bit(i, j):
        cand = j | lax.shift_left(jnp.int32(1), idx_bits - 1 - i)
        cnt = _count(eq & (idx < cand))
        return jnp.where(cnt < need, cand, j)

    j = lax.fori_loop(0, idx_bits, index_bit, jnp.zeros((rows, 1), I32))
    return gt | (eq & (idx <= j))


def _masked_softmax_parts(s, mask):
    s = jnp.where(mask, s, NEG)
    m = jnp.max(s, axis=-1, keepdims=True)
    p = jnp.where(mask, jnp.exp(s - m), 0.0)
    return p, jnp.sum(p, axis=-1, keepdims=True)


def _attend_heads(qg, k, v, mask):
    s = _dot_nt(qg, k) * ATT_SCALE
    ps, ls = [], []
    for r in range(GQA):
        p, l = _masked_softmax_parts(s[r * Q_BLOCK:(r + 1) * Q_BLOCK], mask)
        ps.append(p.astype(BF16))
        ls.append(l)
    o = _dot(jnp.concatenate(ps, axis=0), v)
    return o / jnp.maximum(jnp.concatenate(ls, axis=0), 1e-30)


def _rope(y, c, a, b, half):
    return y * c + pltpu.roll(y, LANES - half, 1) * a + pltpu.roll(y, half, 1) * b


def _proj_kernel(kind_ref, x_ref, g_ref, w_ref, tab_ref, o_ref, xn_ref, *, tn):
    j = pl.program_id(1)

    @pl.when(j == 0)
    def _():
        xf = x_ref[...]
        r = lax.rsqrt(jnp.mean(xf * xf, axis=-1, keepdims=True) + RMS_EPS)
        xn_ref[...] = ((xf * r) * g_ref[...]).astype(BF16)

    acc = _dot(xn_ref[...], w_ref[...])
    lane = _iota((1, LANES), 1)
    for c in range(tn // LANES):
        cs = slice(c * LANES, (c + 1) * LANES)
        y = acc[:, cs]
        k = kind_ref[j * (tn // LANES) + c]

        @pl.when(k == K_NONE)
        def _(y=y, cs=cs):
            o_ref[:, cs] = y

        @pl.when(k == K_ROPE_HEAD)
        def _(y=y, cs=cs):
            o_ref[:, cs] = _rope(y, tab_ref[:, 0:128], tab_ref[:, 128:256], tab_ref[:, 256:384], ROPE_DIM // 2)

        @pl.when((k == K_ROPE_IDX) | (k == K_ROPE_IDX_LOW))
        def _(y=y, cs=cs, k=k):
            r = _rope(y, tab_ref[:, 384:512], tab_ref[:, 512:640], tab_ref[:, 640:768], IDX_ROPE_DIM // 2)
            lim = jnp.where(k == K_ROPE_IDX_LOW, IDX_DIM, LANES)
            o_ref[:, cs] = jnp.where(lane < lim, r, y)

        @pl.when(k == K_SIGMOID)
        def _(y=y, cs=cs):
            o_ref[:, cs] = 1.0 / (1.0 + jnp.exp(-y))


def _project(x2d, gain, w_packed, kinds, tables, tm, n_tab_blocks):
    t = x2d.shape[0]
    tn = 512
    grid = (t // tm, NP // tn)
    return pl.pallas_call(
        functools.partial(_proj_kernel, tn=tn),
        grid_spec=pltpu.PrefetchScalarGridSpec(
            num_scalar_prefetch=1, grid=grid,
            in_specs=[
                pl.BlockSpec((tm, D_MODEL), lambda i, j, kr: (i, 0)),
                pl.BlockSpec((1, D_MODEL), lambda i, j, kr: (0, 0)),
                pl.BlockSpec((D_MODEL, tn), lambda i, j, kr: (0, j)),
                pl.BlockSpec((tm, 6 * LANES), lambda i, j, kr: (i % n_tab_blocks, 0)),
            ],
            out_specs=pl.BlockSpec((tm, tn), lambda i, j, kr: (i, j)),
            scratch_shapes=[pltpu.VMEM((tm, D_MODEL), BF16)]),
        out_shape=jax.ShapeDtypeStruct((t, NP), F32),
        compiler_params=pltpu.CompilerParams(
            dimension_semantics=("parallel", "arbitrary"), vmem_limit_bytes=VMEM_LIMIT),
        name="in_proj",
    )(kinds, x2d, gain, w_packed, tables)


def _dsa_prompt_kernel(qa_ref, qi_ref, kiwq_ref, kiwk_ref, kva_ref, o_ref, *, seq):
    q0 = pl.program_id(1) * Q_BLOCK
    lane = _iota((1, LANES), 1)
    kk = kiwk_ref[...]
    k_dup = jnp.where(lane < IDX_DIM, kk, pltpu.roll(kk, IDX_DIM, 1)).astype(BF16)
    wi = kiwq_ref[...]
    score = jnp.zeros((Q_BLOCK, seq), F32)
    for pair in range(IDX_HEADS // 2):
        qp = qi_ref[:, pair * LANES:(pair + 1) * LANES]
        for half in range(2):
            h = 2 * pair + half
            keep = (lane < IDX_DIM) if half == 0 else (lane >= IDX_DIM)
            qm = jnp.where(keep, qp, 0.0).astype(BF16)
            rel = jnp.maximum(_dot_nt(qm, k_dup), 0.0)
            score = score + rel * wi[:, IDX_DIM + h:IDX_DIM + h + 1]
    score = score * IDX_SCALE
    qpos = q0 + _iota((Q_BLOCK, 1), 0)
    kpos = _iota((1, seq), 1)
    causal = kpos <= qpos
    score = jnp.where(causal, score, NEG)
    sel = _topk_mask(score, min(DSA_TOPK, seq // 4), int(np.log2(seq))) & causal

    for g in range(A_KV_HEADS):
        kg = kva_ref[:, (2 * g) * LANES:(2 * g + 1) * LANES].astype(BF16)
        vg = kva_ref[:, (2 * g + 1) * LANES:(2 * g + 2) * LANES].astype(BF16)
        qg = jnp.concatenate(
            [qa_ref[:, (g * GQA + r) * LANES:(g * GQA + r + 1) * LANES] for r in range(GQA)], axis=0).astype(BF16)
        o = _attend_heads(qg, kg, vg, sel)
        for r in range(GQA):
            o_ref[:, (g * GQA + r) * LANES:(g * GQA + r + 1) * LANES] = o[r * Q_BLOCK:(r + 1) * Q_BLOCK]


def _dsa_prompt(y, batch, seq):
    nqb = seq // Q_BLOCK
    return pl.pallas_call(
        functools.partial(_dsa_prompt_kernel, seq=seq),
        grid=(batch, nqb),
        in_specs=[
            pl.BlockSpec((Q_BLOCK, 8 * LANES), lambda b, i: (b * nqb + i, C_QA // 8)),
            pl.BlockSpec((Q_BLOCK, 8 * LANES), lambda b, i: (b * nqb + i, C_QI // 8)),
            pl.BlockSpec((Q_BLOCK, LANES), lambda b, i: (b * nqb + i, C_KIW)),
            pl.BlockSpec((seq, LANES), lambda b, i: (b, C_KIW)),
            pl.BlockSpec((seq, 4 * LANES), lambda b, i: (b, C_KVA // 4)),
        ],
        out_specs=pl.BlockSpec((Q_BLOCK, 8 * LANES), lambda b, i: (b * nqb + i, 0)),
        out_shape=jax.ShapeDtypeStruct((batch * seq, A_HEADS * HEAD_DIM), F32),
        compiler_params=pltpu.CompilerParams(
            dimension_semantics=("parallel", "arbitrary"), vmem_limit_bytes=VMEM_LIMIT),
        name="dsa_prompt",
    )(y, y, y, y, y)


def _pe_bias(pe_ref, w_ref, c):
    pb = _dot(pe_ref[c].astype(BF16), w_ref[c])
    return pb[0:1, 0:LANES] + pb[1:2, LANES:2 * LANES]


def _compress_prompt_kernel(kvc00_ref, kvc01_ref, kvc10_ref, kvc11_ref, w_ref, pe_ref, tab_ref, o_ref, *, nsub):
    kvc = ((kvc00_ref, kvc01_ref), (kvc10_ref, kvc11_ref))
    for c in range(2):
        xs = []
        for g in range(B_KV_HEADS):
            xs.append(jnp.concatenate(
                [kvc[g][c][pl.ds(l, nsub, stride=CMP_STRIDE), :] for l in range(CMP_STRIDE)], axis=1))
        pq = _dot(jnp.concatenate(xs, axis=0).astype(BF16), w_ref[c])
        bias = _pe_bias(pe_ref, w_ref, c)
        for g in range(B_KV_HEADS):
            top = pq[g * nsub:(g + 1) * nsub, 0:LANES]
            bot = pq[g * nsub:(g + 1) * nsub, LANES:2 * LANES]
            comp = top + pltpu.roll(bot, nsub - 1, 0) + bias
            if c == 0:
                comp = _rope(comp, tab_ref[:, 0:128], tab_ref[:, 128:256], tab_ref[:, 256:384], ROPE_DIM // 2)
            o_ref[:, (2 * g + c) * LANES:(2 * g + c + 1) * LANES] = comp


def _compress_prompt(y, w_cat, pe_r, tab_c, batch, seq):
    nsub = seq // CMP_STRIDE
    return pl.pallas_call(
        functools.partial(_compress_prompt_kernel, nsub=nsub),
        grid=(batch,),
        in_specs=[
            pl.BlockSpec((seq, LANES), lambda b: (b, C_KVC)),
            pl.BlockSpec((seq, LANES), lambda b: (b, C_KVC + 1)),
            pl.BlockSpec((seq, LANES), lambda b: (b, C_KVC + 2)),
            pl.BlockSpec((seq, LANES), lambda b: (b, C_KVC + 3)),
            pl.BlockSpec((2, CMP_STRIDE * HEAD_DIM, 2 * LANES), lambda b: (0, 0, 0)),
            pl.BlockSpec((2, 8, CMP_STRIDE * HEAD_DIM), lambda b: (0, 0, 0)),
            pl.BlockSpec((nsub, 3 * LANES), lambda b: (0, 0)),
        ],
        out_specs=pl.BlockSpec((nsub, 4 * LANES), lambda b: (b, 0)),
        out_shape=jax.ShapeDtypeStruct((batch * nsub, 4 * LANES), F32),
        compiler_params=pltpu.CompilerParams(dimension_semantics=("parallel",), vmem_limit_bytes=VMEM_LIMIT),
        name="compress_prompt",
    )(y, y, y, y, w_cat, pe_r, tab_c)


def _cover(n_rows, n_cols, n_cmp, n_slc):
    ci = _iota((n_rows, n_cols), 0) * CMP_STRIDE
    sj = _iota((n_rows, n_cols), 1) * SLC_BLOCK
    m = (ci < sj + SLC_BLOCK) & (ci + CMP_BLOCK > sj)
    m = m & (_iota((n_rows, n_cols), 0) < n_cmp) & (_iota((n_rows, n_cols), 1) < n_slc)
    return jnp.where(m, 1.0, 0.0).astype(BF16)


def _block_scores(p_s, qpos_col, n_slc):
    blk = _iota(p_s.shape, 1)
    cur = qpos_col >> 6
    forced = (blk == 0) | ((blk <= cur) & (blk > cur - SLC_LOCAL))
    score = jnp.where(blk <= cur, p_s + FORCE_BONUS * jnp.where(forced, 1.0, 0.0), NEG)
    return jnp.where(blk < n_slc, score, -jnp.inf)


def _nsa_prompt_kernel(qb_ref, gn_ref, cmp_ref, kvs_ref, kvw_ref, o_ref, *, seq):
    i = pl.program_id(1)
    q0 = i * Q_BLOCK
    nsub = seq // CMP_STRIDE
    n_cmp = (seq - CMP_BLOCK) // CMP_STRIDE + 1
    n_slc = -(-seq // SLC_BLOCK)
    qpos = q0 + _iota((Q_BLOCK, 1), 0)
    c_end = _iota((1, nsub), 1) * CMP_STRIDE + (CMP_BLOCK - 1)
    cmask = c_end <= qpos
    cover = _cover(nsub, LANES, n_cmp, n_slc)
    kpos = _iota((1, seq), 1)
    expand = jnp.where((_iota((LANES, seq), 1) >> 6) == _iota((LANES, seq), 0), 1.0, 0.0).astype(BF16)
    gates = gn_ref[...]

    for g in range(B_KV_HEADS):
        qg = jnp.concatenate(
            [qb_ref[:, (g * GQA + r) * LANES:(g * GQA + r + 1) * LANES] for r in range(GQA)], axis=0).astype(BF16)
        kc = cmp_ref[:, (2 * g) * LANES:(2 * g + 1) * LANES].astype(BF16)
        vc = cmp_ref[:, (2 * g + 1) * LANES:(2 * g + 2) * LANES].astype(BF16)
        s = _dot_nt(qg, kc) * ATT_SCALE
        ps = []
        for r in range(GQA):
            p, l = _masked_softmax_parts(s[r * Q_BLOCK:(r + 1) * Q_BLOCK], cmask)
            ps.append(p / jnp.maximum(l, 1e-30))
        o_c = _dot(jnp.concatenate(ps, axis=0).astype(BF16), vc)
        p_s = _dot3_exact_rhs(ps[0] + ps[1] + ps[2] + ps[3], cover)
        sel = _topk_mask(_block_scores(p_s, qpos, n_slc), min(SLC_TOPN, n_slc), 7)
        tok = (_dot(jnp.where(sel, 1.0, 0.0).astype(BF16), expand) > 0.5) & (kpos <= qpos)
        ks = kvs_ref[:, (2 * g) * LANES:(2 * g + 1) * LANES].astype(BF16)
        vs = kvs_ref[:, (2 * g + 1) * LANES:(2 * g + 2) * LANES].astype(BF16)
        o_s = _attend_heads(qg, ks, vs, tok)
        nwb = WINDOW // Q_BLOCK + 1
        k_parts, v_parts, m_parts = [], [], []
        for w in range(nwb):
            kb = i - (nwb - 1) + w
            start = pl.multiple_of(jnp.maximum(kb, 0) * Q_BLOCK, Q_BLOCK)
            k_parts.append(kvw_ref[pl.ds(start, Q_BLOCK), (2 * g) * LANES:(2 * g + 1) * LANES].astype(BF16))
            v_parts.append(kvw_ref[pl.ds(start, Q_BLOCK), (2 * g + 1) * LANES:(2 * g + 2) * LANES].astype(BF16))
            wpos = kb * Q_BLOCK + _iota((1, Q_BLOCK), 1)
            m_parts.append(jnp.where((wpos <= qpos) & (wpos > qpos - WINDOW) & (wpos >= 0), 1.0, 0.0))
        wmask = jnp.concatenate(m_parts, axis=1) > 0.5
        o_w = _attend_heads(qg, jnp.concatenate(k_parts, axis=0), jnp.concatenate(v_parts, axis=0), wmask)
        for r in range(GQA):
            h = g * GQA + r
            rs = slice(r * Q_BLOCK, (r + 1) * Q_BLOCK)
            o_ref[:, h * LANES:(h + 1) * LANES] = (gates[:, 3 * h:3 * h + 1] * o_c[rs]
                                                   + gates[:, 3 * h + 1:3 * h + 2] * o_s[rs]
                                                   + gates[:, 3 * h + 2:3 * h + 3] * o_w[rs])


def _nsa_prompt(y, cmp, batch, seq):
    nqb = seq // Q_BLOCK
    nsub = seq // CMP_STRIDE
    return pl.pallas_call(
        functools.partial(_nsa_prompt_kernel, seq=seq),
        grid=(batch, nqb),
        in_specs=[
            pl.BlockSpec((Q_BLOCK, 8 * LANES), lambda b, i: (b * nqb + i, C_QB // 8)),
            pl.BlockSpec((Q_BLOCK, LANES), lambda b, i: (b * nqb + i, C_GN)),
            pl.BlockSpec((nsub, 4 * LANES), lambda b, i: (b, 0)),
            pl.BlockSpec((seq, 4 * LANES), lambda b, i: (b, C_KVS // 4)),
            pl.BlockSpec((seq, 4 * LANES), lambda b, i: (b, C_KVW // 4)),
        ],
        out_specs=pl.BlockSpec((Q_BLOCK, 8 * LANES), lambda b, i: (b * nqb + i, 0)),
        out_shape=jax.ShapeDtypeStruct((batch * seq, B_HEADS * HEAD_DIM), F32),
        compiler_params=pltpu.CompilerParams(
            dimension_semantics=("parallel", "arbitrary"), vmem_limit_bytes=VMEM_LIMIT),
        name="nsa_prompt",
    )(y, y, cmp, y, y)


def _merge_kernel(x_ref, ga_ref, gb_ref, oa_ref, ob_ref, woa_ref, wob_ref, wout_ref, o_ref):
    u = (ga_ref[...] * _dot(oa_ref[...].astype(BF16), woa_ref[...])
         + gb_ref[...] * _dot(ob_ref[...].astype(BF16), wob_ref[...]))
    o_ref[...] = x_ref[...] + _dot(u.astype(BF16), wout_ref[...])


def _merge(x2d, y, o_a, o_b, w_o_a, w_o_b, w_out, tm):
    t = x2d.shape[0]
    const = dict(pipeline_mode=pl.Buffered(1))
    return pl.pallas_call(
        _merge_kernel,
        grid=(t // tm,),
        in_specs=[
            pl.BlockSpec((tm, D_MODEL), lambda i: (i, 0)),
            pl.BlockSpec((tm, D_MODEL), lambda i: (i, 0)),
            pl.BlockSpec((tm, D_MODEL), lambda i: (i, 1)),
            pl.BlockSpec((tm, A_HEADS * HEAD_DIM), lambda i: (i, 0)),
            pl.BlockSpec((tm, B_HEADS * HEAD_DIM), lambda i: (i, 0)),
            pl.BlockSpec((A_HEADS * HEAD_DIM, D_MODEL), lambda i: (0, 0), **const),
            pl.BlockSpec((B_HEADS * HEAD_DIM, D_MODEL), lambda i: (0, 0), **const),
            pl.BlockSpec((D_MODEL, D_MODEL), lambda i: (0, 0), **const),
        ],
        out_specs=pl.BlockSpec((tm, D_MODEL), lambda i: (i, 0)),
        out_shape=jax.ShapeDtypeStruct((t, D_MODEL), F32),
        compiler_params=pltpu.CompilerParams(dimension_semantics=("parallel",), vmem_limit_bytes=VMEM_LIMIT),
        name="merge",
    )(x2d, y, y, o_a, o_b, w_o_a, w_o_b, w_out)


def _router_kernel(h_ref, g_ref, whi_ref, wlo_ref, b_ref, xn_ref, comb_ref):
    hf = h_ref[...]
    r = lax.rsqrt(jnp.mean(hf * hf, axis=-1, keepdims=True) + RMS_EPS)
    xn = (hf * r) * g_ref[...]
    xhi = xn.astype(BF16)
    xn_ref[...] = xhi
    xlo = (xn - xhi.astype(F32)).astype(BF16)
    logits = _dot(xhi, whi_ref[...]) + _dot(xlo, whi_ref[...]) + _dot(xhi, wlo_ref[...]) + b_ref[...]
    lane_i = _iota(logits.shape, 1)
    lane = lane_i.astype(F32)
    big = jnp.float32(1e9)
    gm = (lane_i >= N_EXPERTS) & (lane_i < N_EXPERTS + N_GROUPS)
    gmax = jnp.max(jnp.where(gm, logits, -jnp.inf), axis=-1, keepdims=True)
    gsum = jnp.sum(jnp.where(gm, jnp.exp(logits - gmax), 0.0), axis=-1, keepdims=True)
    p_group = 1.0 / gsum
    g_sel = jnp.min(jnp.where(gm & (logits == gmax), lane, big), axis=-1, keepdims=True) - N_EXPERTS
    em = (lane_i < N_EXPERTS) & ((lane_i >> 2).astype(F32) == g_sel)
    emax = jnp.max(jnp.where(em, logits, -jnp.inf), axis=-1, keepdims=True)
    pe = jnp.where(em, jnp.exp(logits - emax), 0.0)
    p_exp = pe / jnp.sum(pe, axis=-1, keepdims=True)
    p1 = jnp.max(jnp.where(em, p_exp, -1.0), axis=-1, keepdims=True)
    i1 = jnp.min(jnp.where(em & (p_exp == p1), lane, big), axis=-1, keepdims=True)
    em2 = em & (lane != i1)
    p2 = jnp.max(jnp.where(em2, p_exp, -1.0), axis=-1, keepdims=True)
    i2 = jnp.min(jnp.where(em2 & (p_exp == p2), lane, big), axis=-1, keepdims=True)
    tot = p1 + p2
    comb_ref[...] = jnp.where(lane == i1, p_group * p1 / tot, jnp.where(lane == i2, p_group * p2 / tot, 0.0))


def _router(h2d, gain, w_hi, w_lo, bias, tm):
    t = h2d.shape[0]
    return pl.pallas_call(
        _router_kernel,
        grid=(t // tm,),
        in_specs=[
            pl.BlockSpec((tm, D_MODEL), lambda i: (i, 0)),
            pl.BlockSpec((1, D_MODEL), lambda i: (0, 0)),
            pl.BlockSpec((D_MODEL, LANES), lambda i: (0, 0)),
            pl.BlockSpec((D_MODEL, LANES), lambda i: (0, 0)),
            pl.BlockSpec((1, LANES), lambda i: (0, 0)),
        ],
        out_specs=[pl.BlockSpec((tm, D_MODEL), lambda i: (i, 0)), pl.BlockSpec((tm, LANES), lambda i: (i, 0))],
        out_shape=[jax.ShapeDtypeStruct((t, D_MODEL), BF16), jax.ShapeDtypeStruct((t, LANES), F32)],
        compiler_params=pltpu.CompilerParams(dimension_semantics=("parallel",), vmem_limit_bytes=VMEM_LIMIT),
        name="router",
    )(h2d, gain, w_hi, w_lo, bias)


def _moe_kernel(xn_ref, comb_ref, h_ref, wgu_ref, wd_ref, gf_ref, o_ref, acc_ref):
    e = pl.program_id(1)

    @pl.when(e == 0)
    def _():
        acc_ref[...] = jnp.zeros_like(acc_ref)

    gu = _dot(xn_ref[...], wgu_ref[0])
    gate = gu[:, :EXPERT_FF]
    hid = (gate * (1.0 / (1.0 + jnp.exp(-gate)))) * gu[:, EXPERT_FF:]
    comb = comb_ref[...]
    ce = jnp.sum(jnp.where(_iota(comb.shape, 1) == e, comb, 0.0), axis=-1, keepdims=True)
    acc_ref[...] += _dot((hid * ce).astype(BF16), wd_ref[0])

    @pl.when(e == N_EXPERTS - 1)
    def _():
        out = h_ref[...] + acc_ref[...]
        r = lax.rsqrt(jnp.mean(out * out, axis=-1, keepdims=True) + RMS_EPS)
        o_ref[...] = (out * r) * gf_ref[...]


def _moe(xn, comb, h2d, w_gu, w_d, gain_final, tm):
    t = xn.shape[0]
    return pl.pallas_call(
        _moe_kernel,
        grid=(t // tm, N_EXPERTS),
        in_specs=[
            pl.BlockSpec((tm, D_MODEL), lambda i, e: (i, 0)),
            pl.BlockSpec((tm, LANES), lambda i, e: (i, 0)),
            pl.BlockSpec((tm, D_MODEL), lambda i, e: (i, 0)),
            pl.BlockSpec((1, D_MODEL, 2 * EXPERT_FF), lambda i, e: (e, 0, 0)),
            pl.BlockSpec((1, EXPERT_FF, D_MODEL), lambda i, e: (e, 0, 0)),
            pl.BlockSpec((1, D_MODEL), lambda i, e: (0, 0)),
        ],
        out_specs=pl.BlockSpec((tm, D_MODEL), lambda i, e: (i, 0)),
        out_shape=jax.ShapeDtypeStruct((t, D_MODEL), F32),
        scratch_shapes=[pltpu.VMEM((tm, D_MODEL), F32)],
        compiler_params=pltpu.CompilerParams(
            dimension_semantics=("parallel", "arbitrary"), vmem_limit_bytes=VMEM_LIMIT),
        name="moe",
    )(xn, comb, h2d, w_gu, w_d, gain_final)


def _page_specs(block, n_chunk_pages, row_of):
    def make(k):
        def index_map(s, c, pt):
            return (pt[s, c * n_chunk_pages + k],) + row_of
        return pl.BlockSpec(block, index_map)
    return [make(k) for k in range(n_chunk_pages)]


def _idx_sample_kernel(pt_ref, q_ref, w_ref, knew_ref, *rest, past):
    pages, (o_ref, onew_ref) = rest[:PAGES_PER_STEP], rest[PAGES_PER_STEP:]
    c = pl.program_id(1)
    q = q_ref[...].astype(BF16)
    w = w_ref[...]

    def scores(kt):
        rel = jnp.maximum(_dot(q, kt.astype(BF16)), 0.0) * w
        return jnp.sum(rel.reshape(IDX_HEADS, 8, kt.shape[1]), axis=0) * IDX_SCALE

    kt = jnp.concatenate([jnp.concatenate([p[0], p[0]], axis=0) for p in pages], axis=1)
    o_ref[...] = scores(kt)

    @pl.when(c == 0)
    def _():
        t = _iota((8, LANES), 0) & 3
        j = _iota((8, LANES), 1)
        onew_ref[...] = jnp.where(j <= t, scores(knew_ref[...]), NEG)


def _idx_sample(pt, q_rows, w_col, knew_t, kidx_t):
    ns, n_pages = pt.shape
    n_chunks = n_pages // PAGES_PER_STEP
    past = n_pages * PAGE
    width = PAGES_PER_STEP * PAGE
    return pl.pallas_call(
        functools.partial(_idx_sample_kernel, past=past),
        grid_spec=pltpu.PrefetchScalarGridSpec(
            num_scalar_prefetch=1, grid=(ns, n_chunks),
            in_specs=[
                pl.BlockSpec((None, LANES, LANES), lambda s, c, pt: (s, 0, 0)),
                pl.BlockSpec((None, LANES, 1), lambda s, c, pt: (s, 0, 0)),
                pl.BlockSpec((None, LANES, LANES), lambda s, c, pt: (s, 0, 0)),
            ] + _page_specs((1, IDX_DIM, PAGE), PAGES_PER_STEP, (0, 0)),
            out_specs=[pl.BlockSpec((None, 8, width), lambda s, c, pt: (s, 0, c)),
                       pl.BlockSpec((None, 8, LANES), lambda s, c, pt: (s, 0, 0))]),
        out_shape=[jax.ShapeDtypeStruct((ns, 8, past), F32), jax.ShapeDtypeStruct((ns, 8, LANES), F32)],
        compiler_params=pltpu.CompilerParams(
            dimension_semantics=("parallel", "arbitrary"), vmem_limit_bytes=VMEM_LIMIT),
        name="idx_sample",
    )(pt, q_rows, w_col, knew_t, *([kidx_t] * PAGES_PER_STEP))


def _topk_sample_kernel(s_ref, snew_ref, m_ref, mnew_ref, *, past, k):
    s = jnp.concatenate([s_ref[...], snew_ref[...]], axis=1)
    sel = _topk_mask(s, k, int(np.ceil(np.log2(past + LANES))))
    sel = jnp.where(sel, 1.0, 0.0)
    m_ref[...] = sel[:, :past]
    mnew_ref[...] = sel[:, past:]


def _topk_sample(scores, scores_new, k):
    ns, _, past = scores.shape
    return pl.pallas_call(
        functools.partial(_topk_sample_kernel, past=past, k=k),
        grid=(ns,),
        in_specs=[pl.BlockSpec((None, 8, past), lambda s: (s, 0, 0)),
                  pl.BlockSpec((None, 8, LANES), lambda s: (s, 0, 0))],
        out_specs=[pl.BlockSpec((None, 8, past), lambda s: (s, 0, 0)),
                   pl.BlockSpec((None, 8, LANES), lambda s: (s, 0, 0))],
        out_shape=[jax.ShapeDtypeStruct((ns, 8, past), F32), jax.ShapeDtypeStruct((ns, 8, LANES), F32)],
        compiler_params=pltpu.CompilerParams(dimension_semantics=("parallel",), vmem_limit_bytes=VMEM_LIMIT),
        name="topk_sample",
    )(scores, scores_new)


def _rows16_from_rows8(g):
    return jnp.where(_iota((16, 8), 1) == (g * 4 + (_iota((16, 8), 0) >> 2)), 1.0, 0.0).astype(BF16)


def _paged_attn_kernel(pt_ref, q_ref, knew_ref, vnew_ref, m_ref, mnew_ref, *rest, block_mask):
    pages, (o_ref,), (m_sc, l_sc, acc_sc) = rest[:PAGES_PER_STEP], rest[PAGES_PER_STEP:PAGES_PER_STEP + 1], \
        rest[PAGES_PER_STEP + 1:]
    c = pl.program_id(1)
    width = PAGES_PER_STEP * PAGE

    @pl.when(c == 0)
    def _():
        m_sc[...] = jnp.full_like(m_sc, NEG)
        l_sc[...] = jnp.zeros_like(l_sc)
        acc_sc[...] = jnp.zeros_like(acc_sc)

    if block_mask:
        nb = m_ref.shape[1]
        first = c * (width // SLC_BLOCK)
        expand = jnp.where(_iota((nb, width), 0) == first + (_iota((nb, width), 1) >> 6), 1.0, 0.0).astype(BF16)
        mask8 = _dot(m_ref[...].astype(BF16), expand)
    else:
        mask8 = m_ref[...]
    mask8 = mask8.astype(BF16)

    def update(g, kg, vg, mask16):
        rs = slice(g * 16, (g + 1) * 16)
        s = _dot_nt(q_ref[g].astype(BF16), kg) * ATT_SCALE
        s = jnp.where(mask16, s, NEG)
        m_old = m_sc[rs]
        m_new = jnp.maximum(m_old, jnp.max(s, axis=-1, keepdims=True))
        p = jnp.where(mask16, jnp.exp(s - m_new), 0.0)
        alpha = jnp.exp(m_old - m_new)
        l_sc[rs] = alpha * l_sc[rs] + jnp.sum(p, axis=-1, keepdims=True)
        acc_sc[rs] = alpha * acc_sc[rs] + _dot(p.astype(BF16), vg)
        m_sc[rs] = m_new

    for g in range(2):
        kg = jnp.concatenate([p[0, pl.ds(2 * g, PAGE, stride=4), :] for p in pages], axis=0).astype(BF16)
        vg = jnp.concatenate([p[0, pl.ds(2 * g + 1, PAGE, stride=4), :] for p in pages], axis=0).astype(BF16)
        update(g, kg, vg, _dot(_rows16_from_rows8(g), mask8) > 0.5)

    @pl.when(c == pl.num_programs(1) - 1)
    def _():
        mnew8 = mnew_ref[...].astype(BF16)
        for g in range(2):
            update(g, knew_ref[g].astype(BF16), vnew_ref[g].astype(BF16), _dot(_rows16_from_rows8(g), mnew8) > 0.5)
        o_ref[...] = acc_sc[...] / jnp.maximum(l_sc[...], 1e-30)


def _paged_attn(pt, q_g, knew, vnew, mask, mask_new, pool, block_mask, name):
    ns, n_pages = pt.shape
    n_chunks = n_pages // PAGES_PER_STEP
    width = PAGES_PER_STEP * PAGE
    if block_mask:
        mspec = pl.BlockSpec((None, 8, mask.shape[2]), lambda s, c, pt: (s, 0, 0))
    else:
        mspec = pl.BlockSpec((None, 8, width), lambda s, c, pt: (s, 0, c))
    return pl.pallas_call(
        functools.partial(_paged_attn_kernel, block_mask=block_mask),
        grid_spec=pltpu.PrefetchScalarGridSpec(
            num_scalar_prefetch=1, grid=(ns, n_chunks),
            in_specs=[
                pl.BlockSpec((None, 2, 16, HEAD_DIM), lambda s, c, pt: (s, 0, 0, 0)),
                pl.BlockSpec((None, 2, LANES, HEAD_DIM), lambda s, c, pt: (s, 0, 0, 0)),
                pl.BlockSpec((None, 2, LANES, HEAD_DIM), lambda s, c, pt: (s, 0, 0, 0)),
                mspec,
                pl.BlockSpec((None, 8, LANES), lambda s, c, pt: (s, 0, 0)),
            ] + _page_specs((1, 4 * PAGE, HEAD_DIM), PAGES_PER_STEP, (0, 0)),
            out_specs=pl.BlockSpec((None, 32, HEAD_DIM), lambda s, c, pt: (s, 0, 0)),
            scratch_shapes=[pltpu.VMEM((32, 1), F32), pltpu.VMEM((32, 1), F32), pltpu.VMEM((32, HEAD_DIM), F32)]),
        out_shape=jax.ShapeDtypeStruct((ns, 32, HEAD_DIM), F32),
        compiler_params=pltpu.CompilerParams(
            dimension_semantics=("parallel", "arbitrary"), vmem_limit_bytes=VMEM_LIMIT),
        name=name,
    )(pt, q_g, knew, vnew, mask, mask_new, *([pool] * PAGES_PER_STEP))


def _compress_sample_kernel(pt_ref, w_ref, *rest):
    pages, (o_ref,) = rest[:PAGES_PER_STEP], rest[PAGES_PER_STEP:]
    sub = PAGE // CMP_STRIDE
    for c in range(2):
        xs = []
        for g in range(B_KV_HEADS):
            per_page = []
            for p in pages:
                per_page.append(jnp.concatenate(
                    [p[0, pl.ds(4 * l + 2 * g + c, sub, stride=4 * CMP_STRIDE), :] for l in range(CMP_STRIDE)],
                    axis=1))
            xs.append(jnp.concatenate(per_page, axis=0))
        pq = _dot(jnp.concatenate(xs, axis=0).astype(BF16), w_ref[c])
        n = PAGES_PER_STEP * sub
        for g in range(B_KV_HEADS):
            o_ref[:, (2 * g + c) * 2 * LANES:(2 * g + c + 1) * 2 * LANES] = pq[g * n:(g + 1) * n]


def _compress_sample(pt, w_cat, pool):
    ns, n_pages = pt.shape
    n_chunks = n_pages // PAGES_PER_STEP
    rows = PAGES_PER_STEP * (PAGE // CMP_STRIDE)
    return pl.pallas_call(
        _compress_sample_kernel,
        grid_spec=pltpu.PrefetchScalarGridSpec(
            num_scalar_prefetch=1, grid=(ns, n_chunks),
            in_specs=[pl.BlockSpec((2, CMP_STRIDE * HEAD_DIM, 2 * LANES), lambda s, c, pt: (0, 0, 0))]
            + _page_specs((1, 4 * PAGE, HEAD_DIM), PAGES_PER_STEP, (0, 0)),
            out_specs=pl.BlockSpec((None, rows, 8 * LANES), lambda s, c, pt: (s, c, 0))),
        out_shape=jax.ShapeDtypeStruct((ns, n_chunks * rows, 8 * LANES), F32),
        compiler_params=pltpu.CompilerParams(
            dimension_semantics=("parallel", "arbitrary"), vmem_limit_bytes=VMEM_LIMIT),
        name="compress_sample",
    )(pt, w_cat, *([pool] * PAGES_PER_STEP))


def _nsa_select_sample_kernel(pq_ref, w_ref, pe_ref, tab_ref, q_ref, oc_ref, bm_ref, *, past, n_blk_pad):
    nsub = past // CMP_STRIDE
    total = past + 4
    n_cmp = (total - CMP_BLOCK) // CMP_STRIDE + 1
    n_slc = -(-total // SLC_BLOCK)
    row_t = _iota((16, 1), 0) >> 2
    c_end = _iota((1, nsub), 1) * CMP_STRIDE + (CMP_BLOCK - 1)
    cmask = (c_end <= past + row_t) & (_iota((1, nsub), 1) < n_cmp)
    p_rows = []
    for g in range(B_KV_HEADS):
        comp = []
        for c in range(2):
            base = (2 * g + c) * 2 * LANES
            v = pq_ref[:, base:base + LANES] + pltpu.roll(pq_ref[:, base + LANES:base + 2 * LANES], nsub - 1, 0)
            v = v + _pe_bias(pe_ref, w_ref, c)
            if c == 0:
                v = _rope(v, tab_ref[:, 0:128], tab_ref[:, 128:256], tab_ref[:, 256:384], ROPE_DIM // 2)
            comp.append(v.astype(BF16))
        s = _dot_nt(q_ref[g].astype(BF16), comp[0]) * ATT_SCALE
        p, l = _masked_softmax_parts(s, cmask)
        p = p / jnp.maximum(l, 1e-30)
        oc_ref[g * 16:(g + 1) * 16, :] = _dot(p.astype(BF16), comp[1])
        p_rows.append(p)
    p_all = jnp.concatenate(p_rows, axis=0)
    gather = jnp.where(_iota((8, 32), 0) == (_iota((8, 32), 1) >> 2), 1.0, 0.0).astype(BF16)
    p_sum = _dot3_exact_lhs(gather, p_all)
    p_s = _dot3_exact_rhs(p_sum, _cover(nsub, n_blk_pad, n_cmp, n_slc))
    qpos = past + (_iota((8, 1), 0) & 3)
    sel = _topk_mask(_block_scores(p_s, qpos, n_slc), min(SLC_TOPN, n_slc), int(np.ceil(np.log2(n_blk_pad))))
    bm_ref[...] = jnp.where(sel, 1.0, 0.0)


def _nsa_select_sample(pq, w_cat, pe_r, tab_c, q_g, past):
    ns, nsub, _ = pq.shape
    n_slc = -(-(past + 4) // SLC_BLOCK)
    n_blk_pad = -(-n_slc // LANES) * LANES
    return pl.pallas_call(
        functools.partial(_nsa_select_sample_kernel, past=past, n_blk_pad=n_blk_pad),
        grid=(ns,),
        in_specs=[
            pl.BlockSpec((None, nsub, 8 * LANES), lambda s: (s, 0, 0)),
            pl.BlockSpec((2, CMP_STRIDE * HEAD_DIM, 2 * LANES), lambda s: (0, 0, 0)),
            pl.BlockSpec((2, 8, CMP_STRIDE * HEAD_DIM), lambda s: (0, 0, 0)),
            pl.BlockSpec((nsub, 3 * LANES), lambda s: (0, 0)),
            pl.BlockSpec((None, 2, 16, HEAD_DIM), lambda s: (s, 0, 0, 0)),
        ],
        out_specs=[pl.BlockSpec((None, 32, HEAD_DIM), lambda s: (s, 0, 0)),
                   pl.BlockSpec((None, 8, n_blk_pad), lambda s: (s, 0, 0))],
        out_shape=[jax.ShapeDtypeStruct((ns, 32, HEAD_DIM), F32), jax.ShapeDtypeStruct((ns, 8, n_blk_pad), F32)],
        compiler_params=pltpu.CompilerParams(dimension_semantics=("parallel",), vmem_limit_bytes=VMEM_LIMIT),
        name="nsa_select_sample",
    )(pq, w_cat, pe_r, tab_c, q_g)


def _window_sample_kernel(win_ref, knew_ref, vnew_ref, q_ref, oc_ref, os_ref, gate_ref, o_ref, *, wb):
    row_t = _iota((16, 1), 0) >> 2
    kpos = _iota((1, wb), 1)
    mask_old = kpos > row_t + (wb - WINDOW)
    mask_new = _iota((1, LANES), 1) <= row_t
    for g in range(B_KV_HEADS):
        rs = slice(g * 16, (g + 1) * 16)
        q = q_ref[g].astype(BF16)
        kw = win_ref[pl.ds(2 * g, wb, stride=4), :].astype(BF16)
        vw = win_ref[pl.ds(2 * g + 1, wb, stride=4), :].astype(BF16)
        s = jnp.concatenate([_dot_nt(q, kw), _dot_nt(q, knew_ref[g].astype(BF16))], axis=1) * ATT_SCALE
        mask = jnp.concatenate([jnp.where(mask_old, 1.0, 0.0), jnp.where(mask_new, 1.0, 0.0)], axis=1) > 0.5
        p, l = _masked_softmax_parts(s, mask)
        o_w = (_dot(p[:, :wb].astype(BF16), vw) + _dot(p[:, wb:].astype(BF16), vnew_ref[g].astype(BF16)))
        o_w = o_w / jnp.maximum(l, 1e-30)
        gt = gate_ref[rs]
        o_ref[rs] = gt[:, 0:1] * oc_ref[rs] + gt[:, 1:2] * os_ref[rs] + gt[:, 2:3] * o_w


def _window_sample(win_rows, knew, vnew, q_g, o_c, o_s, gates, wb):
    ns = q_g.shape[0]
    blk = lambda *shape: pl.BlockSpec((None,) + shape, lambda s: (s,) + (0,) * len(shape))
    return pl.pallas_call(
        functools.partial(_window_sample_kernel, wb=wb),
        grid=(ns,),
        in_specs=[blk(4 * wb, HEAD_DIM), blk(2, LANES, HEAD_DIM), blk(2, LANES, HEAD_DIM), blk(2, 16, HEAD_DIM),
                  blk(32, HEAD_DIM), blk(32, HEAD_DIM), blk(32, LANES)],
        out_specs=blk(32, HEAD_DIM),
        out_shape=jax.ShapeDtypeStruct((ns, 32, HEAD_DIM), F32),
        compiler_params=pltpu.CompilerParams(dimension_semantics=("parallel",), vmem_limit_bytes=VMEM_LIMIT),
        name="window_sample",
    )(win_rows, knew, vnew, q_g, o_c, o_s, gates)


def _rope_table(pos, half, period):
    inv = ROPE_THETA ** (-jnp.arange(half, dtype=F32) / half)
    ang = pos.astype(F32)[:, None] * inv[None, :]
    cos, sin = jnp.cos(ang), jnp.sin(ang)
    lane = np.arange(LANES) % period
    idx = lane % half
    lo = jnp.asarray(lane < half)[None, :]
    hi = jnp.asarray((lane >= half) & (lane < 2 * half))[None, :]
    c = jnp.where(lo | hi, cos[:, idx], 1.0)
    a = jnp.where(lo, -sin[:, idx], 0.0)
    b = jnp.where(hi, sin[:, idx], 0.0)
    return jnp.concatenate([c, a, b], axis=1)


def _proj_tables(pos):
    return jnp.concatenate([_rope_table(pos, ROPE_DIM // 2, LANES), _rope_table(pos, IDX_ROPE_DIM // 2, IDX_DIM)],
                           axis=1)


def _chunk_kinds():
    kinds = np.zeros((N_CHUNKS,), np.int32)
    kinds[C_GM:C_GM + 32] = K_SIGMOID
    kinds[C_QA:C_QA + 8] = K_ROPE_HEAD
    kinds[C_QI:C_QI + 8] = K_ROPE_IDX
    kinds[C_QB:C_QB + 8] = K_ROPE_HEAD
    for base in (C_KVA, C_KVS, C_KVW):
        kinds[base] = K_ROPE_HEAD
        kinds[base + 2] = K_ROPE_HEAD
    kinds[C_KIW] = K_ROPE_IDX_LOW
    kinds[C_GN] = K_SIGMOID
    return jnp.asarray(kinds)


def _pack_w_in(w):
    sizes = [("q_a", 1024), ("kv_a", 512), ("q_i", 1024), ("k_i", 64), ("w_i", 16), ("q_b", 1024), ("kv_c", 512),
             ("kv_s", 512), ("kv_w", 512), ("g_n", 24), ("g_m", 4096)]
    parts, off = {}, 0
    for name, n in sizes:
        parts[name] = w[:, off:off + n]
        off += n
    z = lambda n: jnp.zeros((w.shape[0], n), w.dtype)
    cols = [parts["g_m"], parts["q_a"], parts["q_i"], parts["q_b"], parts["kv_a"], parts["kv_c"], parts["kv_s"],
            parts["kv_w"], parts["k_i"], parts["w_i"], z(48), parts["g_n"], z(104), z(2 * LANES)]
    return jnp.concatenate(cols, axis=1).astype(BF16)


def _cols(y, chunk, n):
    return y[:, chunk * LANES:chunk * LANES + n]


def _group_rows(q):
    ns = q.shape[0]
    return q.reshape(ns, 4, 2, GQA, HEAD_DIM).transpose(0, 2, 1, 3, 4).reshape(ns, 2, 16, HEAD_DIM)


def _new_kv(kv):
    ns = kv.shape[0]
    kv = kv.reshape(ns, 4, 2, 2, HEAD_DIM).transpose(0, 2, 3, 1, 4)
    kv = jnp.pad(kv, ((0, 0), (0, 0), (0, 0), (0, LANES - 4), (0, 0)))
    return kv[:, :, 0], kv[:, :, 1]


def _ungroup_rows(o):
    ns = o.shape[0]
    return o.reshape(ns, 2, 4, GQA, HEAD_DIM).transpose(0, 2, 1, 3, 4).reshape(ns * 4, A_HEADS * HEAD_DIM)


def kernel(x_prompt, x_sample, cache_kv_a, cache_kidx, cache_kv_cmp, cache_kv_slc, state_kv_win, page_table,
           norm_mix, w_in, w_phi, pe_phi, w_o_a, w_o_b, w_out, norm_ffn, w_router_group, b_router_group,
           w_router_expert, b_router_expert, w_gate, w_up, w_down, norm_final):
    batch, seq, _ = x_prompt.shape
    ns, dec, _ = x_sample.shape
    assert dec == 4 and norm_mix.shape[0] == 1, "kernel is written for DEC_SEQ=4 and DEPTH=1"
    n_pool = cache_kv_a.shape[1]
    n_pages = page_table.shape[1]
    past = n_pages * PAGE
    wb = state_kv_win.shape[2]

    w_packed = _pack_w_in(w_in[0])
    kinds = _chunk_kinds()
    wp = w_phi[0].astype(BF16)
    w_cat = jnp.concatenate([wp[:, :CMP_STRIDE].reshape(2, CMP_STRIDE * HEAD_DIM, HEAD_DIM),
                             wp[:, CMP_STRIDE:].reshape(2, CMP_STRIDE * HEAD_DIM, HEAD_DIM)], axis=2)
    pe_r = pe_phi[0].transpose(1, 0, 2).reshape(2, 2, CMP_STRIDE * HEAD_DIM)
    pe_r = jnp.pad(pe_r, ((0, 0), (0, 6), (0, 0)))
    woa, wob, wout = w_o_a[0].astype(BF16), w_o_b[0].astype(BF16), w_out[0].astype(BF16)
    w_r = jnp.concatenate([w_router_expert[0], w_router_group[0],
                           jnp.zeros((D_MODEL, LANES - N_EXPERTS - N_GROUPS), F32)], axis=1)
    w_r_hi = w_r.astype(BF16)
    w_r_lo = (w_r - w_r_hi.astype(F32)).astype(BF16)
    b_r = jnp.concatenate([b_router_expert[0], b_router_group[0],
                           jnp.zeros((LANES - N_EXPERTS - N_GROUPS,), F32)])[None, :]
    w_gu = jnp.concatenate([w_gate[0], w_up[0]], axis=2).astype(BF16)
    w_d = w_down[0].astype(BF16)
    g_mix, g_ffn, g_fin = norm_mix[0][None, :], norm_ffn[0][None, :], norm_final[None, :]

    xp = x_prompt.reshape(batch * seq, D_MODEL)
    tm_p = 1024
    y_p = _project(xp, g_mix, w_packed, kinds, _proj_tables(jnp.arange(seq)), tm_p, seq // tm_p)
    o_a_p = _dsa_prompt(y_p, batch, seq)
    nsub_p = seq // CMP_STRIDE
    tab_cp = _rope_table(jnp.arange(nsub_p) * CMP_STRIDE + CMP_BLOCK - 1, ROPE_DIM // 2, LANES)
    cmp_p = _compress_prompt(y_p, w_cat, pe_r, tab_cp, batch, seq)
    o_b_p = _nsa_prompt(y_p, cmp_p, batch, seq)
    h_p = _merge(xp, y_p, o_a_p, o_b_p, woa, wob, wout, 256)
    xn_p, comb_p = _router(h_p, g_ffn, w_r_hi, w_r_lo, b_r, 512)
    y_prompt = _moe(xn_p, comb_p, h_p, w_gu, w_d, g_fin, 512).reshape(batch, seq, D_MODEL)

    xs = x_sample.reshape(ns * 4, D_MODEL)
    tm_s = ns * 4
    pos_s = past + (jnp.arange(ns * 4) % 4)
    y_s = _project(xs, g_mix, w_packed, kinds, _proj_tables(pos_s), tm_s, 1)
    pt = page_table.astype(I32)
    qi = _cols(y_s, C_QI, 1024).reshape(ns, 4, IDX_HEADS, IDX_DIM).transpose(0, 2, 1, 3)
    qi = jnp.concatenate([qi, qi], axis=2)
    par = (jnp.arange(IDX_HEADS) % 2)[None, :, None, None, None]
    q_rows = jnp.where(par == jnp.arange(2)[None, None, None, :, None], qi[:, :, :, None, :], 0.0)
    q_rows = q_rows.reshape(ns, IDX_HEADS * 8, 2 * IDX_DIM)
    kiw = _cols(y_s, C_KIW, LANES).reshape(ns, 4, LANES)
    wi = kiw[:, :, IDX_DIM:IDX_DIM + IDX_HEADS].transpose(0, 2, 1)
    w_col = jnp.concatenate([wi, wi], axis=2).reshape(ns, IDX_HEADS * 8, 1)
    knew_t = jnp.pad(kiw[:, :, :IDX_DIM].transpose(0, 2, 1), ((0, 0), (0, 0), (0, LANES - 4)))
    knew_t = jnp.concatenate([knew_t, knew_t], axis=1)
    kidx_t = jnp.swapaxes(cache_kidx[0], 1, 2)
    sc, sc_new = _idx_sample(pt, q_rows, w_col, knew_t, kidx_t)
    m_a, m_a_new = _topk_sample(sc, sc_new, min(DSA_TOPK, (past + 4) // 4))
    q_a = _group_rows(_cols(y_s, C_QA, 1024).reshape(ns, 4, A_HEADS, HEAD_DIM))
    ka_new, va_new = _new_kv(_cols(y_s, C_KVA, 512).reshape(ns, 4, 512))
    o_a_s = _paged_attn(pt, q_a, ka_new, va_new, m_a, m_a_new, cache_kv_a.reshape(n_pool, 4 * PAGE, HEAD_DIM),
                        False, "dsa_sample")
    q_b = _group_rows(_cols(y_s, C_QB, 1024).reshape(ns, 4, B_HEADS, HEAD_DIM))
    pq = _compress_sample(pt, w_cat, cache_kv_cmp.reshape(n_pool, 4 * PAGE, HEAD_DIM))
    nsub_s = past // CMP_STRIDE
    tab_cs = _rope_table(jnp.arange(nsub_s) * CMP_STRIDE + CMP_BLOCK - 1, ROPE_DIM // 2, LANES)
    o_c, blk_mask = _nsa_select_sample(pq, w_cat, pe_r, tab_cs, q_b, past)
    ks_new, vs_new = _new_kv(_cols(y_s, C_KVS, 512).reshape(ns, 4, 512))
    first_new = past // SLC_BLOCK
    t_row = (jnp.arange(8) & 3)[None, :, None]
    m_s_new = jnp.where(jnp.arange(LANES)[None, None, :] <= t_row, blk_mask[:, :, first_new:first_new + 1], 0.0)
    o_s = _paged_attn(pt, q_b, ks_new, vs_new, blk_mask, m_s_new,
                      cache_kv_slc.reshape(n_pool, 4 * PAGE, HEAD_DIM), True, "slc_sample")
    kw_new, vw_new = _new_kv(_cols(y_s, C_KVW, 512).reshape(ns, 4, 512))
    gates = _group_rows(jnp.pad(_cols(y_s, C_GN, 24).reshape(ns, 4, B_HEADS, 3), ((0, 0),) * 3 + ((0, LANES - 3),)))
    gates = gates.reshape(ns, 32, LANES)
    o_b_s = _window_sample(state_kv_win[0].reshape(ns, 4 * wb, HEAD_DIM), kw_new, vw_new, q_b, o_c, o_s, gates, wb)
    h_s = _merge(xs, y_s, _ungroup_rows(o_a_s), _ungroup_rows(o_b_s), woa, wob, wout, tm_s)
    xn_s, comb_s = _router(h_s, g_ffn, w_r_hi, w_r_lo, b_r, tm_s)
    y_sample = _moe(xn_s, comb_s, h_s, w_gu, w_d, g_fin, tm_s).reshape(ns, 4, D_MODEL)

    def kv_out(y, chunk, lead):
        return _cols(y, chunk, 512).reshape((1,) + lead + (2, 2, HEAD_DIM))

    kv_w_p = _cols(y_p, C_KVW, 512).reshape(batch, seq, 2, 2, HEAD_DIM)
    wb_p = min(WINDOW, seq)
    win_s = jnp.concatenate([state_kv_win[0][:, 4:], _cols(y_s, C_KVW, 512).reshape(ns, 4, 2, 2, HEAD_DIM)], axis=1)
    return (y_prompt, y_sample,
            kv_out(y_p, C_KVA, (batch, seq)), _cols(y_p, C_KIW, IDX_DIM).reshape(1, batch, seq, IDX_DIM),
            kv_out(y_p, C_KVC, (batch, seq)), kv_out(y_p, C_KVS, (batch, seq)), kv_w_p[None, :, seq - wb_p:],
            kv_out(y_s, C_KVA, (ns, 4)), _cols(y_s, C_KIW, IDX_DIM).reshape(1, ns, 4, IDX_DIM),
            kv_out(y_s, C_KVC, (ns, 4)), kv_out(y_s, C_KVS, (ns, 4)), win_s[None])
```

```python
import functools

import numpy as np
import jax
import jax.numpy as jnp
from jax import lax
from jax.experimental import pallas as pl
from jax.experimental.pallas import tpu as pltpu

F32 = jnp.float32
BF16 = jnp.bfloat16
I32 = jnp.int32

LANES = 128
VMEM_LIMIT = 56 * 1024 * 1024

D_MODEL = 2048
HEAD_DIM = 128
ROPE_DIM = HEAD_DIM // 4
ROPE_THETA = 500000.0
A_HEADS = 8
A_KV_HEADS = 2
IDX_HEADS = 16
IDX_DIM = 64
IDX_ROPE_DIM = IDX_DIM // 4
DSA_TOPK = 256
B_HEADS = 8
B_KV_HEADS = 2
GQA = 4
CMP_BLOCK = 32
CMP_STRIDE = 16
SLC_BLOCK = 64
SLC_TOPN = 16
SLC_LOCAL = 2
WINDOW = 512
PAGE = 128
N_GROUPS = 4
EXPERTS_PER_GROUP = 4
N_EXPERTS = 16
GROUP_LANE = N_EXPERTS
EXPERT_FF = 512
Q_BLOCK = 128
RMS_EPS = 1e-6
NEG = -1e30
FORCE_BONUS = 1e4
ATT_SCALE = HEAD_DIM ** -0.5
IDX_SCALE = (IDX_DIM * IDX_HEADS) ** -0.5
IDX_PAGES = 32
ATTN_PAGES = 32
CMP_PAGES = 16
TOPK_SEQS_PER_STEP = 4
N_CAUSAL_EXTENTS = 8

C_GM, C_QA, C_QI, C_QB, C_KVA, C_KVC, C_KVS, C_KVW, C_KIW, C_GN = 0, 32, 40, 48, 56, 60, 64, 68, 72, 73
N_CHUNKS = 76
NP = N_CHUNKS * LANES
K_NONE, K_ROPE_HEAD, K_ROPE_IDX, K_SIGMOID, K_ROPE_IDX_LOW = 0, 1, 2, 3, 4

NT_DIMS = (((1,), (1,)), ((), ()))


def _dot(a, b):
    return jnp.dot(a, b, preferred_element_type=F32)


def _dot_nt(a, b):
    return lax.dot_general(a, b, NT_DIMS, preferred_element_type=F32)


def _split3(x):
    hi = x.astype(BF16)
    r1 = x - hi.astype(F32)
    mid = r1.astype(BF16)
    lo = (r1 - mid.astype(F32)).astype(BF16)
    return hi, mid, lo


def _dot3_exact_rhs(x, rhs_bf16):
    hi, mid, lo = _split3(x)
    return _dot(hi, rhs_bf16) + _dot(mid, rhs_bf16) + _dot(lo, rhs_bf16)


def _dot3_exact_lhs(lhs_bf16, x):
    hi, mid, lo = _split3(x)
    return _dot(lhs_bf16, hi) + _dot(lhs_bf16, mid) + _dot(lhs_bf16, lo)


def _iota(shape, dim):
    return lax.broadcasted_iota(I32, shape, dim)


def _count(mask):
    return jnp.sum(jnp.where(mask, 1.0, 0.0), axis=-1, keepdims=True)


def _topk_mask(score, k, idx_bits):
    s = score + 0.0
    b = pltpu.bitcast(s, I32)
    key = b ^ ((b >> 31) & jnp.int32(0x7FFFFFFF))
    rows = key.shape[0]
    imin = jnp.int32(-2 ** 31)
    kf = jnp.float32(k)

    def value_bit(i, tu):
        cand = tu | lax.shift_left(jnp.int32(1), 31 - i)
        cnt = _count(key >= (cand ^ imin))
        return jnp.where(cnt >= kf, cand, tu)

    tu = lax.fori_loop(0, 32, value_bit, jnp.zeros((rows, 1), I32))
    thr = tu ^ imin
    gt = key > thr
    eq = key == thr
    need = kf - _count(gt)
    idx = _iota(key.shape, 1)

    def index_bit(i, j):
        cand = j | lax.shift_left(jnp.int32(1), idx_bits - 1 - i)
        cnt = _count(eq & (idx < cand))
        return jnp.where(cnt < need, cand, j)

    def tie_search():
        return lax.fori_loop(0, idx_bits, index_bit, jnp.zeros((rows, 1), I32))

    def take_all():
        return jnp.full((rows, 1), 2 ** idx_bits, I32)

    surplus = jnp.max(jnp.abs(_count(eq) - need))
    j = lax.cond(surplus == 0.0, take_all, tie_search)
    return gt | (eq & (idx <= j))


def _masked_softmax_parts(s, mask):
    s = jnp.where(mask, s, NEG)
    m = jnp.max(s, axis=-1, keepdims=True)
    p = jnp.where(mask, jnp.exp(s - m), 0.0)
    return p, jnp.sum(p, axis=-1, keepdims=True)


def _attend_heads(qg, k, v, mask):
    s = _dot_nt(qg, k) * ATT_SCALE
    ps, ls = [], []
    for r in range(GQA):
        p, l = _masked_softmax_parts(s[r * Q_BLOCK:(r + 1) * Q_BLOCK], mask)
        ps.append(p.astype(BF16))
        ls.append(l)
    o = _dot(jnp.concatenate(ps, axis=0), v)
    return o / jnp.maximum(jnp.concatenate(ls, axis=0), 1e-30)


def _rope(y, c, a, b, half):
    return y * c + pltpu.roll(y, LANES - half, 1) * a + pltpu.roll(y, half, 1) * b


def _proj_kernel(kind_ref, x_ref, g_ref, w_ref, tab_ref, o_ref, xn_ref, *, tn):
    j = pl.program_id(1)

    @pl.when(j == 0)
    def _():
        xf = x_ref[...]
        r = lax.rsqrt(jnp.mean(xf * xf, axis=-1, keepdims=True) + RMS_EPS)
        xn_ref[...] = ((xf * r) * g_ref[...]).astype(BF16)

    acc = _dot(xn_ref[...], w_ref[...])
    lane = _iota((1, LANES), 1)
    for c in range(tn // LANES):
        cs = slice(c * LANES, (c + 1) * LANES)
        y = acc[:, cs]
        k = kind_ref[j * (tn // LANES) + c]

        @pl.when(k == K_NONE)
        def _(y=y, cs=cs):
            o_ref[:, cs] = y

        @pl.when(k == K_ROPE_HEAD)
        def _(y=y, cs=cs):
            o_ref[:, cs] = _rope(y, tab_ref[:, 0:128], tab_ref[:, 128:256], tab_ref[:, 256:384], ROPE_DIM // 2)

        @pl.when((k == K_ROPE_IDX) | (k == K_ROPE_IDX_LOW))
        def _(y=y, cs=cs, k=k):
            r = _rope(y, tab_ref[:, 384:512], tab_ref[:, 512:640], tab_ref[:, 640:768], IDX_ROPE_DIM // 2)
            lim = jnp.where(k == K_ROPE_IDX_LOW, IDX_DIM, LANES)
            o_ref[:, cs] = jnp.where(lane < lim, r, y)

        @pl.when(k == K_SIGMOID)
        def _(y=y, cs=cs):
            o_ref[:, cs] = 1.0 / (1.0 + jnp.exp(-y))


def _project(x2d, gain, w_packed, kinds, tables, tm, n_tab_blocks):
    t = x2d.shape[0]
    tn = 512
    grid = (t // tm, NP // tn)
    return pl.pallas_call(
        functools.partial(_proj_kernel, tn=tn),
        grid_spec=pltpu.PrefetchScalarGridSpec(
            num_scalar_prefetch=1, grid=grid,
            in_specs=[
                pl.BlockSpec((tm, D_MODEL), lambda i, j, kr: (i, 0)),
                pl.BlockSpec((1, D_MODEL), lambda i, j, kr: (0, 0)),
                pl.BlockSpec((D_MODEL, tn), lambda i, j, kr: (0, j)),
                pl.BlockSpec((tm, 6 * LANES), lambda i, j, kr: (i % n_tab_blocks, 0)),
            ],
            out_specs=pl.BlockSpec((tm, tn), lambda i, j, kr: (i, j)),
            scratch_shapes=[pltpu.VMEM((tm, D_MODEL), BF16)]),
        out_shape=jax.ShapeDtypeStruct((t, NP), F32),
        compiler_params=pltpu.CompilerParams(
            dimension_semantics=("parallel", "arbitrary"), vmem_limit_bytes=VMEM_LIMIT),
        name="in_proj",
    )(kinds, x2d, gain, w_packed, tables)


def _dsa_prompt_body(qa_ref, qi_ref, kiwq_ref, kiwk_ref, kva_ref, o_ref, q0, ext, k_top):
    lane = _iota((1, LANES), 1)
    qpos = q0 + _iota((Q_BLOCK, 1), 0)
    causal = _iota((1, ext), 1) <= qpos
    if ext <= k_top:
        sel = causal
    else:
        kk = kiwk_ref[0:ext, :]
        k_dup = jnp.where(lane < IDX_DIM, kk, pltpu.roll(kk, IDX_DIM, 1)).astype(BF16)
        wi = kiwq_ref[...]
        score = jnp.zeros((Q_BLOCK, ext), F32)
        for pair in range(IDX_HEADS // 2):
            qp = qi_ref[:, pair * LANES:(pair + 1) * LANES]
            for half in range(2):
                h = 2 * pair + half
                keep = (lane < IDX_DIM) if half == 0 else (lane >= IDX_DIM)
                qm = jnp.where(keep, qp, 0.0).astype(BF16)
                rel = jnp.maximum(_dot_nt(qm, k_dup), 0.0)
                score = score + rel * wi[:, IDX_DIM + h:IDX_DIM + h + 1]
        score = jnp.where(causal, score * IDX_SCALE, NEG)
        sel = _topk_mask(score, k_top, int(np.ceil(np.log2(ext)))) & causal

    for g in range(A_KV_HEADS):
        kg = kva_ref[0:ext, (2 * g) * LANES:(2 * g + 1) * LANES].astype(BF16)
        vg = kva_ref[0:ext, (2 * g + 1) * LANES:(2 * g + 2) * LANES].astype(BF16)
        qg = jnp.concatenate(
            [qa_ref[:, (g * GQA + r) * LANES:(g * GQA + r + 1) * LANES] for r in range(GQA)], axis=0).astype(BF16)
        o = _attend_heads(qg, kg, vg, sel)
        for r in range(GQA):
            o_ref[:, (g * GQA + r) * LANES:(g * GQA + r + 1) * LANES] = o[r * Q_BLOCK:(r + 1) * Q_BLOCK]


def _dsa_prompt_kernel(qa_ref, qi_ref, kiwq_ref, kiwk_ref, kva_ref, o_ref, *, seq, n_ext):
    i = pl.program_id(1)
    per = (seq // Q_BLOCK) // n_ext
    for v in range(n_ext):
        @pl.when(i // per == v)
        def _(v=v):
            _dsa_prompt_body(qa_ref, qi_ref, kiwq_ref, kiwk_ref, kva_ref, o_ref, i * Q_BLOCK,
                             (v + 1) * per * Q_BLOCK, min(DSA_TOPK, seq // 4))


def _dsa_prompt(y, batch, seq):
    nqb = seq // Q_BLOCK
    return pl.pallas_call(
        functools.partial(_dsa_prompt_kernel, seq=seq, n_ext=N_CAUSAL_EXTENTS),
        grid=(batch, nqb),
        in_specs=[
            pl.BlockSpec((Q_BLOCK, 8 * LANES), lambda b, i: (b * nqb + i, C_QA // 8)),
            pl.BlockSpec((Q_BLOCK, 8 * LANES), lambda b, i: (b * nqb + i, C_QI // 8)),
            pl.BlockSpec((Q_BLOCK, LANES), lambda b, i: (b * nqb + i, C_KIW)),
            pl.BlockSpec((seq, LANES), lambda b, i: (b, C_KIW)),
            pl.BlockSpec((seq, 4 * LANES), lambda b, i: (b, C_KVA // 4)),
        ],
        out_specs=pl.BlockSpec((Q_BLOCK, 8 * LANES), lambda b, i: (b * nqb + i, 0)),
        out_shape=jax.ShapeDtypeStruct((batch * seq, A_HEADS * HEAD_DIM), F32),
        compiler_params=pltpu.CompilerParams(
            dimension_semantics=("parallel", "arbitrary"), vmem_limit_bytes=VMEM_LIMIT),
        name="dsa_prompt",
    )(y, y, y, y, y)


def _pe_bias(pe_ref, w_ref, c):
    pb = _dot(pe_ref[c].astype(BF16), w_ref[c])
    return pb[0:1, 0:LANES] + pb[1:2, LANES:2 * LANES]


def _compress_prompt_kernel(kvc00_ref, kvc01_ref, kvc10_ref, kvc11_ref, w_ref, pe_ref, tab_ref, o_ref, *, nsub):
    kvc = ((kvc00_ref, kvc01_ref), (kvc10_ref, kvc11_ref))
    for c in range(2):
        xs = []
        for g in range(B_KV_HEADS):
            xs.append(jnp.concatenate(
                [kvc[g][c][pl.ds(l, nsub, stride=CMP_STRIDE), :] for l in range(CMP_STRIDE)], axis=1))
        pq = _dot(jnp.concatenate(xs, axis=0).astype(BF16), w_ref[c])
        bias = _pe_bias(pe_ref, w_ref, c)
        for g in range(B_KV_HEADS):
            top = pq[g * nsub:(g + 1) * nsub, 0:LANES]
            bot = pq[g * nsub:(g + 1) * nsub, LANES:2 * LANES]
            comp = top + pltpu.roll(bot, nsub - 1, 0) + bias
            if c == 0:
                comp = _rope(comp, tab_ref[:, 0:128], tab_ref[:, 128:256], tab_ref[:, 256:384], ROPE_DIM // 2)
            o_ref[:, (2 * g + c) * LANES:(2 * g + c + 1) * LANES] = comp


def _compress_prompt(y, w_cat, pe_r, tab_c, batch, seq):
    nsub = seq // CMP_STRIDE
    return pl.pallas_call(
        functools.partial(_compress_prompt_kernel, nsub=nsub),
        grid=(batch,),
        in_specs=[
            pl.BlockSpec((seq, LANES), lambda b: (b, C_KVC)),
            pl.BlockSpec((seq, LANES), lambda b: (b, C_KVC + 1)),
            pl.BlockSpec((seq, LANES), lambda b: (b, C_KVC + 2)),
            pl.BlockSpec((seq, LANES), lambda b: (b, C_KVC + 3)),
            pl.BlockSpec((2, CMP_STRIDE * HEAD_DIM, 2 * LANES), lambda b: (0, 0, 0)),
            pl.BlockSpec((2, 8, CMP_STRIDE * HEAD_DIM), lambda b: (0, 0, 0)),
            pl.BlockSpec((nsub, 3 * LANES), lambda b: (0, 0)),
        ],
        out_specs=pl.BlockSpec((nsub, 4 * LANES), lambda b: (b, 0)),
        out_shape=jax.ShapeDtypeStruct((batch * nsub, 4 * LANES), F32),
        compiler_params=pltpu.CompilerParams(dimension_semantics=("parallel",), vmem_limit_bytes=VMEM_LIMIT),
        name="compress_prompt",
    )(y, y, y, y, w_cat, pe_r, tab_c)


def _cover(n_rows, n_cols, n_cmp, n_slc):
    ci = _iota((n_rows, n_cols), 0) * CMP_STRIDE
    sj = _iota((n_rows, n_cols), 1) * SLC_BLOCK
    m = (ci < sj + SLC_BLOCK) & (ci + CMP_BLOCK > sj)
    m = m & (_iota((n_rows, n_cols), 0) < n_cmp) & (_iota((n_rows, n_cols), 1) < n_slc)
    return jnp.where(m, 1.0, 0.0).astype(BF16)


def _block_scores(p_s, qpos_col, n_slc):
    blk = _iota(p_s.shape, 1)
    cur = qpos_col >> 6
    forced = (blk == 0) | ((blk <= cur) & (blk > cur - SLC_LOCAL))
    score = jnp.where(blk <= cur, p_s + FORCE_BONUS * jnp.where(forced, 1.0, 0.0), NEG)
    return jnp.where(blk < n_slc, score, -jnp.inf)


def _select_blocks(p_sum, cover_t, qpos_row, n_slc, n_top):
    hi, mid, lo = _split3(p_sum)
    p_s = (_dot_nt(cover_t, hi) + _dot_nt(cover_t, mid) + _dot_nt(cover_t, lo))[0:n_slc]
    blk = _iota((n_slc, Q_BLOCK), 0)
    cur = qpos_row >> 6
    forced = (blk == 0) | ((blk <= cur) & (blk > cur - SLC_LOCAL))
    score = jnp.where(blk <= cur, p_s + FORCE_BONUS * jnp.where(forced, 1.0, 0.0), NEG)
    rank = jnp.zeros((n_slc, Q_BLOCK), F32)
    for i in range(n_slc):
        si = score[i:i + 1, :]
        rank = rank + jnp.where((si > score) | ((si == score) & (blk > i)), 1.0, 0.0)
    sel_t = jnp.where(rank < n_top, 1.0, 0.0)
    sel_t = jnp.concatenate([sel_t, jnp.zeros((LANES - n_slc, Q_BLOCK), F32)], axis=0)
    return sel_t.T


def _nsa_prompt_body(qb_ref, gn_ref, cmp_ref, kvs_ref, kvw_ref, o_ref, i, ext, seq):
    q0 = i * Q_BLOCK
    nsub = seq // CMP_STRIDE
    n_cmp = (seq - CMP_BLOCK) // CMP_STRIDE + 1
    n_slc = seq // SLC_BLOCK
    qpos = q0 + _iota((Q_BLOCK, 1), 0)
    qpos_row = q0 + _iota((1, Q_BLOCK), 1)
    c_end = _iota((1, nsub), 1) * CMP_STRIDE + (CMP_BLOCK - 1)
    cmask = c_end <= qpos
    cj = _iota((LANES, nsub), 0) * SLC_BLOCK
    cn = _iota((LANES, nsub), 1) * CMP_STRIDE
    cover_t = jnp.where((cn < cj + SLC_BLOCK) & (cn + CMP_BLOCK > cj) & (_iota((LANES, nsub), 1) < n_cmp)
                        & (_iota((LANES, nsub), 0) < n_slc), 1.0, 0.0).astype(BF16)
    kpos = _iota((1, ext), 1)
    expand = jnp.where((_iota((LANES, ext), 1) >> 6) == _iota((LANES, ext), 0), 1.0, 0.0).astype(BF16)
    gates = gn_ref[...]

    for g in range(B_KV_HEADS):
        qg = jnp.concatenate(
            [qb_ref[:, (g * GQA + r) * LANES:(g * GQA + r + 1) * LANES] for r in range(GQA)], axis=0).astype(BF16)
        kc = cmp_ref[:, (2 * g) * LANES:(2 * g + 1) * LANES].astype(BF16)
        vc = cmp_ref[:, (2 * g + 1) * LANES:(2 * g + 2) * LANES].astype(BF16)
        s = _dot_nt(qg, kc) * ATT_SCALE
        ps = []
        for r in range(GQA):
            p, l = _masked_softmax_parts(s[r * Q_BLOCK:(r + 1) * Q_BLOCK], cmask)
            ps.append(p / jnp.maximum(l, 1e-30))
        o_c = _dot(jnp.concatenate(ps, axis=0).astype(BF16), vc)
        sel = _select_blocks(ps[0] + ps[1] + ps[2] + ps[3], cover_t, qpos_row, n_slc, min(SLC_TOPN, n_slc))
        tok = (_dot(sel.astype(BF16), expand) > 0.5) & (kpos <= qpos)
        ks = kvs_ref[0:ext, (2 * g) * LANES:(2 * g + 1) * LANES].astype(BF16)
        vs = kvs_ref[0:ext, (2 * g + 1) * LANES:(2 * g + 2) * LANES].astype(BF16)
        o_s = _attend_heads(qg, ks, vs, tok)
        nwb = WINDOW // Q_BLOCK + 1
        k_parts, v_parts, m_parts = [], [], []
        for w in range(nwb):
            kb = i - (nwb - 1) + w
            start = pl.multiple_of(jnp.maximum(kb, 0) * Q_BLOCK, Q_BLOCK)
            k_parts.append(kvw_ref[pl.ds(start, Q_BLOCK), (2 * g) * LANES:(2 * g + 1) * LANES].astype(BF16))
            v_parts.append(kvw_ref[pl.ds(start, Q_BLOCK), (2 * g + 1) * LANES:(2 * g + 2) * LANES].astype(BF16))
            wpos = kb * Q_BLOCK + _iota((1, Q_BLOCK), 1)
            m_parts.append(jnp.where((wpos <= qpos) & (wpos > qpos - WINDOW) & (wpos >= 0), 1.0, 0.0))
        wmask = jnp.concatenate(m_parts, axis=1) > 0.5
        o_w = _attend_heads(qg, jnp.concatenate(k_parts, axis=0), jnp.concatenate(v_parts, axis=0), wmask)
        for r in range(GQA):
            h = g * GQA + r
            rs = slice(r * Q_BLOCK, (r + 1) * Q_BLOCK)
            o_ref[:, h * LANES:(h + 1) * LANES] = (gates[:, 3 * h:3 * h + 1] * o_c[rs]
                                                   + gates[:, 3 * h + 1:3 * h + 2] * o_s[rs]
                                                   + gates[:, 3 * h + 2:3 * h + 3] * o_w[rs])


def _nsa_prompt_kernel(qb_ref, gn_ref, cmp_ref, kvs_ref, kvw_ref, o_ref, *, seq, n_ext):
    i = pl.program_id(1)
    per = (seq // Q_BLOCK) // n_ext
    for v in range(n_ext):
        @pl.when(i // per == v)
        def _(v=v):
            _nsa_prompt_body(qb_ref, gn_ref, cmp_ref, kvs_ref, kvw_ref, o_ref, i, (v + 1) * per * Q_BLOCK, seq)


def _nsa_prompt(y, cmp, batch, seq):
    nqb = seq // Q_BLOCK
    nsub = seq // CMP_STRIDE
    return pl.pallas_call(
        functools.partial(_nsa_prompt_kernel, seq=seq, n_ext=N_CAUSAL_EXTENTS),
        grid=(batch, nqb),
        in_specs=[
            pl.BlockSpec((Q_BLOCK, 8 * LANES), lambda b, i: (b * nqb + i, C_QB // 8)),
            pl.BlockSpec((Q_BLOCK, LANES), lambda b, i: (b * nqb + i, C_GN)),
            pl.BlockSpec((nsub, 4 * LANES), lambda b, i: (b, 0)),
            pl.BlockSpec((seq, 4 * LANES), lambda b, i: (b, C_KVS // 4)),
            pl.BlockSpec((seq, 4 * LANES), lambda b, i: (b, C_KVW // 4)),
        ],
        out_specs=pl.BlockSpec((Q_BLOCK, 8 * LANES), lambda b, i: (b * nqb + i, 0)),
        out_shape=jax.ShapeDtypeStruct((batch * seq, B_HEADS * HEAD_DIM), F32),
        compiler_params=pltpu.CompilerParams(
            dimension_semantics=("parallel", "arbitrary"), vmem_limit_bytes=VMEM_LIMIT),
        name="nsa_prompt",
    )(y, y, cmp, y, y)


def _merge_kernel(x_ref, ga_ref, gb_ref, oa_ref, ob_ref, woa_ref, wob_ref, wout_ref, o_ref):
    u = (ga_ref[...] * _dot(oa_ref[...].astype(BF16), woa_ref[...])
         + gb_ref[...] * _dot(ob_ref[...].astype(BF16), wob_ref[...]))
    o_ref[...] = x_ref[...] + _dot(u.astype(BF16), wout_ref[...])


def _merge(x2d, y, o_a, o_b, w_o_a, w_o_b, w_out, tm):
    t = x2d.shape[0]
    const = dict(pipeline_mode=pl.Buffered(1))
    return pl.pallas_call(
        _merge_kernel,
        grid=(t // tm,),
        in_specs=[
            pl.BlockSpec((tm, D_MODEL), lambda i: (i, 0)),
            pl.BlockSpec((tm, D_MODEL), lambda i: (i, 0)),
            pl.BlockSpec((tm, D_MODEL), lambda i: (i, 1)),
            pl.BlockSpec((tm, A_HEADS * HEAD_DIM), lambda i: (i, 0)),
            pl.BlockSpec((tm, B_HEADS * HEAD_DIM), lambda i: (i, 0)),
            pl.BlockSpec((A_HEADS * HEAD_DIM, D_MODEL), lambda i: (0, 0), **const),
            pl.BlockSpec((B_HEADS * HEAD_DIM, D_MODEL), lambda i: (0, 0), **const),
            pl.BlockSpec((D_MODEL, D_MODEL), lambda i: (0, 0), **const),
        ],
        out_specs=pl.BlockSpec((tm, D_MODEL), lambda i: (i, 0)),
        out_shape=jax.ShapeDtypeStruct((t, D_MODEL), F32),
        compiler_params=pltpu.CompilerParams(dimension_semantics=("parallel",), vmem_limit_bytes=VMEM_LIMIT),
        name="merge",
    )(x2d, y, y, o_a, o_b, w_o_a, w_o_b, w_out)


def _router_kernel(h_ref, g_ref, whi_ref, wlo_ref, b_ref, xn_ref, comb_ref):
    hf = h_ref[...]
    r = lax.rsqrt(jnp.mean(hf * hf, axis=-1, keepdims=True) + RMS_EPS)
    xn = (hf * r) * g_ref[...]
    xhi = xn.astype(BF16)
    xn_ref[...] = xhi
    xlo = (xn - xhi.astype(F32)).astype(BF16)
    logits = _dot(xhi, whi_ref[...]) + _dot(xlo, whi_ref[...]) + _dot(xhi, wlo_ref[...]) + b_ref[...]
    lane_i = _iota(logits.shape, 1)
    lane = lane_i.astype(F32)
    big = jnp.float32(1e9)
    gm = (lane_i >= N_EXPERTS) & (lane_i < N_EXPERTS + N_GROUPS)
    gmax = jnp.max(jnp.where(gm, logits, -jnp.inf), axis=-1, keepdims=True)
    gsum = jnp.sum(jnp.where(gm, jnp.exp(logits - gmax), 0.0), axis=-1, keepdims=True)
    p_group = 1.0 / gsum
    g_sel = jnp.min(jnp.where(gm & (logits == gmax), lane, big), axis=-1, keepdims=True) - N_EXPERTS
    em = (lane_i < N_EXPERTS) & ((lane_i >> 2).astype(F32) == g_sel)
    emax = jnp.max(jnp.where(em, logits, -jnp.inf), axis=-1, keepdims=True)
    pe = jnp.where(em, jnp.exp(logits - emax), 0.0)
    p_exp = pe / jnp.sum(pe, axis=-1, keepdims=True)
    p1 = jnp.max(jnp.where(em, p_exp, -1.0), axis=-1, keepdims=True)
    i1 = jnp.min(jnp.where(em & (p_exp == p1), lane, big), axis=-1, keepdims=True)
    em2 = em & (lane != i1)
    p2 = jnp.max(jnp.where(em2, p_exp, -1.0), axis=-1, keepdims=True)
    i2 = jnp.min(jnp.where(em2 & (p_exp == p2), lane, big), axis=-1, keepdims=True)
    tot = p1 + p2
    comb = jnp.where(lane == i1, p_group * p1 / tot, jnp.where(lane == i2, p_group * p2 / tot, 0.0))
    comb_ref[...] = jnp.where(lane_i == GROUP_LANE, g_sel, comb)


def _router(h2d, gain, w_hi, w_lo, bias, tm):
    t = h2d.shape[0]
    return pl.pallas_call(
        _router_kernel,
        grid=(t // tm,),
        in_specs=[
            pl.BlockSpec((tm, D_MODEL), lambda i: (i, 0)),
            pl.BlockSpec((1, D_MODEL), lambda i: (0, 0)),
            pl.BlockSpec((D_MODEL, LANES), lambda i: (0, 0)),
            pl.BlockSpec((D_MODEL, LANES), lambda i: (0, 0)),
            pl.BlockSpec((1, LANES), lambda i: (0, 0)),
        ],
        out_specs=[pl.BlockSpec((tm, D_MODEL), lambda i: (i, 0)), pl.BlockSpec((tm, LANES), lambda i: (i, 0))],
        out_shape=[jax.ShapeDtypeStruct((t, D_MODEL), BF16), jax.ShapeDtypeStruct((t, LANES), F32)],
        compiler_params=pltpu.CompilerParams(dimension_semantics=("parallel",), vmem_limit_bytes=VMEM_LIMIT),
        name="router",
    )(h2d, gain, w_hi, w_lo, bias)


def _moe_capacity(tm):
    return max(16, -(-(3 * tm // 8) // 16) * 16)


def _lane_col(x, lane_idx):
    return jnp.sum(jnp.where(_iota(x.shape, 1) == lane_idx, x, 0.0), axis=-1, keepdims=True)


def _expert_ffn(x, wg_ref, wu_ref, wd_ref, ce):
    gate = _dot(x, wg_ref[0])
    hid = (gate * (1.0 / (1.0 + jnp.exp(-gate)))) * _dot(x, wu_ref[0])
    return _dot((hid * ce).astype(BF16), wd_ref[0])


def _moe_kernel(xn_ref, comb_ref, h_ref, wg_ref, wu_ref, wd_ref, gf_ref, o_ref,
                acc_ref, xc_ref, cc_ref, oc_ref, st_ref, cnt_ref, *, cap):
    e = pl.program_id(1)
    g = e // EXPERTS_PER_GROUP
    tm = xn_ref.shape[0]

    @pl.when(e == 0)
    def _():
        acc_ref[...] = jnp.zeros_like(acc_ref)

    @pl.when(e % EXPERTS_PER_GROUP == 0)
    def _():
        comb = comb_ref[...]
        member = comb[:, GROUP_LANE:GROUP_LANE + 1] == g.astype(F32)
        mb = jnp.broadcast_to(jnp.where(member, 1.0, 0.0), (tm, LANES))
        cnt_ref[0] = jnp.sum(mb[:, 0:1]).astype(I32)
        lower = jnp.where(_iota((tm, tm), 0) > _iota((tm, tm), 1), 1.0, 0.0).astype(BF16)
        rank_b = _dot(lower, mb.astype(BF16))
        rank_row = rank_b.T[0:1, :]
        m_row = mb.T[0:1, :]
        sel = jnp.where((m_row > 0.5) & (rank_row == _iota((cap, tm), 0).astype(F32)), 1.0, 0.0).astype(BF16)
        st_ref[...] = jnp.where(member & (rank_b[:, 0:1] == _iota((tm, cap), 1).astype(F32)), 1.0, 0.0).astype(BF16)
        xc_ref[...] = _dot(sel, xn_ref[...]).astype(BF16)
        cc_ref[...] = _dot3_exact_lhs(sel, comb)
        oc_ref[...] = jnp.zeros_like(oc_ref)

    fits = cnt_ref[0] <= cap

    @pl.when(fits)
    def _():
        oc_ref[...] += _expert_ffn(xc_ref[...], wg_ref, wu_ref, wd_ref, _lane_col(cc_ref[...], e))

    @pl.when(jnp.logical_not(fits))
    def _():
        acc_ref[...] += _expert_ffn(xn_ref[...], wg_ref, wu_ref, wd_ref, _lane_col(comb_ref[...], e))

    @pl.when(fits & (e % EXPERTS_PER_GROUP == EXPERTS_PER_GROUP - 1))
    def _():
        oc = oc_ref[...]
        hi = oc.astype(BF16)
        lo = (oc - hi.astype(F32)).astype(BF16)
        acc_ref[...] += _dot(st_ref[...], hi) + _dot(st_ref[...], lo)

    @pl.when(e == N_EXPERTS - 1)
    def _():
        out = h_ref[...] + acc_ref[...]
        r = lax.rsqrt(jnp.mean(out * out, axis=-1, keepdims=True) + RMS_EPS)
        o_ref[...] = (out * r) * gf_ref[...]


def _moe(xn, comb, h2d, w_g, w_u, w_d, gain_final, tm):
    t = xn.shape[0]
    cap = _moe_capacity(tm)
    return pl.pallas_call(
        functools.partial(_moe_kernel, cap=cap),
        grid=(t // tm, N_EXPERTS),
        in_specs=[
            pl.BlockSpec((tm, D_MODEL), lambda i, e: (i, 0)),
            pl.BlockSpec((tm, LANES), lambda i, e: (i, 0)),
            pl.BlockSpec((tm, D_MODEL), lambda i, e: (i, 0)),
            pl.BlockSpec((1, D_MODEL, EXPERT_FF), lambda i, e: (e, 0, 0)),
            pl.BlockSpec((1, D_MODEL, EXPERT_FF), lambda i, e: (e, 0, 0)),
            pl.BlockSpec((1, EXPERT_FF, D_MODEL), lambda i, e: (e, 0, 0)),
            pl.BlockSpec((1, D_MODEL), lambda i, e: (0, 0)),
        ],
        out_specs=pl.BlockSpec((tm, D_MODEL), lambda i, e: (i, 0)),
        out_shape=jax.ShapeDtypeStruct((t, D_MODEL), F32),
        scratch_shapes=[pltpu.VMEM((tm, D_MODEL), F32), pltpu.VMEM((cap, D_MODEL), BF16),
                        pltpu.VMEM((cap, LANES), F32), pltpu.VMEM((cap, D_MODEL), F32),
                        pltpu.VMEM((tm, cap), BF16), pltpu.SMEM((1,), I32)],
        compiler_params=pltpu.CompilerParams(
            dimension_semantics=("parallel", "arbitrary"), vmem_limit_bytes=VMEM_LIMIT),
        name="moe",
    )(xn, comb, h2d, w_g, w_u, w_d, gain_final)


def _page_specs(block, n_chunk_pages, row_of):
    def make(k):
        def index_map(s, c, pt):
            return (pt[s, c * n_chunk_pages + k],) + row_of
        return pl.BlockSpec(block, index_map)
    return [make(k) for k in range(n_chunk_pages)]


def _idx_sample_kernel(pt_ref, q_ref, w_ref, knew_ref, *rest, past):
    pages, (o_ref, onew_ref) = rest[:IDX_PAGES], rest[IDX_PAGES:]
    c = pl.program_id(1)
    q = q_ref[...].astype(BF16)
    w = w_ref[...]

    def scores(kt):
        rel = jnp.maximum(_dot(q, kt.astype(BF16)), 0.0) * w
        return jnp.sum(rel.reshape(IDX_HEADS, 8, kt.shape[1]), axis=0) * IDX_SCALE

    kt = jnp.concatenate([jnp.concatenate([p[0], p[0]], axis=0) for p in pages], axis=1)
    o_ref[...] = scores(kt)

    @pl.when(c == 0)
    def _():
        t = _iota((8, LANES), 0) & 3
        j = _iota((8, LANES), 1)
        onew_ref[...] = jnp.where(j <= t, scores(knew_ref[...]), NEG)


def _idx_sample(pt, q_rows, w_col, knew_t, kidx_t):
    ns, n_pages = pt.shape
    n_chunks = n_pages // IDX_PAGES
    past = n_pages * PAGE
    width = IDX_PAGES * PAGE
    return pl.pallas_call(
        functools.partial(_idx_sample_kernel, past=past),
        grid_spec=pltpu.PrefetchScalarGridSpec(
            num_scalar_prefetch=1, grid=(ns, n_chunks),
            in_specs=[
                pl.BlockSpec((None, LANES, LANES), lambda s, c, pt: (s, 0, 0)),
                pl.BlockSpec((None, LANES, 1), lambda s, c, pt: (s, 0, 0)),
                pl.BlockSpec((None, LANES, LANES), lambda s, c, pt: (s, 0, 0)),
            ] + _page_specs((1, IDX_DIM, PAGE), IDX_PAGES, (0, 0)),
            out_specs=[pl.BlockSpec((None, 8, width), lambda s, c, pt: (s, 0, c)),
                       pl.BlockSpec((None, 8, LANES), lambda s, c, pt: (s, 0, 0))]),
        out_shape=[jax.ShapeDtypeStruct((ns, 8, past), F32), jax.ShapeDtypeStruct((ns, 8, LANES), F32)],
        compiler_params=pltpu.CompilerParams(
            dimension_semantics=("parallel", "arbitrary"), vmem_limit_bytes=VMEM_LIMIT),
        name="idx_sample",
    )(pt, q_rows, w_col, knew_t, *([kidx_t] * IDX_PAGES))


def _topk_sample_kernel(s_ref, snew_ref, m_ref, mnew_ref, *, past, k):
    n = s_ref.shape[0] * 8
    s = jnp.concatenate([s_ref[...].reshape(n, past), snew_ref[...].reshape(n, LANES)], axis=1)
    sel = _topk_mask(s, k, int(np.ceil(np.log2(past + LANES))))
    sel = jnp.where(sel, 1.0, 0.0)
    m_ref[...] = sel[:, :past].reshape(m_ref.shape)
    mnew_ref[...] = sel[:, past:].reshape(mnew_ref.shape)


def _topk_sample(scores, scores_new, k):
    ns, _, past = scores.shape
    sps = TOPK_SEQS_PER_STEP if ns % TOPK_SEQS_PER_STEP == 0 else 1
    return pl.pallas_call(
        functools.partial(_topk_sample_kernel, past=past, k=k),
        grid=(ns // sps,),
        in_specs=[pl.BlockSpec((sps, 8, past), lambda s: (s, 0, 0)),
                  pl.BlockSpec((sps, 8, LANES), lambda s: (s, 0, 0))],
        out_specs=[pl.BlockSpec((sps, 8, past), lambda s: (s, 0, 0)),
                   pl.BlockSpec((sps, 8, LANES), lambda s: (s, 0, 0))],
        out_shape=[jax.ShapeDtypeStruct((ns, 8, past), F32), jax.ShapeDtypeStruct((ns, 8, LANES), F32)],
        compiler_params=pltpu.CompilerParams(dimension_semantics=("parallel",), vmem_limit_bytes=VMEM_LIMIT),
        name="topk_sample",
    )(scores, scores_new)


def _rows16_from_rows8(g):
    return jnp.where(_iota((16, 8), 1) == (g * 4 + (_iota((16, 8), 0) >> 2)), 1.0, 0.0).astype(BF16)


def _paged_attn_kernel(pt_ref, q_ref, knew_ref, vnew_ref, m_ref, mnew_ref, *rest, block_mask):
    pages, (o_ref,), (m_sc, l_sc, acc_sc) = rest[:ATTN_PAGES], rest[ATTN_PAGES:ATTN_PAGES + 1], \
        rest[ATTN_PAGES + 1:]
    c = pl.program_id(1)
    width = ATTN_PAGES * PAGE

    @pl.when(c == 0)
    def _():
        m_sc[...] = jnp.full_like(m_sc, NEG)
        l_sc[...] = jnp.zeros_like(l_sc)
        acc_sc[...] = jnp.zeros_like(acc_sc)

    if block_mask:
        nb = m_ref.shape[1]
        first = c * (width // SLC_BLOCK)
        expand = jnp.where(_iota((nb, width), 0) == first + (_iota((nb, width), 1) >> 6), 1.0, 0.0).astype(BF16)
        mask8 = _dot(m_ref[...].astype(BF16), expand)
    else:
        mask8 = m_ref[...]
    mask8 = mask8.astype(BF16)

    def update(g, kg, vg, mask16):
        rs = slice(g * 16, (g + 1) * 16)
        s = _dot_nt(q_ref[g].astype(BF16), kg) * ATT_SCALE
        s = jnp.where(mask16, s, NEG)
        m_old = m_sc[rs]
        m_new = jnp.maximum(m_old, jnp.max(s, axis=-1, keepdims=True))
        p = jnp.where(mask16, jnp.exp(s - m_new), 0.0)
        alpha = jnp.exp(m_old - m_new)
        l_sc[rs] = alpha * l_sc[rs] + jnp.sum(p, axis=-1, keepdims=True)
        acc_sc[rs] = alpha * acc_sc[rs] + _dot(p.astype(BF16), vg)
        m_sc[rs] = m_new

    for g in range(2):
        kg = jnp.concatenate([p[0, pl.ds(2 * g, PAGE, stride=4), :] for p in pages], axis=0).astype(BF16)
        vg = jnp.concatenate([p[0, pl.ds(2 * g + 1, PAGE, stride=4), :] for p in pages], axis=0).astype(BF16)
        update(g, kg, vg, _dot(_rows16_from_rows8(g), mask8) > 0.5)

    @pl.when(c == pl.num_programs(1) - 1)
    def _():
        mnew8 = mnew_ref[...].astype(BF16)
        for g in range(2):
            update(g, knew_ref[g].astype(BF16), vnew_ref[g].astype(BF16), _dot(_rows16_from_rows8(g), mnew8) > 0.5)
        o_ref[...] = acc_sc[...] / jnp.maximum(l_sc[...], 1e-30)


def _paged_attn(pt, q_g, knew, vnew, mask, mask_new, pool, block_mask, name):
    ns, n_pages = pt.shape
    n_chunks = n_pages // ATTN_PAGES
    width = ATTN_PAGES * PAGE
    if block_mask:
        mspec = pl.BlockSpec((None, 8, mask.shape[2]), lambda s, c, pt: (s, 0, 0))
    else:
        mspec = pl.BlockSpec((None, 8, width), lambda s, c, pt: (s, 0, c))
    return pl.pallas_call(
        functools.partial(_paged_attn_kernel, block_mask=block_mask),
        grid_spec=pltpu.PrefetchScalarGridSpec(
            num_scalar_prefetch=1, grid=(ns, n_chunks),
            in_specs=[
                pl.BlockSpec((None, 2, 16, HEAD_DIM), lambda s, c, pt: (s, 0, 0, 0)),
                pl.BlockSpec((None, 2, LANES, HEAD_DIM), lambda s, c, pt: (s, 0, 0, 0)),
                pl.BlockSpec((None, 2, LANES, HEAD_DIM), lambda s, c, pt: (s, 0, 0, 0)),
                mspec,
                pl.BlockSpec((None, 8, LANES), lambda s, c, pt: (s, 0, 0)),
            ] + _page_specs((1, 4 * PAGE, HEAD_DIM), ATTN_PAGES, (0, 0)),
            out_specs=pl.BlockSpec((None, 32, HEAD_DIM), lambda s, c, pt: (s, 0, 0)),
            scratch_shapes=[pltpu.VMEM((32, 1), F32), pltpu.VMEM((32, 1), F32), pltpu.VMEM((32, HEAD_DIM), F32)]),
        out_shape=jax.ShapeDtypeStruct((ns, 32, HEAD_DIM), F32),
        compiler_params=pltpu.CompilerParams(
            dimension_semantics=("parallel", "arbitrary"), vmem_limit_bytes=VMEM_LIMIT),
        name=name,
    )(pt, q_g, knew, vnew, mask, mask_new, *([pool] * ATTN_PAGES))


def _compress_sample_kernel(pt_ref, w_ref, *rest):
    pages, (o_ref,) = rest[:CMP_PAGES], rest[CMP_PAGES:]
    sub = PAGE // CMP_STRIDE
    r = _iota((PAGE, PAGE), 0)
    perm = jnp.where(_iota((PAGE, PAGE), 1) == (r & (sub - 1)) * CMP_STRIDE + (r >> 3), 1.0, 0.0).astype(BF16)
    taps = []
    for p in pages:
        a = jnp.concatenate([p[0, pl.ds(gc, PAGE, stride=4), :] for gc in range(4)], axis=1)
        taps.append(_dot(perm, a.astype(BF16)))
    for c in range(2):
        xs = []
        for g in range(B_KV_HEADS):
            col = (2 * g + c) * LANES
            xs.append(jnp.concatenate(
                [jnp.concatenate([t[l * sub:(l + 1) * sub, col:col + LANES] for l in range(CMP_STRIDE)], axis=1)
                 for t in taps], axis=0))
        pq = _dot(jnp.concatenate(xs, axis=0).astype(BF16), w_ref[c])
        n = CMP_PAGES * sub
        for g in range(B_KV_HEADS):
            o_ref[:, (2 * g + c) * 2 * LANES:(2 * g + c + 1) * 2 * LANES] = pq[g * n:(g + 1) * n]


def _compress_sample(pt, w_cat, pool):
    ns, n_pages = pt.shape
    n_chunks = n_pages // CMP_PAGES
    rows = CMP_PAGES * (PAGE // CMP_STRIDE)
    return pl.pallas_call(
        _compress_sample_kernel,
        grid_spec=pltpu.PrefetchScalarGridSpec(
            num_scalar_prefetch=1, grid=(ns, n_chunks),
            in_specs=[pl.BlockSpec((2, CMP_STRIDE * HEAD_DIM, 2 * LANES), lambda s, c, pt: (0, 0, 0))]
            + _page_specs((1, 4 * PAGE, HEAD_DIM), CMP_PAGES, (0, 0)),
            out_specs=pl.BlockSpec((None, rows, 8 * LANES), lambda s, c, pt: (s, c, 0))),
        out_shape=jax.ShapeDtypeStruct((ns, n_chunks * rows, 8 * LANES), F32),
        compiler_params=pltpu.CompilerParams(
            dimension_semantics=("parallel", "arbitrary"), vmem_limit_bytes=VMEM_LIMIT),
        name="compress_sample",
    )(pt, w_cat, *([pool] * CMP_PAGES))


def _nsa_select_sample_kernel(pq_ref, w_ref, pe_ref, tab_ref, q_ref, oc_ref, bm_ref, *, past, n_blk_pad):
    nsub = past // CMP_STRIDE
    total = past + 4
    n_cmp = (total - CMP_BLOCK) // CMP_STRIDE + 1
    n_slc = -(-total // SLC_BLOCK)
    row_t = _iota((16, 1), 0) >> 2
    c_end = _iota((1, nsub), 1) * CMP_STRIDE + (CMP_BLOCK - 1)
    cmask = (c_end <= past + row_t) & (_iota((1, nsub), 1) < n_cmp)
    p_rows = []
    for g in range(B_KV_HEADS):
        comp = []
        for c in range(2):
            base = (2 * g + c) * 2 * LANES
            v = pq_ref[:, base:base + LANES] + pltpu.roll(pq_ref[:, base + LANES:base + 2 * LANES], nsub - 1, 0)
            v = v + _pe_bias(pe_ref, w_ref, c)
            if c == 0:
                v = _rope(v, tab_ref[:, 0:128], tab_ref[:, 128:256], tab_ref[:, 256:384], ROPE_DIM // 2)
            comp.append(v.astype(BF16))
        s = _dot_nt(q_ref[g].astype(BF16), comp[0]) * ATT_SCALE
        p, l = _masked_softmax_parts(s, cmask)
        p = p / jnp.maximum(l, 1e-30)
        oc_ref[g * 16:(g + 1) * 16, :] = _dot(p.astype(BF16), comp[1])
        p_rows.append(p)
    p_all = jnp.concatenate(p_rows, axis=0)
    gather = jnp.where(_iota((8, 32), 0) == (_iota((8, 32), 1) >> 2), 1.0, 0.0).astype(BF16)
    p_sum = _dot3_exact_lhs(gather, p_all)
    p_s = _dot3_exact_rhs(p_sum, _cover(nsub, n_blk_pad, n_cmp, n_slc))
    qpos = past + (_iota((8, 1), 0) & 3)
    sel = _topk_mask(_block_scores(p_s, qpos, n_slc), min(SLC_TOPN, n_slc), int(np.ceil(np.log2(n_blk_pad))))
    bm_ref[...] = jnp.where(sel, 1.0, 0.0)


def _nsa_select_sample(pq, w_cat, pe_r, tab_c, q_g, past):
    ns, nsub, _ = pq.shape
    n_slc = -(-(past + 4) // SLC_BLOCK)
    n_blk_pad = -(-n_slc // LANES) * LANES
    return pl.pallas_call(
        functools.partial(_nsa_select_sample_kernel, past=past, n_blk_pad=n_blk_pad),
        grid=(ns,),
        in_specs=[
            pl.BlockSpec((None, nsub, 8 * LANES), lambda s: (s, 0, 0)),
            pl.BlockSpec((2, CMP_STRIDE * HEAD_DIM, 2 * LANES), lambda s: (0, 0, 0)),
            pl.BlockSpec((2, 8, CMP_STRIDE * HEAD_DIM), lambda s: (0, 0, 0)),
            pl.BlockSpec((nsub, 3 * LANES), lambda s: (0, 0)),
            pl.BlockSpec((None, 2, 16, HEAD_DIM), lambda s: (s, 0, 0, 0)),
        ],
        out_specs=[pl.BlockSpec((None, 32, HEAD_DIM), lambda s: (s, 0, 0)),
                   pl.BlockSpec((None, 8, n_blk_pad), lambda s: (s, 0, 0))],
        out_shape=[jax.ShapeDtypeStruct((ns, 32, HEAD_DIM), F32), jax.ShapeDtypeStruct((ns, 8, n_blk_pad), F32)],
        compiler_params=pltpu.CompilerParams(dimension_semantics=("parallel",), vmem_limit_bytes=VMEM_LIMIT),
        name="nsa_select_sample",
    )(pq, w_cat, pe_r, tab_c, q_g)


def _window_sample_kernel(win_ref, knew_ref, vnew_ref, q_ref, oc_ref, os_ref, gate_ref, o_ref, *, wb):
    row_t = _iota((16, 1), 0) >> 2
    kpos = _iota((1, wb), 1)
    mask_old = kpos > row_t + (wb - WINDOW)
    mask_new = _iota((1, LANES), 1) <= row_t
    for g in range(B_KV_HEADS):
        rs = slice(g * 16, (g + 1) * 16)
        q = q_ref[g].astype(BF16)
        kw = win_ref[pl.ds(2 * g, wb, stride=4), :].astype(BF16)
        vw = win_ref[pl.ds(2 * g + 1, wb, stride=4), :].astype(BF16)
        s = jnp.concatenate([_dot_nt(q, kw), _dot_nt(q, knew_ref[g].astype(BF16))], axis=1) * ATT_SCALE
        mask = jnp.concatenate([jnp.where(mask_old, 1.0, 0.0), jnp.where(mask_new, 1.0, 0.0)], axis=1) > 0.5
        p, l = _masked_softmax_parts(s, mask)
        o_w = (_dot(p[:, :wb].astype(BF16), vw) + _dot(p[:, wb:].astype(BF16), vnew_ref[g].astype(BF16)))
        o_w = o_w / jnp.maximum(l, 1e-30)
        gt = gate_ref[rs]
        o_ref[rs] = gt[:, 0:1] * oc_ref[rs] + gt[:, 1:2] * os_ref[rs] + gt[:, 2:3] * o_w


def _window_sample(win_rows, knew, vnew, q_g, o_c, o_s, gates, wb):
    ns = q_g.shape[0]
    blk = lambda *shape: pl.BlockSpec((None,) + shape, lambda s: (s,) + (0,) * len(shape))
    return pl.pallas_call(
        functools.partial(_window_sample_kernel, wb=wb),
        grid=(ns,),
        in_specs=[blk(4 * wb, HEAD_DIM), blk(2, LANES, HEAD_DIM), blk(2, LANES, HEAD_DIM), blk(2, 16, HEAD_DIM),
                  blk(32, HEAD_DIM), blk(32, HEAD_DIM), blk(32, LANES)],
        out_specs=blk(32, HEAD_DIM),
        out_shape=jax.ShapeDtypeStruct((ns, 32, HEAD_DIM), F32),
        compiler_params=pltpu.CompilerParams(dimension_semantics=("parallel",), vmem_limit_bytes=VMEM_LIMIT),
        name="window_sample",
    )(win_rows, knew, vnew, q_g, o_c, o_s, gates)


def _rope_table(pos, half, period):
    inv = ROPE_THETA ** (-jnp.arange(half, dtype=F32) / half)
    ang = pos.astype(F32)[:, None] * inv[None, :]
    cos, sin = jnp.cos(ang), jnp.sin(ang)
    lane = np.arange(LANES) % period
    idx = lane % half
    lo = jnp.asarray(lane < half)[None, :]
    hi = jnp.asarray((lane >= half) & (lane < 2 * half))[None, :]
    c = jnp.where(lo | hi, cos[:, idx], 1.0)
    a = jnp.where(lo, -sin[:, idx], 0.0)
    b = jnp.where(hi, sin[:, idx], 0.0)
    return jnp.concatenate([c, a, b], axis=1)


def _proj_tables(pos):
    return jnp.concatenate([_rope_table(pos, ROPE_DIM // 2, LANES), _rope_table(pos, IDX_ROPE_DIM // 2, IDX_DIM)],
                           axis=1)


def _chunk_kinds():
    kinds = np.zeros((N_CHUNKS,), np.int32)
    kinds[C_GM:C_GM + 32] = K_SIGMOID
    kinds[C_QA:C_QA + 8] = K_ROPE_HEAD
    kinds[C_QI:C_QI + 8] = K_ROPE_IDX
    kinds[C_QB:C_QB + 8] = K_ROPE_HEAD
    for base in (C_KVA, C_KVS, C_KVW):
        kinds[base] = K_ROPE_HEAD
        kinds[base + 2] = K_ROPE_HEAD
    kinds[C_KIW] = K_ROPE_IDX_LOW
    kinds[C_GN] = K_SIGMOID
    return jnp.asarray(kinds)


def _pack_w_in(w):
    sizes = [("q_a", 1024), ("kv_a", 512), ("q_i", 1024), ("k_i", 64), ("w_i", 16), ("q_b", 1024), ("kv_c", 512),
             ("kv_s", 512), ("kv_w", 512), ("g_n", 24), ("g_m", 4096)]
    parts, off = {}, 0
    for name, n in sizes:
        parts[name] = w[:, off:off + n]
        off += n
    z = lambda n: jnp.zeros((w.shape[0], n), w.dtype)
    cols = [parts["g_m"], parts["q_a"], parts["q_i"], parts["q_b"], parts["kv_a"], parts["kv_c"], parts["kv_s"],
            parts["kv_w"], parts["k_i"], parts["w_i"], z(48), parts["g_n"], z(104), z(2 * LANES)]
    return jnp.concatenate(cols, axis=1).astype(BF16)


def _cols(y, chunk, n):
    return y[:, chunk * LANES:chunk * LANES + n]


def _group_rows(q):
    ns = q.shape[0]
    return q.reshape(ns, 4, 2, GQA, HEAD_DIM).transpose(0, 2, 1, 3, 4).reshape(ns, 2, 16, HEAD_DIM)


def _new_kv(kv):
    ns = kv.shape[0]
    kv = kv.reshape(ns, 4, 2, 2, HEAD_DIM).transpose(0, 2, 3, 1, 4)
    kv = jnp.pad(kv, ((0, 0), (0, 0), (0, 0), (0, LANES - 4), (0, 0)))
    return kv[:, :, 0], kv[:, :, 1]


def _ungroup_rows(o):
    ns = o.shape[0]
    return o.reshape(ns, 2, 4, GQA, HEAD_DIM).transpose(0, 2, 1, 3, 4).reshape(ns * 4, A_HEADS * HEAD_DIM)


def kernel(x_prompt, x_sample, cache_kv_a, cache_kidx, cache_kv_cmp, cache_kv_slc, state_kv_win, page_table,
           norm_mix, w_in, w_phi, pe_phi, w_o_a, w_o_b, w_out, norm_ffn, w_router_group, b_router_group,
           w_router_expert, b_router_expert, w_gate, w_up, w_down, norm_final):
    batch, seq, _ = x_prompt.shape
    ns, dec, _ = x_sample.shape
    assert dec == 4 and norm_mix.shape[0] == 1, "kernel is written for DEC_SEQ=4 and DEPTH=1"
    n_pool = cache_kv_a.shape[1]
    n_pages = page_table.shape[1]
    past = n_pages * PAGE
    wb = state_kv_win.shape[2]

    w_packed = _pack_w_in(w_in[0])
    kinds = _chunk_kinds()
    wp = w_phi[0].astype(BF16)
    w_cat = jnp.concatenate([wp[:, :CMP_STRIDE].reshape(2, CMP_STRIDE * HEAD_DIM, HEAD_DIM),
                             wp[:, CMP_STRIDE:].reshape(2, CMP_STRIDE * HEAD_DIM, HEAD_DIM)], axis=2)
    pe_r = pe_phi[0].transpose(1, 0, 2).reshape(2, 2, CMP_STRIDE * HEAD_DIM)
    pe_r = jnp.pad(pe_r, ((0, 0), (0, 6), (0, 0)))
    woa, wob, wout = w_o_a[0].astype(BF16), w_o_b[0].astype(BF16), w_out[0].astype(BF16)
    w_r = jnp.concatenate([w_router_expert[0], w_router_group[0],
                           jnp.zeros((D_MODEL, LANES - N_EXPERTS - N_GROUPS), F32)], axis=1)
    w_r_hi = w_r.astype(BF16)
    w_r_lo = (w_r - w_r_hi.astype(F32)).astype(BF16)
    b_r = jnp.concatenate([b_router_expert[0], b_router_group[0],
                           jnp.zeros((LANES - N_EXPERTS - N_GROUPS,), F32)])[None, :]
    w_g, w_u, w_d = w_gate[0].astype(BF16), w_up[0].astype(BF16), w_down[0].astype(BF16)
    g_mix, g_ffn, g_fin = norm_mix[0][None, :], norm_ffn[0][None, :], norm_final[None, :]

    xp = x_prompt.reshape(batch * seq, D_MODEL)
    tm_p = 1024
    y_p = _project(xp, g_mix, w_packed, kinds, _proj_tables(jnp.arange(seq)), tm_p, seq // tm_p)
    o_a_p = _dsa_prompt(y_p, batch, seq)
    nsub_p = seq // CMP_STRIDE
    tab_cp = _rope_table(jnp.arange(nsub_p) * CMP_STRIDE + CMP_BLOCK - 1, ROPE_DIM // 2, LANES)
    cmp_p = _compress_prompt(y_p, w_cat, pe_r, tab_cp, batch, seq)
    o_b_p = _nsa_prompt(y_p, cmp_p, batch, seq)
    h_p = _merge(xp, y_p, o_a_p, o_b_p, woa, wob, wout, 256)
    xn_p, comb_p = _router(h_p, g_ffn, w_r_hi, w_r_lo, b_r, 512)
    y_prompt = _moe(xn_p, comb_p, h_p, w_g, w_u, w_d, g_fin, 512).reshape(batch, seq, D_MODEL)

    xs = x_sample.reshape(ns * 4, D_MODEL)
    tm_s = ns * 4
    pos_s = past + (jnp.arange(ns * 4) % 4)
    y_s = _project(xs, g_mix, w_packed, kinds, _proj_tables(pos_s), tm_s, 1)
    pt = page_table.astype(I32)
    qi = _cols(y_s, C_QI, 1024).reshape(ns, 4, IDX_HEADS, IDX_DIM).transpose(0, 2, 1, 3)
    qi = jnp.concatenate([qi, qi], axis=2)
    par = (jnp.arange(IDX_HEADS) % 2)[None, :, None, None, None]
    q_rows = jnp.where(par == jnp.arange(2)[None, None, None, :, None], qi[:, :, :, None, :], 0.0)
    q_rows = q_rows.reshape(ns, IDX_HEADS * 8, 2 * IDX_DIM)
    kiw = _cols(y_s, C_KIW, LANES).reshape(ns, 4, LANES)
    wi = kiw[:, :, IDX_DIM:IDX_DIM + IDX_HEADS].transpose(0, 2, 1)
    w_col = jnp.concatenate([wi, wi], axis=2).reshape(ns, IDX_HEADS * 8, 1)
    knew_t = jnp.pad(kiw[:, :, :IDX_DIM].transpose(0, 2, 1), ((0, 0), (0, 0), (0, LANES - 4)))
    knew_t = jnp.concatenate([knew_t, knew_t], axis=1)
    kidx_t = jnp.swapaxes(cache_kidx[0], 1, 2)
    sc, sc_new = _idx_sample(pt, q_rows, w_col, knew_t, kidx_t)
    m_a, m_a_new = _topk_sample(sc, sc_new, min(DSA_TOPK, (past + 4) // 4))
    q_a = _group_rows(_cols(y_s, C_QA, 1024).reshape(ns, 4, A_HEADS, HEAD_DIM))
    ka_new, va_new = _new_kv(_cols(y_s, C_KVA, 512).reshape(ns, 4, 512))
    o_a_s = _paged_attn(pt, q_a, ka_new, va_new, m_a, m_a_new, cache_kv_a.reshape(n_pool, 4 * PAGE, HEAD_DIM),
                        False, "dsa_sample")
    q_b = _group_rows(_cols(y_s, C_QB, 1024).reshape(ns, 4, B_HEADS, HEAD_DIM))
    pq = _compress_sample(pt, w_cat, cache_kv_cmp.reshape(n_pool, 4 * PAGE, HEAD_DIM))
    nsub_s = past // CMP_STRIDE
    tab_cs = _rope_table(jnp.arange(nsub_s) * CMP_STRIDE + CMP_BLOCK - 1, ROPE_DIM // 2, LANES)
    o_c, blk_mask = _nsa_select_sample(pq, w_cat, pe_r, tab_cs, q_b, past)
    ks_new, vs_new = _new_kv(_cols(y_s, C_KVS, 512).reshape(ns, 4, 512))
    first_new = past // SLC_BLOCK
    t_row = (jnp.arange(8) & 3)[None, :, None]
    m_s_new = jnp.where(jnp.arange(LANES)[None, None, :] <= t_row, blk_mask[:, :, first_new:first_new + 1], 0.0)
    o_s = _paged_attn(pt, q_b, ks_new, vs_new, blk_mask, m_s_new,
                      cache_kv_slc.reshape(n_pool, 4 * PAGE, HEAD_DIM), True, "slc_sample")
    kw_new, vw_new = _new_kv(_cols(y_s, C_KVW, 512).reshape(ns, 4, 512))
    gates = _group_rows(jnp.pad(_cols(y_s, C_GN, 24).reshape(ns, 4, B_HEADS, 3), ((0, 0),) * 3 + ((0, LANES - 3),)))
    gates = gates.reshape(ns, 32, LANES)
    o_b_s = _window_sample(state_kv_win[0].reshape(ns, 4 * wb, HEAD_DIM), kw_new, vw_new, q_b, o_c, o_s, gates, wb)
    h_s = _merge(xs, y_s, _ungroup_rows(o_a_s), _ungroup_rows(o_b_s), woa, wob, wout, tm_s)
    xn_s, comb_s = _router(h_s, g_ffn, w_r_hi, w_r_lo, b_r, tm_s)
    y_sample = _moe(xn_s, comb_s, h_s, w_g, w_u, w_d, g_fin, tm_s).reshape(ns, 4, D_MODEL)

    def kv_out(y, chunk, lead):
        return _cols(y, chunk, 512).reshape((1,) + lead + (2, 2, HEAD_DIM))

    kv_w_p = _cols(y_p, C_KVW, 512).reshape(batch, seq, 2, 2, HEAD_DIM)
    wb_p = min(WINDOW, seq)
    win_s = jnp.concatenate([state_kv_win[0][:, 4:], _cols(y_s, C_KVW, 512).reshape(ns, 4, 2, 2, HEAD_DIM)], axis=1)
    return (y_prompt, y_sample,
            kv_out(y_p, C_KVA, (batch, seq)), _cols(y_p, C_KIW, IDX_DIM).reshape(1, batch, seq, IDX_DIM),
            kv_out(y_p, C_KVC, (batch, seq)), kv_out(y_p, C_KVS, (batch, seq)), kv_w_p[None, :, seq - wb_p:],
            kv_out(y_s, C_KVA, (ns, 4)), _cols(y_s, C_KIW, IDX_DIM).reshape(1, ns, 4, IDX_DIM),
            kv_out(y_s, C_KVC, (ns, 4)), kv_out(y_s, C_KVS, (ns, 4)), win_s[None])
```

```python
import functools

import numpy as np
import jax
import jax.numpy as jnp
from jax import lax
from jax.experimental import pallas as pl
from jax.experimental.pallas import tpu as pltpu

F32 = jnp.float32
BF16 = jnp.bfloat16
I32 = jnp.int32

LANES = 128
VMEM_LIMIT = 56 * 1024 * 1024

D_MODEL = 2048
HEAD_DIM = 128
ROPE_DIM = HEAD_DIM // 4
ROPE_THETA = 500000.0
A_HEADS = 8
A_KV_HEADS = 2
IDX_HEADS = 16
IDX_DIM = 64
IDX_ROPE_DIM = IDX_DIM // 4
DSA_TOPK = 256
B_HEADS = 8
B_KV_HEADS = 2
GQA = 4
CMP_BLOCK = 32
CMP_STRIDE = 16
SLC_BLOCK = 64
SLC_TOPN = 16
SLC_LOCAL = 2
WINDOW = 512
PAGE = 128
N_GROUPS = 4
EXPERTS_PER_GROUP = 4
N_EXPERTS = 16
GROUP_LANE = N_EXPERTS
EXPERT_FF = 512
Q_BLOCK = 128
RMS_EPS = 1e-6
NEG = -1e30
FORCE_BONUS = 1e4
ATT_SCALE = HEAD_DIM ** -0.5
IDX_SCALE = (IDX_DIM * IDX_HEADS) ** -0.5
IDX_PAGES = 64
ATTN_PAGES = 32
CMP_PAGES = 32
TOPK_SEQS_PER_STEP = 4
MAX_CAUSAL_EXTENTS = 4

C_GM, C_QA, C_QI, C_QB, C_KVA, C_KVC, C_KVS, C_KVW, C_KIW, C_GN = 0, 32, 40, 48, 56, 60, 64, 68, 72, 73
N_CHUNKS = 76
NP = N_CHUNKS * LANES
K_NONE, K_ROPE_HEAD, K_ROPE_IDX, K_SIGMOID, K_ROPE_IDX_LOW = 0, 1, 2, 3, 4

NT_DIMS = (((1,), (1,)), ((), ()))


def _dot(a, b):
    return jnp.dot(a, b, preferred_element_type=F32)


def _dot_nt(a, b):
    return lax.dot_general(a, b, NT_DIMS, preferred_element_type=F32)


def _split3(x):
    hi = x.astype(BF16)
    r1 = x - hi.astype(F32)
    mid = r1.astype(BF16)
    lo = (r1 - mid.astype(F32)).astype(BF16)
    return hi, mid, lo


def _dot3_exact_rhs(x, rhs_bf16):
    hi, mid, lo = _split3(x)
    return _dot(hi, rhs_bf16) + _dot(mid, rhs_bf16) + _dot(lo, rhs_bf16)


def _dot3_exact_lhs(lhs_bf16, x):
    hi, mid, lo = _split3(x)
    return _dot(lhs_bf16, hi) + _dot(lhs_bf16, mid) + _dot(lhs_bf16, lo)


def _iota(shape, dim):
    return lax.broadcasted_iota(I32, shape, dim)


def _count(mask):
    return jnp.sum(jnp.where(mask, 1.0, 0.0), axis=-1, keepdims=True)


def _topk_mask(score, k, idx_bits):
    s = score + 0.0
    b = pltpu.bitcast(s, I32)
    key = b ^ ((b >> 31) & jnp.int32(0x7FFFFFFF))
    rows = key.shape[0]
    imin = jnp.int32(-2 ** 31)
    kf = jnp.float32(k)

    def value_bit(i, tu):
        cand = tu | lax.shift_left(jnp.int32(1), 31 - i)
        cnt = _count(key >= (cand ^ imin))
        return jnp.where(cnt >= kf, cand, tu)

    tu = lax.fori_loop(0, 32, value_bit, jnp.zeros((rows, 1), I32))
    thr = tu ^ imin
    gt = key > thr
    eq = key == thr
    need = kf - _count(gt)
    idx = _iota(key.shape, 1)

    def index_bit(i, j):
        cand = j | lax.shift_left(jnp.int32(1), idx_bits - 1 - i)
        cnt = _count(eq & (idx < cand))
        return jnp.where(cnt < need, cand, j)

    def tie_search():
        return lax.fori_loop(0, idx_bits, index_bit, jnp.zeros((rows, 1), I32))

    def take_all():
        return jnp.full((rows, 1), 2 ** idx_bits, I32)

    surplus = jnp.max(jnp.abs(_count(eq) - need))
    j = lax.cond(surplus == 0.0, take_all, tie_search)
    return gt | (eq & (idx <= j))


def _masked_softmax_parts(s, mask):
    s = jnp.where(mask, s, NEG)
    m = jnp.max(s, axis=-1, keepdims=True)
    p = jnp.where(mask, jnp.exp(s - m), 0.0)
    return p, jnp.sum(p, axis=-1, keepdims=True)


def _attend_heads(qg, k, v, mask):
    s = _dot_nt(qg, k) * ATT_SCALE
    ps, ls = [], []
    for r in range(GQA):
        p, l = _masked_softmax_parts(s[r * Q_BLOCK:(r + 1) * Q_BLOCK], mask)
        ps.append(p.astype(BF16))
        ls.append(l)
    o = _dot(jnp.concatenate(ps, axis=0), v)
    return o / jnp.maximum(jnp.concatenate(ls, axis=0), 1e-30)


def _rope(y, c, a, b, half):
    return y * c + pltpu.roll(y, LANES - half, 1) * a + pltpu.roll(y, half, 1) * b


def _proj_kernel(kind_ref, x_ref, g_ref, w_ref, tab_ref, o_ref, xn_ref, *, tn):
    j = pl.program_id(1)

    @pl.when(j == 0)
    def _():
        xf = x_ref[...]
        r = lax.rsqrt(jnp.mean(xf * xf, axis=-1, keepdims=True) + RMS_EPS)
        xn_ref[...] = ((xf * r) * g_ref[...]).astype(BF16)

    acc = _dot(xn_ref[...], w_ref[...])
    lane = _iota((1, LANES), 1)
    for c in range(tn // LANES):
        cs = slice(c * LANES, (c + 1) * LANES)
        y = acc[:, cs]
        k = kind_ref[j * (tn // LANES) + c]

        @pl.when(k == K_NONE)
        def _(y=y, cs=cs):
            o_ref[:, cs] = y

        @pl.when(k == K_ROPE_HEAD)
        def _(y=y, cs=cs):
            o_ref[:, cs] = _rope(y, tab_ref[:, 0:128], tab_ref[:, 128:256], tab_ref[:, 256:384], ROPE_DIM // 2)

        @pl.when((k == K_ROPE_IDX) | (k == K_ROPE_IDX_LOW))
        def _(y=y, cs=cs, k=k):
            r = _rope(y, tab_ref[:, 384:512], tab_ref[:, 512:640], tab_ref[:, 640:768], IDX_ROPE_DIM // 2)
            lim = jnp.where(k == K_ROPE_IDX_LOW, IDX_DIM, LANES)
            o_ref[:, cs] = jnp.where(lane < lim, r, y)

        @pl.when(k == K_SIGMOID)
        def _(y=y, cs=cs):
            o_ref[:, cs] = 1.0 / (1.0 + jnp.exp(-y))


def _project(x2d, gain, w_packed, kinds, tables, tm, n_tab_blocks):
    t = x2d.shape[0]
    tn = 512
    grid = (t // tm, NP // tn)
    return pl.pallas_call(
        functools.partial(_proj_kernel, tn=tn),
        grid_spec=pltpu.PrefetchScalarGridSpec(
            num_scalar_prefetch=1, grid=grid,
            in_specs=[
                pl.BlockSpec((tm, D_MODEL), lambda i, j, kr: (i, 0)),
                pl.BlockSpec((1, D_MODEL), lambda i, j, kr: (0, 0)),
                pl.BlockSpec((D_MODEL, tn), lambda i, j, kr: (0, j)),
                pl.BlockSpec((tm, 6 * LANES), lambda i, j, kr: (i % n_tab_blocks, 0)),
            ],
            out_specs=pl.BlockSpec((tm, tn), lambda i, j, kr: (i, j)),
            scratch_shapes=[pltpu.VMEM((tm, D_MODEL), BF16)]),
        out_shape=jax.ShapeDtypeStruct((t, NP), F32),
        compiler_params=pltpu.CompilerParams(
            dimension_semantics=("parallel", "arbitrary"), vmem_limit_bytes=VMEM_LIMIT),
        name="in_proj",
    )(kinds, x2d, gain, w_packed, tables)


def _dsa_prompt_body(qa_ref, qi_ref, kiwq_ref, kiwk_ref, kva_ref, o_ref, q0, ext, k_top):
    lane = _iota((1, LANES), 1)
    qpos = q0 + _iota((Q_BLOCK, 1), 0)
    causal = _iota((1, ext), 1) <= qpos
    if ext <= k_top:
        sel = causal
    else:
        kk = kiwk_ref[0:ext, :]
        k_dup = jnp.where(lane < IDX_DIM, kk, pltpu.roll(kk, IDX_DIM, 1)).astype(BF16)
        wi = kiwq_ref[...]
        score = jnp.zeros((Q_BLOCK, ext), F32)
        for pair in range(IDX_HEADS // 2):
            qp = qi_ref[:, pair * LANES:(pair + 1) * LANES]
            for half in range(2):
                h = 2 * pair + half
                keep = (lane < IDX_DIM) if half == 0 else (lane >= IDX_DIM)
                qm = jnp.where(keep, qp, 0.0).astype(BF16)
                rel = jnp.maximum(_dot_nt(qm, k_dup), 0.0)
                score = score + rel * wi[:, IDX_DIM + h:IDX_DIM + h + 1]
        score = jnp.where(causal, score * IDX_SCALE, NEG)
        sel = _topk_mask(score, k_top, int(np.ceil(np.log2(ext)))) & causal

    for g in range(A_KV_HEADS):
        kg = kva_ref[0:ext, (2 * g) * LANES:(2 * g + 1) * LANES].astype(BF16)
        vg = kva_ref[0:ext, (2 * g + 1) * LANES:(2 * g + 2) * LANES].astype(BF16)
        qg = jnp.concatenate(
            [qa_ref[:, (g * GQA + r) * LANES:(g * GQA + r + 1) * LANES] for r in range(GQA)], axis=0).astype(BF16)
        o = _attend_heads(qg, kg, vg, sel)
        for r in range(GQA):
            o_ref[:, (g * GQA + r) * LANES:(g * GQA + r + 1) * LANES] = o[r * Q_BLOCK:(r + 1) * Q_BLOCK]


def _causal_extents(seq):
    exts = [seq]
    while exts[0] // 2 >= 2 * Q_BLOCK and len(exts) < MAX_CAUSAL_EXTENTS:
        exts.insert(0, exts[0] // 2)
    return exts


def _for_each_extent(i, seq, body):
    lo = 0
    for ext in _causal_extents(seq):
        hi = ext // Q_BLOCK

        @pl.when((i >= lo) & (i < hi))
        def _(ext=ext):
            body(ext)
        lo = hi


def _dsa_prompt_kernel(qa_ref, qi_ref, kiwq_ref, kiwk_ref, kva_ref, o_ref, *, seq):
    i = pl.program_id(1)
    _for_each_extent(i, seq, lambda ext: _dsa_prompt_body(
        qa_ref, qi_ref, kiwq_ref, kiwk_ref, kva_ref, o_ref, i * Q_BLOCK, ext, min(DSA_TOPK, seq // 4)))


def _dsa_prompt(y, batch, seq):
    nqb = seq // Q_BLOCK
    return pl.pallas_call(
        functools.partial(_dsa_prompt_kernel, seq=seq),
        grid=(batch, nqb),
        in_specs=[
            pl.BlockSpec((Q_BLOCK, 8 * LANES), lambda b, i: (b * nqb + i, C_QA // 8)),
            pl.BlockSpec((Q_BLOCK, 8 * LANES), lambda b, i: (b * nqb + i, C_QI // 8)),
            pl.BlockSpec((Q_BLOCK, LANES), lambda b, i: (b * nqb + i, C_KIW)),
            pl.BlockSpec((seq, LANES), lambda b, i: (b, C_KIW)),
            pl.BlockSpec((seq, 4 * LANES), lambda b, i: (b, C_KVA // 4)),
        ],
        out_specs=pl.BlockSpec((Q_BLOCK, 8 * LANES), lambda b, i: (b * nqb + i, 0)),
        out_shape=jax.ShapeDtypeStruct((batch * seq, A_HEADS * HEAD_DIM), F32),
        compiler_params=pltpu.CompilerParams(
            dimension_semantics=("parallel", "arbitrary"), vmem_limit_bytes=VMEM_LIMIT),
        name="dsa_prompt",
    )(y, y, y, y, y)


def _pe_bias(pe_ref, w_ref, c):
    pb = _dot(pe_ref[c].astype(BF16), w_ref[c])
    return pb[0:1, 0:LANES] + pb[1:2, LANES:2 * LANES]


def _compress_prompt_kernel(kvc00_ref, kvc01_ref, kvc10_ref, kvc11_ref, w_ref, pe_ref, tab_ref, o_ref, *, nsub):
    kvc = ((kvc00_ref, kvc01_ref), (kvc10_ref, kvc11_ref))
    for c in range(2):
        xs = []
        for g in range(B_KV_HEADS):
            xs.append(jnp.concatenate(
                [kvc[g][c][pl.ds(l, nsub, stride=CMP_STRIDE), :] for l in range(CMP_STRIDE)], axis=1))
        pq = _dot(jnp.concatenate(xs, axis=0).astype(BF16), w_ref[c])
        bias = _pe_bias(pe_ref, w_ref, c)
        for g in range(B_KV_HEADS):
            top = pq[g * nsub:(g + 1) * nsub, 0:LANES]
            bot = pq[g * nsub:(g + 1) * nsub, LANES:2 * LANES]
            comp = top + pltpu.roll(bot, nsub - 1, 0) + bias
            if c == 0:
                comp = _rope(comp, tab_ref[:, 0:128], tab_ref[:, 128:256], tab_ref[:, 256:384], ROPE_DIM // 2)
            o_ref[:, (2 * g + c) * LANES:(2 * g + c + 1) * LANES] = comp


def _compress_prompt(y, w_cat, pe_r, tab_c, batch, seq):
    nsub = seq // CMP_STRIDE
    return pl.pallas_call(
        functools.partial(_compress_prompt_kernel, nsub=nsub),
        grid=(batch,),
        in_specs=[
            pl.BlockSpec((seq, LANES), lambda b: (b, C_KVC)),
            pl.BlockSpec((seq, LANES), lambda b: (b, C_KVC + 1)),
            pl.BlockSpec((seq, LANES), lambda b: (b, C_KVC + 2)),
            pl.BlockSpec((seq, LANES), lambda b: (b, C_KVC + 3)),
            pl.BlockSpec((2, CMP_STRIDE * HEAD_DIM, 2 * LANES), lambda b: (0, 0, 0)),
            pl.BlockSpec((2, 8, CMP_STRIDE * HEAD_DIM), lambda b: (0, 0, 0)),
            pl.BlockSpec((nsub, 3 * LANES), lambda b: (0, 0)),
        ],
        out_specs=pl.BlockSpec((nsub, 4 * LANES), lambda b: (b, 0)),
        out_shape=jax.ShapeDtypeStruct((batch * nsub, 4 * LANES), F32),
        compiler_params=pltpu.CompilerParams(dimension_semantics=("parallel",), vmem_limit_bytes=VMEM_LIMIT),
        name="compress_prompt",
    )(y, y, y, y, w_cat, pe_r, tab_c)


def _cover(n_rows, n_cols, n_cmp, n_slc):
    ci = _iota((n_rows, n_cols), 0) * CMP_STRIDE
    sj = _iota((n_rows, n_cols), 1) * SLC_BLOCK
    m = (ci < sj + SLC_BLOCK) & (ci + CMP_BLOCK > sj)
    m = m & (_iota((n_rows, n_cols), 0) < n_cmp) & (_iota((n_rows, n_cols), 1) < n_slc)
    return jnp.where(m, 1.0, 0.0).astype(BF16)


def _block_scores(p_s, qpos_col, n_slc):
    blk = _iota(p_s.shape, 1)
    cur = qpos_col >> 6
    forced = (blk == 0) | ((blk <= cur) & (blk > cur - SLC_LOCAL))
    score = jnp.where(blk <= cur, p_s + FORCE_BONUS * jnp.where(forced, 1.0, 0.0), NEG)
    return jnp.where(blk < n_slc, score, -jnp.inf)


def _select_blocks(p_sum, cover_t, qpos_row, n_slc, n_top):
    hi, mid, lo = _split3(p_sum)
    p_s = (_dot_nt(cover_t, hi) + _dot_nt(cover_t, mid) + _dot_nt(cover_t, lo))[0:n_slc]
    blk = _iota((n_slc, Q_BLOCK), 0)
    cur = qpos_row >> 6
    forced = (blk == 0) | ((blk <= cur) & (blk > cur - SLC_LOCAL))
    score = jnp.where(blk <= cur, p_s + FORCE_BONUS * jnp.where(forced, 1.0, 0.0), NEG)
    rank = jnp.zeros((n_slc, Q_BLOCK), F32)
    for i in range(n_slc):
        si = score[i:i + 1, :]
        rank = rank + jnp.where((si > score) | ((si == score) & (blk > i)), 1.0, 0.0)
    sel_t = jnp.where(rank < n_top, 1.0, 0.0)
    sel_t = jnp.concatenate([sel_t, jnp.zeros((LANES - n_slc, Q_BLOCK), F32)], axis=0)
    return sel_t.T


def _nsa_prompt_body(qb_ref, gn_ref, cmp_ref, kvs_ref, kvw_ref, o_ref, i, ext, seq):
    q0 = i * Q_BLOCK
    nsub = seq // CMP_STRIDE
    n_cmp = (seq - CMP_BLOCK) // CMP_STRIDE + 1
    n_slc = seq // SLC_BLOCK
    qpos = q0 + _iota((Q_BLOCK, 1), 0)
    qpos_row = q0 + _iota((1, Q_BLOCK), 1)
    c_end = _iota((1, nsub), 1) * CMP_STRIDE + (CMP_BLOCK - 1)
    cmask = c_end <= qpos
    cj = _iota((LANES, nsub), 0) * SLC_BLOCK
    cn = _iota((LANES, nsub), 1) * CMP_STRIDE
    cover_t = jnp.where((cn < cj + SLC_BLOCK) & (cn + CMP_BLOCK > cj) & (_iota((LANES, nsub), 1) < n_cmp)
                        & (_iota((LANES, nsub), 0) < n_slc), 1.0, 0.0).astype(BF16)
    kpos = _iota((1, ext), 1)
    expand = jnp.where((_iota((LANES, ext), 1) >> 6) == _iota((LANES, ext), 0), 1.0, 0.0).astype(BF16)
    gates = gn_ref[...]

    for g in range(B_KV_HEADS):
        qg = jnp.concatenate(
            [qb_ref[:, (g * GQA + r) * LANES:(g * GQA + r + 1) * LANES] for r in range(GQA)], axis=0).astype(BF16)
        kc = cmp_ref[:, (2 * g) * LANES:(2 * g + 1) * LANES].astype(BF16)
        vc = cmp_ref[:, (2 * g + 1) * LANES:(2 * g + 2) * LANES].astype(BF16)
        s = _dot_nt(qg, kc) * ATT_SCALE
        ps = []
        for r in range(GQA):
            p, l = _masked_softmax_parts(s[r * Q_BLOCK:(r + 1) * Q_BLOCK], cmask)
            ps.append(p / jnp.maximum(l, 1e-30))
        o_c = _dot(jnp.concatenate(ps, axis=0).astype(BF16), vc)
        sel = _select_blocks(ps[0] + ps[1] + ps[2] + ps[3], cover_t, qpos_row, n_slc, min(SLC_TOPN, n_slc))
        tok = (_dot(sel.astype(BF16), expand) > 0.5) & (kpos <= qpos)
        ks = kvs_ref[0:ext, (2 * g) * LANES:(2 * g + 1) * LANES].astype(BF16)
        vs = kvs_ref[0:ext, (2 * g + 1) * LANES:(2 * g + 2) * LANES].astype(BF16)
        o_s = _attend_heads(qg, ks, vs, tok)
        nwb = WINDOW // Q_BLOCK + 1
        k_parts, v_parts, m_parts = [], [], []
        for w in range(nwb):
            kb = i - (nwb - 1) + w
            start = pl.multiple_of(jnp.maximum(kb, 0) * Q_BLOCK, Q_BLOCK)
            k_parts.append(kvw_ref[pl.ds(start, Q_BLOCK), (2 * g) * LANES:(2 * g + 1) * LANES].astype(BF16))
            v_parts.append(kvw_ref[pl.ds(start, Q_BLOCK), (2 * g + 1) * LANES:(2 * g + 2) * LANES].astype(BF16))
            wpos = kb * Q_BLOCK + _iota((1, Q_BLOCK), 1)
            m_parts.append(jnp.where((wpos <= qpos) & (wpos > qpos - WINDOW) & (wpos >= 0), 1.0, 0.0))
        wmask = jnp.concatenate(m_parts, axis=1) > 0.5
        o_w = _attend_heads(qg, jnp.concatenate(k_parts, axis=0), jnp.concatenate(v_parts, axis=0), wmask)
        for r in range(GQA):
            h = g * GQA + r
            rs = slice(r * Q_BLOCK, (r + 1) * Q_BLOCK)
            o_ref[:, h * LANES:(h + 1) * LANES] = (gates[:, 3 * h:3 * h + 1] * o_c[rs]
                                                   + gates[:, 3 * h + 1:3 * h + 2] * o_s[rs]
                                                   + gates[:, 3 * h + 2:3 * h + 3] * o_w[rs])


def _nsa_prompt_kernel(qb_ref, gn_ref, cmp_ref, kvs_ref, kvw_ref, o_ref, *, seq):
    i = pl.program_id(1)
    _for_each_extent(i, seq, lambda ext: _nsa_prompt_body(
        qb_ref, gn_ref, cmp_ref, kvs_ref, kvw_ref, o_ref, i, ext, seq))


def _nsa_prompt(y, cmp, batch, seq):
    nqb = seq // Q_BLOCK
    nsub = seq // CMP_STRIDE
    return pl.pallas_call(
        functools.partial(_nsa_prompt_kernel, seq=seq),
        grid=(batch, nqb),
        in_specs=[
            pl.BlockSpec((Q_BLOCK, 8 * LANES), lambda b, i: (b * nqb + i, C_QB // 8)),
            pl.BlockSpec((Q_BLOCK, LANES), lambda b, i: (b * nqb + i, C_GN)),
            pl.BlockSpec((nsub, 4 * LANES), lambda b, i: (b, 0)),
            pl.BlockSpec((seq, 4 * LANES), lambda b, i: (b, C_KVS // 4)),
            pl.BlockSpec((seq, 4 * LANES), lambda b, i: (b, C_KVW // 4)),
        ],
        out_specs=pl.BlockSpec((Q_BLOCK, 8 * LANES), lambda b, i: (b * nqb + i, 0)),
        out_shape=jax.ShapeDtypeStruct((batch * seq, B_HEADS * HEAD_DIM), F32),
        compiler_params=pltpu.CompilerParams(
            dimension_semantics=("parallel", "arbitrary"), vmem_limit_bytes=VMEM_LIMIT),
        name="nsa_prompt",
    )(y, y, cmp, y, y)


def _merge_kernel(x_ref, ga_ref, gb_ref, oa_ref, ob_ref, woa_ref, wob_ref, wout_ref, o_ref):
    u = (ga_ref[...] * _dot(oa_ref[...].astype(BF16), woa_ref[...])
         + gb_ref[...] * _dot(ob_ref[...].astype(BF16), wob_ref[...]))
    o_ref[...] = x_ref[...] + _dot(u.astype(BF16), wout_ref[...])


def _merge(x2d, y, o_a, o_b, w_o_a, w_o_b, w_out, tm):
    t = x2d.shape[0]
    const = dict(pipeline_mode=pl.Buffered(1))
    return pl.pallas_call(
        _merge_kernel,
        grid=(t // tm,),
        in_specs=[
            pl.BlockSpec((tm, D_MODEL), lambda i: (i, 0)),
            pl.BlockSpec((tm, D_MODEL), lambda i: (i, 0)),
            pl.BlockSpec((tm, D_MODEL), lambda i: (i, 1)),
            pl.BlockSpec((tm, A_HEADS * HEAD_DIM), lambda i: (i, 0)),
            pl.BlockSpec((tm, B_HEADS * HEAD_DIM), lambda i: (i, 0)),
            pl.BlockSpec((A_HEADS * HEAD_DIM, D_MODEL), lambda i: (0, 0), **const),
            pl.BlockSpec((B_HEADS * HEAD_DIM, D_MODEL), lambda i: (0, 0), **const),
            pl.BlockSpec((D_MODEL, D_MODEL), lambda i: (0, 0), **const),
        ],
        out_specs=pl.BlockSpec((tm, D_MODEL), lambda i: (i, 0)),
        out_shape=jax.ShapeDtypeStruct((t, D_MODEL), F32),
        compiler_params=pltpu.CompilerParams(dimension_semantics=("parallel",), vmem_limit_bytes=VMEM_LIMIT),
        name="merge",
    )(x2d, y, y, o_a, o_b, w_o_a, w_o_b, w_out)


def _router_kernel(h_ref, g_ref, whi_ref, wlo_ref, b_ref, xn_ref, comb_ref):
    hf = h_ref[...]
    r = lax.rsqrt(jnp.mean(hf * hf, axis=-1, keepdims=True) + RMS_EPS)
    xn = (hf * r) * g_ref[...]
    xhi = xn.astype(BF16)
    xn_ref[...] = xhi
    xlo = (xn - xhi.astype(F32)).astype(BF16)
    logits = _dot(xhi, whi_ref[...]) + _dot(xlo, whi_ref[...]) + _dot(xhi, wlo_ref[...]) + b_ref[...]
    lane_i = _iota(logits.shape, 1)
    lane = lane_i.astype(F32)
    big = jnp.float32(1e9)
    gm = (lane_i >= N_EXPERTS) & (lane_i < N_EXPERTS + N_GROUPS)
    gmax = jnp.max(jnp.where(gm, logits, -jnp.inf), axis=-1, keepdims=True)
    gsum = jnp.sum(jnp.where(gm, jnp.exp(logits - gmax), 0.0), axis=-1, keepdims=True)
    p_group = 1.0 / gsum
    g_sel = jnp.min(jnp.where(gm & (logits == gmax), lane, big), axis=-1, keepdims=True) - N_EXPERTS
    em = (lane_i < N_EXPERTS) & ((lane_i >> 2).astype(F32) == g_sel)
    emax = jnp.max(jnp.where(em, logits, -jnp.inf), axis=-1, keepdims=True)
    pe = jnp.where(em, jnp.exp(logits - emax), 0.0)
    p_exp = pe / jnp.sum(pe, axis=-1, keepdims=True)
    p1 = jnp.max(jnp.where(em, p_exp, -1.0), axis=-1, keepdims=True)
    i1 = jnp.min(jnp.where(em & (p_exp == p1), lane, big), axis=-1, keepdims=True)
    em2 = em & (lane != i1)
    p2 = jnp.max(jnp.where(em2, p_exp, -1.0), axis=-1, keepdims=True)
    i2 = jnp.min(jnp.where(em2 & (p_exp == p2), lane, big), axis=-1, keepdims=True)
    tot = p1 + p2
    comb = jnp.where(lane == i1, p_group * p1 / tot, jnp.where(lane == i2, p_group * p2 / tot, 0.0))
    comb_ref[...] = jnp.where(lane_i == GROUP_LANE, g_sel, comb)


def _router(h2d, gain, w_hi, w_lo, bias, tm):
    t = h2d.shape[0]
    return pl.pallas_call(
        _router_kernel,
        grid=(t // tm,),
        in_specs=[
            pl.BlockSpec((tm, D_MODEL), lambda i: (i, 0)),
            pl.BlockSpec((1, D_MODEL), lambda i: (0, 0)),
            pl.BlockSpec((D_MODEL, LANES), lambda i: (0, 0)),
            pl.BlockSpec((D_MODEL, LANES), lambda i: (0, 0)),
            pl.BlockSpec((1, LANES), lambda i: (0, 0)),
        ],
        out_specs=[pl.BlockSpec((tm, D_MODEL), lambda i: (i, 0)), pl.BlockSpec((tm, LANES), lambda i: (i, 0))],
        out_shape=[jax.ShapeDtypeStruct((t, D_MODEL), BF16), jax.ShapeDtypeStruct((t, LANES), F32)],
        compiler_params=pltpu.CompilerParams(dimension_semantics=("parallel",), vmem_limit_bytes=VMEM_LIMIT),
        name="router",
    )(h2d, gain, w_hi, w_lo, bias)


def _moe_capacity(tm):
    return max(16, -(-(3 * tm // 8) // 16) * 16)


def _lane_col(x, lane_idx):
    return jnp.sum(jnp.where(_iota(x.shape, 1) == lane_idx, x, 0.0), axis=-1, keepdims=True)


def _expert_ffn(x, wg_ref, wu_ref, wd_ref, ce):
    gate = _dot(x, wg_ref[0])
    hid = (gate * (1.0 / (1.0 + jnp.exp(-gate)))) * _dot(x, wu_ref[0])
    return _dot((hid * ce).astype(BF16), wd_ref[0])


def _moe_kernel(xn_ref, comb_ref, h_ref, wg_ref, wu_ref, wd_ref, gf_ref, o_ref,
                xc_ref, cc_ref, oc_ref, st_ref, cnt_ref, *, cap):
    e = pl.program_id(1)
    g = e // EXPERTS_PER_GROUP
    tm = xn_ref.shape[0]

    @pl.when(e == 0)
    def _():
        o_ref[...] = h_ref[...]

    @pl.when(e % EXPERTS_PER_GROUP == 0)
    def _():
        comb = comb_ref[...]
        member = comb[:, GROUP_LANE:GROUP_LANE + 1] == g.astype(F32)
        mb = jnp.broadcast_to(jnp.where(member, 1.0, 0.0), (tm, LANES))
        cnt_ref[0] = jnp.sum(mb[:, 0:1]).astype(I32)
        lower = jnp.where(_iota((tm, tm), 0) > _iota((tm, tm), 1), 1.0, 0.0).astype(BF16)
        rank_b = _dot(lower, mb.astype(BF16))
        rank_row = rank_b.T[0:1, :]
        m_row = mb.T[0:1, :]
        sel = jnp.where((m_row > 0.5) & (rank_row == _iota((cap, tm), 0).astype(F32)), 1.0, 0.0).astype(BF16)
        st_ref[...] = jnp.where(member & (rank_b[:, 0:1] == _iota((tm, cap), 1).astype(F32)), 1.0, 0.0).astype(BF16)
        xc_ref[...] = _dot(sel, xn_ref[...]).astype(BF16)
        cc_ref[...] = _dot3_exact_lhs(sel, comb)
        oc_ref[...] = jnp.zeros_like(oc_ref)

    fits = cnt_ref[0] <= cap

    @pl.when(fits)
    def _():
        oc_ref[...] += _expert_ffn(xc_ref[...], wg_ref, wu_ref, wd_ref, _lane_col(cc_ref[...], e))

    @pl.when(jnp.logical_not(fits))
    def _():
        o_ref[...] += _expert_ffn(xn_ref[...], wg_ref, wu_ref, wd_ref, _lane_col(comb_ref[...], e))

    @pl.when(fits & (e % EXPERTS_PER_GROUP == EXPERTS_PER_GROUP - 1))
    def _():
        oc = oc_ref[...]
        hi = oc.astype(BF16)
        lo = (oc - hi.astype(F32)).astype(BF16)
        o_ref[...] += _dot(st_ref[...], hi) + _dot(st_ref[...], lo)

    @pl.when(e == N_EXPERTS - 1)
    def _():
        out = o_ref[...]
        r = lax.rsqrt(jnp.mean(out * out, axis=-1, keepdims=True) + RMS_EPS)
        o_ref[...] = (out * r) * gf_ref[...]


def _moe(xn, comb, h2d, w_g, w_u, w_d, gain_final, tm):
    t = xn.shape[0]
    cap = _moe_capacity(tm)
    return pl.pallas_call(
        functools.partial(_moe_kernel, cap=cap),
        grid=(t // tm, N_EXPERTS),
        in_specs=[
            pl.BlockSpec((tm, D_MODEL), lambda i, e: (i, 0)),
            pl.BlockSpec((tm, LANES), lambda i, e: (i, 0)),
            pl.BlockSpec((tm, D_MODEL), lambda i, e: (i, 0), pipeline_mode=pl.Buffered(1)),
            pl.BlockSpec((1, D_MODEL, EXPERT_FF), lambda i, e: (e, 0, 0)),
            pl.BlockSpec((1, D_MODEL, EXPERT_FF), lambda i, e: (e, 0, 0)),
            pl.BlockSpec((1, EXPERT_FF, D_MODEL), lambda i, e: (e, 0, 0)),
            pl.BlockSpec((1, D_MODEL), lambda i, e: (0, 0)),
        ],
        out_specs=pl.BlockSpec((tm, D_MODEL), lambda i, e: (i, 0), pipeline_mode=pl.Buffered(1)),
        out_shape=jax.ShapeDtypeStruct((t, D_MODEL), F32),
        scratch_shapes=[pltpu.VMEM((cap, D_MODEL), BF16),
                        pltpu.VMEM((cap, LANES), F32), pltpu.VMEM((cap, D_MODEL), F32),
                        pltpu.VMEM((tm, cap), BF16), pltpu.SMEM((1,), I32)],
        compiler_params=pltpu.CompilerParams(
            dimension_semantics=("parallel", "arbitrary"), vmem_limit_bytes=VMEM_LIMIT),
        name="moe",
    )(xn, comb, h2d, w_g, w_u, w_d, gain_final)


def _page_specs(block, n_chunk_pages, row_of):
    def make(k):
        def index_map(s, c, pt):
            return (pt[s, c * n_chunk_pages + k],) + row_of
        return pl.BlockSpec(block, index_map)
    return [make(k) for k in range(n_chunk_pages)]


def _idx_sample_kernel(pt_ref, q_ref, w_ref, knew_ref, *rest, past):
    pages, (o_ref, onew_ref) = rest[:IDX_PAGES], rest[IDX_PAGES:]
    c = pl.program_id(1)
    q = q_ref[...].astype(BF16)
    w = w_ref[...]

    def scores(kt):
        rel = jnp.maximum(_dot(q, kt.astype(BF16)), 0.0) * w
        return jnp.sum(rel.reshape(IDX_HEADS, 8, kt.shape[1]), axis=0) * IDX_SCALE

    kt = jnp.concatenate([jnp.concatenate([p[0], p[0]], axis=0) for p in pages], axis=1)
    o_ref[...] = scores(kt)

    @pl.when(c == 0)
    def _():
        t = _iota((8, LANES), 0) & 3
        j = _iota((8, LANES), 1)
        onew_ref[...] = jnp.where(j <= t, scores(knew_ref[...]), NEG)


def _idx_sample(pt, q_rows, w_col, knew_t, kidx_t):
    ns, n_pages = pt.shape
    n_chunks = n_pages // IDX_PAGES
    past = n_pages * PAGE
    width = IDX_PAGES * PAGE
    return pl.pallas_call(
        functools.partial(_idx_sample_kernel, past=past),
        grid_spec=pltpu.PrefetchScalarGridSpec(
            num_scalar_prefetch=1, grid=(ns, n_chunks),
            in_specs=[
                pl.BlockSpec((None, LANES, LANES), lambda s, c, pt: (s, 0, 0)),
                pl.BlockSpec((None, LANES, 1), lambda s, c, pt: (s, 0, 0)),
                pl.BlockSpec((None, LANES, LANES), lambda s, c, pt: (s, 0, 0)),
            ] + _page_specs((1, IDX_DIM, PAGE), IDX_PAGES, (0, 0)),
            out_specs=[pl.BlockSpec((None, 8, width), lambda s, c, pt: (s, 0, c)),
                       pl.BlockSpec((None, 8, LANES), lambda s, c, pt: (s, 0, 0))]),
        out_shape=[jax.ShapeDtypeStruct((ns, 8, past), F32), jax.ShapeDtypeStruct((ns, 8, LANES), F32)],
        compiler_params=pltpu.CompilerParams(
            dimension_semantics=("parallel", "arbitrary"), vmem_limit_bytes=VMEM_LIMIT),
        name="idx_sample",
    )(pt, q_rows, w_col, knew_t, *([kidx_t] * IDX_PAGES))


def _topk_sample_kernel(s_ref, snew_ref, m_ref, mnew_ref, *, past, k):
    n = s_ref.shape[0] * 8
    s = jnp.concatenate([s_ref[...].reshape(n, past), snew_ref[...].reshape(n, LANES)], axis=1)
    sel = _topk_mask(s, k, int(np.ceil(np.log2(past + LANES))))
    sel = jnp.where(sel, 1.0, 0.0)
    m_ref[...] = sel[:, :past].reshape(m_ref.shape)
    mnew_ref[...] = sel[:, past:].reshape(mnew_ref.shape)


def _topk_sample(scores, scores_new, k):
    ns, _, past = scores.shape
    sps = TOPK_SEQS_PER_STEP if ns % TOPK_SEQS_PER_STEP == 0 else 1
    return pl.pallas_call(
        functools.partial(_topk_sample_kernel, past=past, k=k),
        grid=(ns // sps,),
        in_specs=[pl.BlockSpec((sps, 8, past), lambda s: (s, 0, 0)),
                  pl.BlockSpec((sps, 8, LANES), lambda s: (s, 0, 0))],
        out_specs=[pl.BlockSpec((sps, 8, past), lambda s: (s, 0, 0)),
                   pl.BlockSpec((sps, 8, LANES), lambda s: (s, 0, 0))],
        out_shape=[jax.ShapeDtypeStruct((ns, 8, past), F32), jax.ShapeDtypeStruct((ns, 8, LANES), F32)],
        compiler_params=pltpu.CompilerParams(dimension_semantics=("parallel",), vmem_limit_bytes=VMEM_LIMIT),
        name="topk_sample",
    )(scores, scores_new)


def _rows16_from_rows8(g):
    return jnp.where(_iota((16, 8), 1) == (g * 4 + (_iota((16, 8), 0) >> 2)), 1.0, 0.0).astype(BF16)


def _paged_attn_kernel(pt_ref, q_ref, knew_ref, vnew_ref, m_ref, mnew_ref, *rest, block_mask):
    pages, (o_ref,), (m_sc, l_sc, acc_sc) = rest[:ATTN_PAGES], rest[ATTN_PAGES:ATTN_PAGES + 1], \
        rest[ATTN_PAGES + 1:]
    c = pl.program_id(1)
    width = ATTN_PAGES * PAGE

    @pl.when(c == 0)
    def _():
        m_sc[...] = jnp.full_like(m_sc, NEG)
        l_sc[...] = jnp.zeros_like(l_sc)
        acc_sc[...] = jnp.zeros_like(acc_sc)

    if block_mask:
        nb = m_ref.shape[1]
        first = c * (width // SLC_BLOCK)
        expand = jnp.where(_iota((nb, width), 0) == first + (_iota((nb, width), 1) >> 6), 1.0, 0.0).astype(BF16)
        mask8 = _dot(m_ref[...].astype(BF16), expand)
    else:
        mask8 = m_ref[...]
    mask8 = mask8.astype(BF16)

    def update(g, kg, vg, mask16):
        rs = slice(g * 16, (g + 1) * 16)
        s = _dot_nt(q_ref[g].astype(BF16), kg) * ATT_SCALE
        s = jnp.where(mask16, s, NEG)
        m_old = m_sc[rs]
        m_new = jnp.maximum(m_old, jnp.max(s, axis=-1, keepdims=True))
        p = jnp.where(mask16, jnp.exp(s - m_new), 0.0)
        alpha = jnp.exp(m_old - m_new)
        l_sc[rs] = alpha * l_sc[rs] + jnp.sum(p, axis=-1, keepdims=True)
        acc_sc[rs] = alpha * acc_sc[rs] + _dot(p.astype(BF16), vg)
        m_sc[rs] = m_new

    for g in range(2):
        kg = jnp.concatenate([p[0, pl.ds(2 * g, PAGE, stride=4), :] for p in pages], axis=0).astype(BF16)
        vg = jnp.concatenate([p[0, pl.ds(2 * g + 1, PAGE, stride=4), :] for p in pages], axis=0).astype(BF16)
        update(g, kg, vg, _dot(_rows16_from_rows8(g), mask8) > 0.5)

    @pl.when(c == pl.num_programs(1) - 1)
    def _():
        mnew8 = mnew_ref[...].astype(BF16)
        for g in range(2):
            update(g, knew_ref[g].astype(BF16), vnew_ref[g].astype(BF16), _dot(_rows16_from_rows8(g), mnew8) > 0.5)
        o_ref[...] = acc_sc[...] / jnp.maximum(l_sc[...], 1e-30)


def _paged_attn(pt, q_g, knew, vnew, mask, mask_new, pool, block_mask, name):
    ns, n_pages = pt.shape
    n_chunks = n_pages // ATTN_PAGES
    width = ATTN_PAGES * PAGE
    if block_mask:
        mspec = pl.BlockSpec((None, 8, mask.shape[2]), lambda s, c, pt: (s, 0, 0))
    else:
        mspec = pl.BlockSpec((None, 8, width), lambda s, c, pt: (s, 0, c))
    return pl.pallas_call(
        functools.partial(_paged_attn_kernel, block_mask=block_mask),
        grid_spec=pltpu.PrefetchScalarGridSpec(
            num_scalar_prefetch=1, grid=(ns, n_chunks),
            in_specs=[
                pl.BlockSpec((None, 2, 16, HEAD_DIM), lambda s, c, pt: (s, 0, 0, 0)),
                pl.BlockSpec((None, 2, LANES, HEAD_DIM), lambda s, c, pt: (s, 0, 0, 0)),
                pl.BlockSpec((None, 2, LANES, HEAD_DIM), lambda s, c, pt: (s, 0, 0, 0)),
                mspec,
                pl.BlockSpec((None, 8, LANES), lambda s, c, pt: (s, 0, 0)),
            ] + _page_specs((1, 4 * PAGE, HEAD_DIM), ATTN_PAGES, (0, 0)),
            out_specs=pl.BlockSpec((None, 32, HEAD_DIM), lambda s, c, pt: (s, 0, 0)),
            scratch_shapes=[pltpu.VMEM((32, 1), F32), pltpu.VMEM((32, 1), F32), pltpu.VMEM((32, HEAD_DIM), F32)]),
        out_shape=jax.ShapeDtypeStruct((ns, 32, HEAD_DIM), F32),
        compiler_params=pltpu.CompilerParams(
            dimension_semantics=("parallel", "arbitrary"), vmem_limit_bytes=VMEM_LIMIT),
        name=name,
    )(pt, q_g, knew, vnew, mask, mask_new, *([pool] * ATTN_PAGES))


def _compress_sample_kernel(pt_ref, w_ref, *rest):
    pages, (o_ref,) = rest[:CMP_PAGES], rest[CMP_PAGES:]
    sub = PAGE // CMP_STRIDE
    r = _iota((PAGE, PAGE), 0)
    perm = jnp.where(_iota((PAGE, PAGE), 1) == (r & (sub - 1)) * CMP_STRIDE + (r >> 3), 1.0, 0.0).astype(BF16)
    taps = []
    for p in pages:
        a = jnp.concatenate([p[0, pl.ds(gc, PAGE, stride=4), :] for gc in range(4)], axis=1)
        taps.append(_dot(perm, a.astype(BF16)))
    for c in range(2):
        xs = []
        for g in range(B_KV_HEADS):
            col = (2 * g + c) * LANES
            xs.append(jnp.concatenate(
                [jnp.concatenate([t[l * sub:(l + 1) * sub, col:col + LANES] for l in range(CMP_STRIDE)], axis=1)
                 for t in taps], axis=0))
        pq = _dot(jnp.concatenate(xs, axis=0).astype(BF16), w_ref[c])
        n = CMP_PAGES * sub
        for g in range(B_KV_HEADS):
            o_ref[:, (2 * g + c) * 2 * LANES:(2 * g + c + 1) * 2 * LANES] = pq[g * n:(g + 1) * n]


def _compress_sample(pt, w_cat, pool):
    ns, n_pages = pt.shape
    n_chunks = n_pages // CMP_PAGES
    rows = CMP_PAGES * (PAGE // CMP_STRIDE)
    return pl.pallas_call(
        _compress_sample_kernel,
        grid_spec=pltpu.PrefetchScalarGridSpec(
            num_scalar_prefetch=1, grid=(ns, n_chunks),
            in_specs=[pl.BlockSpec((2, CMP_STRIDE * HEAD_DIM, 2 * LANES), lambda s, c, pt: (0, 0, 0))]
            + _page_specs((1, 4 * PAGE, HEAD_DIM), CMP_PAGES, (0, 0)),
            out_specs=pl.BlockSpec((None, rows, 8 * LANES), lambda s, c, pt: (s, c, 0))),
        out_shape=jax.ShapeDtypeStruct((ns, n_chunks * rows, 8 * LANES), F32),
        compiler_params=pltpu.CompilerParams(
            dimension_semantics=("parallel", "arbitrary"), vmem_limit_bytes=VMEM_LIMIT),
        name="compress_sample",
    )(pt, w_cat, *([pool] * CMP_PAGES))


def _nsa_select_sample_kernel(pq_ref, w_ref, pe_ref, tab_ref, q_ref, oc_ref, bm_ref, *, past, n_blk_pad):
    nsub = past // CMP_STRIDE
    total = past + 4
    n_cmp = (total - CMP_BLOCK) // CMP_STRIDE + 1
    n_slc = -(-total // SLC_BLOCK)
    row_t = _iota((16, 1), 0) >> 2
    c_end = _iota((1, nsub), 1) * CMP_STRIDE + (CMP_BLOCK - 1)
    cmask = (c_end <= past + row_t) & (_iota((1, nsub), 1) < n_cmp)
    p_rows = []
    for g in range(B_KV_HEADS):
        comp = []
        for c in range(2):
            base = (2 * g + c) * 2 * LANES
            v = pq_ref[:, base:base + LANES] + pltpu.roll(pq_ref[:, base + LANES:base + 2 * LANES], nsub - 1, 0)
            v = v + _pe_bias(pe_ref, w_ref, c)
            if c == 0:
                v = _rope(v, tab_ref[:, 0:128], tab_ref[:, 128:256], tab_ref[:, 256:384], ROPE_DIM // 2)
            comp.append(v.astype(BF16))
        s = _dot_nt(q_ref[g].astype(BF16), comp[0]) * ATT_SCALE
        p, l = _masked_softmax_parts(s, cmask)
        p = p / jnp.maximum(l, 1e-30)
        oc_ref[g * 16:(g + 1) * 16, :] = _dot(p.astype(BF16), comp[1])
        p_rows.append(p)
    p_all = jnp.concatenate(p_rows, axis=0)
    gather = jnp.where(_iota((8, 32), 0) == (_iota((8, 32), 1) >> 2), 1.0, 0.0).astype(BF16)
    p_sum = _dot3_exact_lhs(gather, p_all)
    p_s = _dot3_exact_rhs(p_sum, _cover(nsub, n_blk_pad, n_cmp, n_slc))
    qpos = past + (_iota((8, 1), 0) & 3)
    sel = _topk_mask(_block_scores(p_s, qpos, n_slc), min(SLC_TOPN, n_slc), int(np.ceil(np.log2(n_blk_pad))))
    bm_ref[...] = jnp.where(sel, 1.0, 0.0)


def _nsa_select_sample(pq, w_cat, pe_r, tab_c, q_g, past):
    ns, nsub, _ = pq.shape
    n_slc = -(-(past + 4) // SLC_BLOCK)
    n_blk_pad = -(-n_slc // LANES) * LANES
    return pl.pallas_call(
        functools.partial(_nsa_select_sample_kernel, past=past, n_blk_pad=n_blk_pad),
        grid=(ns,),
        in_specs=[
            pl.BlockSpec((None, nsub, 8 * LANES), lambda s: (s, 0, 0)),
            pl.BlockSpec((2, CMP_STRIDE * HEAD_DIM, 2 * LANES), lambda s: (0, 0, 0)),
            pl.BlockSpec((2, 8, CMP_STRIDE * HEAD_DIM), lambda s: (0, 0, 0)),
            pl.BlockSpec((nsub, 3 * LANES), lambda s: (0, 0)),
            pl.BlockSpec((None, 2, 16, HEAD_DIM), lambda s: (s, 0, 0, 0)),
        ],
        out_specs=[pl.BlockSpec((None, 32, HEAD_DIM), lambda s: (s, 0, 0)),
                   pl.BlockSpec((None, 8, n_blk_pad), lambda s: (s, 0, 0))],
        out_shape=[jax.ShapeDtypeStruct((ns, 32, HEAD_DIM), F32), jax.ShapeDtypeStruct((ns, 8, n_blk_pad), F32)],
        compiler_params=pltpu.CompilerParams(dimension_semantics=("parallel",), vmem_limit_bytes=VMEM_LIMIT),
        name="nsa_select_sample",
    )(pq, w_cat, pe_r, tab_c, q_g)


def _window_sample_kernel(win_ref, knew_ref, vnew_ref, q_ref, oc_ref, os_ref, gate_ref, o_ref, *, wb):
    row_t = _iota((16, 1), 0) >> 2
    kpos = _iota((1, wb), 1)
    mask_old = kpos > row_t + (wb - WINDOW)
    mask_new = _iota((1, LANES), 1) <= row_t
    for g in range(B_KV_HEADS):
        rs = slice(g * 16, (g + 1) * 16)
        q = q_ref[g].astype(BF16)
        kw = win_ref[pl.ds(2 * g, wb, stride=4), :].astype(BF16)
        vw = win_ref[pl.ds(2 * g + 1, wb, stride=4), :].astype(BF16)
        s = jnp.concatenate([_dot_nt(q, kw), _dot_nt(q, knew_ref[g].astype(BF16))], axis=1) * ATT_SCALE
        mask = jnp.concatenate([jnp.where(mask_old, 1.0, 0.0), jnp.where(mask_new, 1.0, 0.0)], axis=1) > 0.5
        p, l = _masked_softmax_parts(s, mask)
        o_w = (_dot(p[:, :wb].astype(BF16), vw) + _dot(p[:, wb:].astype(BF16), vnew_ref[g].astype(BF16)))
        o_w = o_w / jnp.maximum(l, 1e-30)
        gt = gate_ref[rs]
        o_ref[rs] = gt[:, 0:1] * oc_ref[rs] + gt[:, 1:2] * os_ref[rs] + gt[:, 2:3] * o_w


def _window_sample(win_rows, knew, vnew, q_g, o_c, o_s, gates, wb):
    ns = q_g.shape[0]
    blk = lambda *shape: pl.BlockSpec((None,) + shape, lambda s: (s,) + (0,) * len(shape))
    return pl.pallas_call(
        functools.partial(_window_sample_kernel, wb=wb),
        grid=(ns,),
        in_specs=[blk(4 * wb, HEAD_DIM), blk(2, LANES, HEAD_DIM), blk(2, LANES, HEAD_DIM), blk(2, 16, HEAD_DIM),
                  blk(32, HEAD_DIM), blk(32, HEAD_DIM), blk(32, LANES)],
        out_specs=blk(32, HEAD_DIM),
        out_shape=jax.ShapeDtypeStruct((ns, 32, HEAD_DIM), F32),
        compiler_params=pltpu.CompilerParams(dimension_semantics=("parallel",), vmem_limit_bytes=VMEM_LIMIT),
        name="window_sample",
    )(win_rows, knew, vnew, q_g, o_c, o_s, gates)


def _rope_table(pos, half, period):
    inv = ROPE_THETA ** (-jnp.arange(half, dtype=F32) / half)
    ang = pos.astype(F32)[:, None] * inv[None, :]
    cos, sin = jnp.cos(ang), jnp.sin(ang)
    lane = np.arange(LANES) % period
    idx = lane % half
    lo = jnp.asarray(lane < half)[None, :]
    hi = jnp.asarray((lane >= half) & (lane < 2 * half))[None, :]
    c = jnp.where(lo | hi, cos[:, idx], 1.0)
    a = jnp.where(lo, -sin[:, idx], 0.0)
    b = jnp.where(hi, sin[:, idx], 0.0)
    return jnp.concatenate([c, a, b], axis=1)


def _proj_tables(pos):
    return jnp.concatenate([_rope_table(pos, ROPE_DIM // 2, LANES), _rope_table(pos, IDX_ROPE_DIM // 2, IDX_DIM)],
                           axis=1)


def _chunk_kinds():
    kinds = np.zeros((N_CHUNKS,), np.int32)
    kinds[C_GM:C_GM + 32] = K_SIGMOID
    kinds[C_QA:C_QA + 8] = K_ROPE_HEAD
    kinds[C_QI:C_QI + 8] = K_ROPE_IDX
    kinds[C_QB:C_QB + 8] = K_ROPE_HEAD
    for base in (C_KVA, C_KVS, C_KVW):
        kinds[base] = K_ROPE_HEAD
        kinds[base + 2] = K_ROPE_HEAD
    kinds[C_KIW] = K_ROPE_IDX_LOW
    kinds[C_GN] = K_SIGMOID
    return jnp.asarray(kinds)


def _pack_w_in(w):
    sizes = [("q_a", 1024), ("kv_a", 512), ("q_i", 1024), ("k_i", 64), ("w_i", 16), ("q_b", 1024), ("kv_c", 512),
             ("kv_s", 512), ("kv_w", 512), ("g_n", 24), ("g_m", 4096)]
    parts, off = {}, 0
    for name, n in sizes:
        parts[name] = w[:, off:off + n]
        off += n
    z = lambda n: jnp.zeros((w.shape[0], n), w.dtype)
    cols = [parts["g_m"], parts["q_a"], parts["q_i"], parts["q_b"], parts["kv_a"], parts["kv_c"], parts["kv_s"],
            parts["kv_w"], parts["k_i"], parts["w_i"], z(48), parts["g_n"], z(104), z(2 * LANES)]
    return jnp.concatenate(cols, axis=1).astype(BF16)


def _cols(y, chunk, n):
    return y[:, chunk * LANES:chunk * LANES + n]


def _group_rows(q):
    ns = q.shape[0]
    return q.reshape(ns, 4, 2, GQA, HEAD_DIM).transpose(0, 2, 1, 3, 4).reshape(ns, 2, 16, HEAD_DIM)


def _new_kv(kv):
    ns = kv.shape[0]
    kv = kv.reshape(ns, 4, 2, 2, HEAD_DIM).transpose(0, 2, 3, 1, 4)
    kv = jnp.pad(kv, ((0, 0), (0, 0), (0, 0), (0, LANES - 4), (0, 0)))
    return kv[:, :, 0], kv[:, :, 1]


def _ungroup_rows(o):
    ns = o.shape[0]
    return o.reshape(ns, 2, 4, GQA, HEAD_DIM).transpose(0, 2, 1, 3, 4).reshape(ns * 4, A_HEADS * HEAD_DIM)


def kernel(x_prompt, x_sample, cache_kv_a, cache_kidx, cache_kv_cmp, cache_kv_slc, state_kv_win, page_table,
           norm_mix, w_in, w_phi, pe_phi, w_o_a, w_o_b, w_out, norm_ffn, w_router_group, b_router_group,
           w_router_expert, b_router_expert, w_gate, w_up, w_down, norm_final):
    batch, seq, _ = x_prompt.shape
    ns, dec, _ = x_sample.shape
    assert dec == 4 and norm_mix.shape[0] == 1, "kernel is written for DEC_SEQ=4 and DEPTH=1"
    n_pool = cache_kv_a.shape[1]
    n_pages = page_table.shape[1]
    past = n_pages * PAGE
    wb = state_kv_win.shape[2]

    w_packed = _pack_w_in(w_in[0])
    kinds = _chunk_kinds()
    wp = w_phi[0].astype(BF16)
    w_cat = jnp.concatenate([wp[:, :CMP_STRIDE].reshape(2, CMP_STRIDE * HEAD_DIM, HEAD_DIM),
                             wp[:, CMP_STRIDE:].reshape(2, CMP_STRIDE * HEAD_DIM, HEAD_DIM)], axis=2)
    pe_r = pe_phi[0].transpose(1, 0, 2).reshape(2, 2, CMP_STRIDE * HEAD_DIM)
    pe_r = jnp.pad(pe_r, ((0, 0), (0, 6), (0, 0)))
    woa, wob, wout = w_o_a[0].astype(BF16), w_o_b[0].astype(BF16), w_out[0].astype(BF16)
    w_r = jnp.concatenate([w_router_expert[0], w_router_group[0],
                           jnp.zeros((D_MODEL, LANES - N_EXPERTS - N_GROUPS), F32)], axis=1)
    w_r_hi = w_r.astype(BF16)
    w_r_lo = (w_r - w_r_hi.astype(F32)).astype(BF16)
    b_r = jnp.concatenate([b_router_expert[0], b_router_group[0],
                           jnp.zeros((LANES - N_EXPERTS - N_GROUPS,), F32)])[None, :]
    w_g, w_u, w_d = w_gate[0].astype(BF16), w_up[0].astype(BF16), w_down[0].astype(BF16)
    g_mix, g_ffn, g_fin = norm_mix[0][None, :], norm_ffn[0][None, :], norm_final[None, :]

    xp = x_prompt.reshape(batch * seq, D_MODEL)
    tm_p = 1024
    y_p = _project(xp, g_mix, w_packed, kinds, _proj_tables(jnp.arange(seq)), tm_p, seq // tm_p)
    o_a_p = _dsa_prompt(y_p, batch, seq)
    nsub_p = seq // CMP_STRIDE
    tab_cp = _rope_table(jnp.arange(nsub_p) * CMP_STRIDE + CMP_BLOCK - 1, ROPE_DIM // 2, LANES)
    cmp_p = _compress_prompt(y_p, w_cat, pe_r, tab_cp, batch, seq)
    o_b_p = _nsa_prompt(y_p, cmp_p, batch, seq)
    h_p = _merge(xp, y_p, o_a_p, o_b_p, woa, wob, wout, 256)
    xn_p, comb_p = _router(h_p, g_ffn, w_r_hi, w_r_lo, b_r, 512)
    y_prompt = _moe(xn_p, comb_p, h_p, w_g, w_u, w_d, g_fin, 1024).reshape(batch, seq, D_MODEL)

    xs = x_sample.reshape(ns * 4, D_MODEL)
    tm_s = ns * 4
    pos_s = past + (jnp.arange(ns * 4) % 4)
    y_s = _project(xs, g_mix, w_packed, kinds, _proj_tables(pos_s), tm_s, 1)
    pt = page_table.astype(I32)
    qi = _cols(y_s, C_QI, 1024).reshape(ns, 4, IDX_HEADS, IDX_DIM).transpose(0, 2, 1, 3)
    qi = jnp.concatenate([qi, qi], axis=2)
    par = (jnp.arange(IDX_HEADS) % 2)[None, :, None, None, None]
    q_rows = jnp.where(par == jnp.arange(2)[None, None, None, :, None], qi[:, :, :, None, :], 0.0)
    q_rows = q_rows.reshape(ns, IDX_HEADS * 8, 2 * IDX_DIM)
    kiw = _cols(y_s, C_KIW, LANES).reshape(ns, 4, LANES)
    wi = kiw[:, :, IDX_DIM:IDX_DIM + IDX_HEADS].transpose(0, 2, 1)
    w_col = jnp.concatenate([wi, wi], axis=2).reshape(ns, IDX_HEADS * 8, 1)
    knew_t = jnp.pad(kiw[:, :, :IDX_DIM].transpose(0, 2, 1), ((0, 0), (0, 0), (0, LANES - 4)))
    knew_t = jnp.concatenate([knew_t, knew_t], axis=1)
    kidx_t = jnp.swapaxes(cache_kidx[0], 1, 2)
    sc, sc_new = _idx_sample(pt, q_rows, w_col, knew_t, kidx_t)
    m_a, m_a_new = _topk_sample(sc, sc_new, min(DSA_TOPK, (past + 4) // 4))
    q_a = _group_rows(_cols(y_s, C_QA, 1024).reshape(ns, 4, A_HEADS, HEAD_DIM))
    ka_new, va_new = _new_kv(_cols(y_s, C_KVA, 512).reshape(ns, 4, 512))
    o_a_s = _paged_attn(pt, q_a, ka_new, va_new, m_a, m_a_new, cache_kv_a.reshape(n_pool, 4 * PAGE, HEAD_DIM),
                        False, "dsa_sample")
    q_b = _group_rows(_cols(y_s, C_QB, 1024).reshape(ns, 4, B_HEADS, HEAD_DIM))
    pq = _compress_sample(pt, w_cat, cache_kv_cmp.reshape(n_pool, 4 * PAGE, HEAD_DIM))
    nsub_s = past // CMP_STRIDE
    tab_cs = _rope_table(jnp.arange(nsub_s) * CMP_STRIDE + CMP_BLOCK - 1, ROPE_DIM // 2, LANES)
    o_c, blk_mask = _nsa_select_sample(pq, w_cat, pe_r, tab_cs, q_b, past)
    ks_new, vs_new = _new_kv(_cols(y_s, C_KVS, 512).reshape(ns, 4, 512))
    first_new = past // SLC_BLOCK
    t_row = (jnp.arange(8) & 3)[None, :, None]
    m_s_new = jnp.where(jnp.arange(LANES)[None, None, :] <= t_row, blk_mask[:, :, first_new:first_new + 1], 0.0)
    o_s = _paged_attn(pt, q_b, ks_new, vs_new, blk_mask, m_s_new,
                      cache_kv_slc.reshape(n_pool, 4 * PAGE, HEAD_DIM), True, "slc_sample")
    kw_new, vw_new = _new_kv(_cols(y_s, C_KVW, 512).reshape(ns, 4, 512))
    gates = _group_rows(jnp.pad(_cols(y_s, C_GN, 24).reshape(ns, 4, B_HEADS, 3), ((0, 0),) * 3 + ((0, LANES - 3),)))
    gates = gates.reshape(ns, 32, LANES)
    o_b_s = _window_sample(state_kv_win[0].reshape(ns, 4 * wb, HEAD_DIM), kw_new, vw_new, q_b, o_c, o_s, gates, wb)
    h_s = _merge(xs, y_s, _ungroup_rows(o_a_s), _ungroup_rows(o_b_s), woa, wob, wout, tm_s)
    xn_s, comb_s = _router(h_s, g_ffn, w_r_hi, w_r_lo, b_r, tm_s)
    y_sample = _moe(xn_s, comb_s, h_s, w_g, w_u, w_d, g_fin, tm_s).reshape(ns, 4, D_MODEL)

    def kv_out(y, chunk, lead):
        return _cols(y, chunk, 512).reshape((1,) + lead + (2, 2, HEAD_DIM))

    kv_w_p = _cols(y_p, C_KVW, 512).reshape(batch, seq, 2, 2, HEAD_DIM)
    wb_p = min(WINDOW, seq)
    win_s = jnp.concatenate([state_kv_win[0][:, 4:], _cols(y_s, C_KVW, 512).reshape(ns, 4, 2, 2, HEAD_DIM)], axis=1)
    return (y_prompt, y_sample,
            kv_out(y_p, C_KVA, (batch, seq)), _cols(y_p, C_KIW, IDX_DIM).reshape(1, batch, seq, IDX_DIM),
            kv_out(y_p, C_KVC, (batch, seq)), kv_out(y_p, C_KVS, (batch, seq)), kv_w_p[None, :, seq - wb_p:],
            kv_out(y_s, C_KVA, (ns, 4)), _cols(y_s, C_KIW, IDX_DIM).reshape(1, ns, 4, IDX_DIM),
            kv_out(y_s, C_KVC, (ns, 4)), kv_out(y_s, C_KVS, (ns, 4)), win_s[None])
```

```python
import functools

import numpy as np
import jax
import jax.numpy as jnp
from jax import lax
from jax.experimental import pallas as pl
from jax.experimental.pallas import tpu as pltpu

F32 = jnp.float32
BF16 = jnp.bfloat16
I32 = jnp.int32

LANES = 128
VMEM_LIMIT = 56 * 1024 * 1024

D_MODEL = 2048
HEAD_DIM = 128
ROPE_DIM = HEAD_DIM // 4
ROPE_THETA = 500000.0
A_HEADS = 8
A_KV_HEADS = 2
IDX_HEADS = 16
IDX_DIM = 64
IDX_ROPE_DIM = IDX_DIM // 4
DSA_TOPK = 256
B_HEADS = 8
B_KV_HEADS = 2
GQA = 4
CMP_BLOCK = 32
CMP_STRIDE = 16
SLC_BLOCK = 64
SLC_TOPN = 16
SLC_LOCAL = 2
WINDOW = 512
PAGE = 128
N_GROUPS = 4
EXPERTS_PER_GROUP = 4
N_EXPERTS = 16
GROUP_LANE = N_EXPERTS
EXPERT_FF = 512
Q_BLOCK = 128
RMS_EPS = 1e-6
NEG = -1e30
FORCE_BONUS = 1e4
ATT_SCALE = HEAD_DIM ** -0.5
IDX_SCALE = (IDX_DIM * IDX_HEADS) ** -0.5
IDX_PAGES = 64
ATTN_PAGES = 64
CMP_PAGES = 32
TOPK_SEQS_PER_STEP = 4
MAX_CAUSAL_EXTENTS = 4

C_GM, C_QA, C_QI, C_QB, C_KVA, C_KVC, C_KVS, C_KVW, C_KIW, C_GN = 0, 32, 40, 48, 56, 60, 64, 68, 72, 73
N_CHUNKS = 76
NP = N_CHUNKS * LANES
K_NONE, K_ROPE_HEAD, K_ROPE_IDX, K_SIGMOID, K_ROPE_IDX_LOW = 0, 1, 2, 3, 4

NT_DIMS = (((1,), (1,)), ((), ()))


def _dot(a, b):
    return jnp.dot(a, b, preferred_element_type=F32)


def _dot_nt(a, b):
    return lax.dot_general(a, b, NT_DIMS, preferred_element_type=F32)


def _split3(x):
    hi = x.astype(BF16)
    r1 = x - hi.astype(F32)
    mid = r1.astype(BF16)
    lo = (r1 - mid.astype(F32)).astype(BF16)
    return hi, mid, lo


def _dot3_exact_rhs(x, rhs_bf16):
    hi, mid, lo = _split3(x)
    return _dot(hi, rhs_bf16) + _dot(mid, rhs_bf16) + _dot(lo, rhs_bf16)


def _dot3_exact_lhs(lhs_bf16, x):
    hi, mid, lo = _split3(x)
    return _dot(lhs_bf16, hi) + _dot(lhs_bf16, mid) + _dot(lhs_bf16, lo)


def _iota(shape, dim):
    return lax.broadcasted_iota(I32, shape, dim)


def _count(mask):
    return jnp.sum(jnp.where(mask, 1.0, 0.0), axis=-1, keepdims=True)


def _topk_mask(score, k, idx_bits):
    rows = score.shape[0]
    imin = jnp.int32(-2 ** 31)
    kf = jnp.float32(k)
    n_part = 2 if rows % 16 == 0 else 1
    pr = rows // n_part
    parts = [score[h * pr:(h + 1) * pr] for h in range(n_part)]

    def float_of(u):
        key = u ^ imin
        return pltpu.bitcast(key ^ ((key >> 31) & jnp.int32(0x7FFFFFFF)), F32)

    def value_bit(i, tus):
        bit = lax.shift_left(jnp.int32(1), 31 - i)
        out = []
        for s_h, tu in zip(parts, tus):
            cand = tu | bit
            cnt = _count(s_h >= float_of(cand))
            out.append(jnp.where(cnt >= kf, cand, tu))
        return tuple(out)

    tus = lax.fori_loop(0, 32, value_bit, tuple(jnp.zeros((pr, 1), I32) for _ in range(n_part)))
    tu = jnp.concatenate(tus, axis=0) if n_part > 1 else tus[0]
    ge_hi = score >= float_of(tu + 1)
    bucket = (score >= float_of(tu)) & jnp.logical_not(ge_hi)
    want = kf - _count(ge_hi)

    def unsettled(c):
        return jnp.max(jnp.where(c[1] < want, 1.0, 0.0)) > 0.0

    def next_value(c):
        v, cnt = c
        below = jnp.max(jnp.where(bucket & (score < v), score, -jnp.inf), axis=-1, keepdims=True)
        v = jnp.where(cnt < want, below, v)
        return v, _count(bucket & (score >= v))

    thr, _ = lax.while_loop(unsettled, next_value, (jnp.full((rows, 1), jnp.inf, F32), jnp.zeros((rows, 1), F32)))
    gt = score > thr
    eq = score == thr
    need = kf - _count(gt)
    idx = _iota(score.shape, 1)

    def index_bit(i, j):
        cand = j | lax.shift_left(jnp.int32(1), idx_bits - 1 - i)
        cnt = _count(eq & (idx < cand))
        return jnp.where(cnt < need, cand, j)

    def tie_search():
        return lax.fori_loop(0, idx_bits, index_bit, jnp.zeros((rows, 1), I32))

    def take_all():
        return jnp.full((rows, 1), 2 ** idx_bits, I32)

    surplus = jnp.max(jnp.abs(_count(eq) - need))
    j = lax.cond(surplus == 0.0, take_all, tie_search)
    return gt | (eq & (idx <= j))


def _masked_softmax_parts(s, mask):
    s = jnp.where(mask, s, NEG)
    m = jnp.max(s, axis=-1, keepdims=True)
    p = jnp.where(mask, jnp.exp(s - m), 0.0)
    return p, jnp.sum(p, axis=-1, keepdims=True)


def _attend_heads(qg, k, v, mask):
    s = _dot_nt(qg, k) * ATT_SCALE
    ps, ls = [], []
    for r in range(GQA):
        p, l = _masked_softmax_parts(s[r * Q_BLOCK:(r + 1) * Q_BLOCK], mask)
        ps.append(p.astype(BF16))
        ls.append(l)
    o = _dot(jnp.concatenate(ps, axis=0), v)
    return o / jnp.maximum(jnp.concatenate(ls, axis=0), 1e-30)


def _rope(y, c, a, b, half):
    return y * c + pltpu.roll(y, LANES - half, 1) * a + pltpu.roll(y, half, 1) * b


def _proj_kernel(kind_ref, x_ref, g_ref, w_ref, tab_ref, o_ref, *rest, tn):
    kv_refs, xn_ref = rest[:4], rest[4]
    assert tn == 4 * LANES, "each K/V part must be exactly one column block"
    j = pl.program_id(1)

    @pl.when(j == 0)
    def _():
        xf = x_ref[...]
        r = lax.rsqrt(jnp.mean(xf * xf, axis=-1, keepdims=True) + RMS_EPS)
        xn_ref[...] = ((xf * r) * g_ref[...]).astype(BF16)

    lane = _iota((1, LANES), 1)
    for n0 in range(0, tn, 2 * LANES):
        acc = _dot(xn_ref[...], w_ref[:, n0:n0 + 2 * LANES])
        for c in range(2):
            cs = slice(n0 + c * LANES, n0 + (c + 1) * LANES)
            y = acc[:, c * LANES:(c + 1) * LANES]
            k = kind_ref[j * (tn // LANES) + n0 // LANES + c]
            is_idx = (k == K_ROPE_IDX) | (k == K_ROPE_IDX_LOW)
            is_rope = (k == K_ROPE_HEAD) | is_idx
            t = jnp.where(is_idx, 1, 0)
            half = jnp.where(is_idx, IDX_ROPE_DIM // 2, ROPE_DIM // 2)
            roped = (y * tab_ref[t, :, 0:128] + pltpu.roll(y, LANES - half, 1) * tab_ref[t, :, 128:256]
                     + pltpu.roll(y, half, 1) * tab_ref[t, :, 256:384])
            rope_lanes = jnp.where(is_rope, jnp.where(k == K_ROPE_IDX_LOW, IDX_DIM, LANES), 0)
            other = jnp.where(k == K_SIGMOID, 1.0 / (1.0 + jnp.exp(-y)), y)
            o_ref[:, cs] = jnp.where(lane < rope_lanes, roped, other)

    tm = o_ref.shape[0]
    for chunk0, kv_ref in ((C_KVA, kv_refs[0]), (C_KVC, kv_refs[1]), (C_KVS, kv_refs[2]), (C_KVW, kv_refs[3])):
        @pl.when(j == chunk0 // (tn // LANES))
        def _(kv_ref=kv_ref):
            for c in range(4):
                kv_ref[pl.ds(c, tm, stride=4), :] = o_ref[:, c * LANES:(c + 1) * LANES]


def _project(x2d, gain, w_packed, kinds, tables, tm, n_tab_blocks):
    t = x2d.shape[0]
    tn = 512
    grid = (t // tm, NP // tn)
    return pl.pallas_call(
        functools.partial(_proj_kernel, tn=tn),
        grid_spec=pltpu.PrefetchScalarGridSpec(
            num_scalar_prefetch=1, grid=grid,
            in_specs=[
                pl.BlockSpec((tm, D_MODEL), lambda i, j, kr: (i, 0)),
                pl.BlockSpec((1, D_MODEL), lambda i, j, kr: (0, 0)),
                pl.BlockSpec((D_MODEL, tn), lambda i, j, kr: (0, j)),
                pl.BlockSpec((2, tm, 3 * LANES), lambda i, j, kr: (0, i % n_tab_blocks, 0)),
            ],
            out_specs=[pl.BlockSpec((tm, tn), lambda i, j, kr: (i, j))]
            + [pl.BlockSpec((4 * tm, LANES), lambda i, j, kr: (i, 0), pipeline_mode=pl.Buffered(1))] * 4,
            scratch_shapes=[pltpu.VMEM((tm, D_MODEL), BF16)]),
        out_shape=[jax.ShapeDtypeStruct((t, NP), F32)] + [jax.ShapeDtypeStruct((4 * t, LANES), F32)] * 4,
        compiler_params=pltpu.CompilerParams(
            dimension_semantics=("parallel", "arbitrary"), vmem_limit_bytes=VMEM_LIMIT),
        name="in_proj",
    )(kinds, x2d, gain, w_packed, tables)


def _dsa_prompt_body(qa_ref, qi_ref, kiwq_ref, kiwk_ref, kva_ref, o_ref, q0, ext, k_top):
    lane = _iota((1, LANES), 1)
    qpos = q0 + _iota((Q_BLOCK, 1), 0)
    causal = _iota((1, ext), 1) <= qpos
    if ext <= k_top:
        sel = causal
    else:
        kk = kiwk_ref[0:ext, :]
        k_dup = jnp.where(lane < IDX_DIM, kk, pltpu.roll(kk, IDX_DIM, 1)).astype(BF16)
        wi = kiwq_ref[...]
        score = jnp.zeros((Q_BLOCK, ext), F32)
        for pair in range(IDX_HEADS // 2):
            qp = qi_ref[:, pair * LANES:(pair + 1) * LANES]
            for half in range(2):
                h = 2 * pair + half
                keep = (lane < IDX_DIM) if half == 0 else (lane >= IDX_DIM)
                qm = jnp.where(keep, qp, 0.0).astype(BF16)
                rel = jnp.maximum(_dot_nt(qm, k_dup), 0.0)
                score = score + rel * wi[:, IDX_DIM + h:IDX_DIM + h + 1]
        score = jnp.where(causal, score * IDX_SCALE, NEG)
        sel = _topk_mask(score, k_top, int(np.ceil(np.log2(ext)))) & causal

    for g in range(A_KV_HEADS):
        kg = kva_ref[0:ext, (2 * g) * LANES:(2 * g + 1) * LANES].astype(BF16)
        vg = kva_ref[0:ext, (2 * g + 1) * LANES:(2 * g + 2) * LANES].astype(BF16)
        qg = jnp.concatenate(
            [qa_ref[:, (g * GQA + r) * LANES:(g * GQA + r + 1) * LANES] for r in range(GQA)], axis=0).astype(BF16)
        o = _attend_heads(qg, kg, vg, sel)
        for r in range(GQA):
            o_ref[:, (g * GQA + r) * LANES:(g * GQA + r + 1) * LANES] = o[r * Q_BLOCK:(r + 1) * Q_BLOCK]


def _causal_extents(seq):
    exts = [seq]
    while exts[0] // 2 >= 2 * Q_BLOCK and len(exts) < MAX_CAUSAL_EXTENTS:
        exts.insert(0, exts[0] // 2)
    return exts


def _for_each_extent(i, seq, body):
    lo = 0
    for ext in _causal_extents(seq):
        hi = ext // Q_BLOCK

        @pl.when((i >= lo) & (i < hi))
        def _(ext=ext):
            body(ext)
        lo = hi


def _dsa_prompt_kernel(qa_ref, qi_ref, kiwq_ref, kiwk_ref, kva_ref, o_ref, *, seq):
    i = pl.program_id(1)
    _for_each_extent(i, seq, lambda ext: _dsa_prompt_body(
        qa_ref, qi_ref, kiwq_ref, kiwk_ref, kva_ref, o_ref, i * Q_BLOCK, ext, min(DSA_TOPK, seq // 4)))


def _dsa_prompt(y, batch, seq):
    nqb = seq // Q_BLOCK
    return pl.pallas_call(
        functools.partial(_dsa_prompt_kernel, seq=seq),
        grid=(batch, nqb),
        in_specs=[
            pl.BlockSpec((Q_BLOCK, 8 * LANES), lambda b, i: (b * nqb + i, C_QA // 8)),
            pl.BlockSpec((Q_BLOCK, 8 * LANES), lambda b, i: (b * nqb + i, C_QI // 8)),
            pl.BlockSpec((Q_BLOCK, LANES), lambda b, i: (b * nqb + i, C_KIW)),
            pl.BlockSpec((seq, LANES), lambda b, i: (b, C_KIW)),
            pl.BlockSpec((seq, 4 * LANES), lambda b, i: (b, C_KVA // 4)),
        ],
        out_specs=pl.BlockSpec((Q_BLOCK, 8 * LANES), lambda b, i: (b * nqb + i, 0)),
        out_shape=jax.ShapeDtypeStruct((batch * seq, A_HEADS * HEAD_DIM), F32),
        compiler_params=pltpu.CompilerParams(
            dimension_semantics=("parallel", "arbitrary"), vmem_limit_bytes=VMEM_LIMIT),
        name="dsa_prompt",
    )(y, y, y, y, y)


def _pe_bias(pe_ref, w_ref, c):
    pb = _dot(pe_ref[c].astype(BF16), w_ref[c])
    return pb[0:1, 0:LANES] + pb[1:2, LANES:2 * LANES]


def _compress_prompt_kernel(kvc00_ref, kvc01_ref, kvc10_ref, kvc11_ref, w_ref, pe_ref, tab_ref, o_ref, *, nsub):
    kvc = ((kvc00_ref, kvc01_ref), (kvc10_ref, kvc11_ref))
    for c in range(2):
        xs = []
        for g in range(B_KV_HEADS):
            xs.append(jnp.concatenate(
                [kvc[g][c][pl.ds(l, nsub, stride=CMP_STRIDE), :] for l in range(CMP_STRIDE)], axis=1))
        pq = _dot(jnp.concatenate(xs, axis=0).astype(BF16), w_ref[c])
        bias = _pe_bias(pe_ref, w_ref, c)
        for g in range(B_KV_HEADS):
            top = pq[g * nsub:(g + 1) * nsub, 0:LANES]
            bot = pq[g * nsub:(g + 1) * nsub, LANES:2 * LANES]
            comp = top + pltpu.roll(bot, nsub - 1, 0) + bias
            if c == 0:
                comp = _rope(comp, tab_ref[:, 0:128], tab_ref[:, 128:256], tab_ref[:, 256:384], ROPE_DIM // 2)
            o_ref[:, (2 * g + c) * LANES:(2 * g + c + 1) * LANES] = comp


def _compress_prompt(y, w_cat, pe_r, tab_c, batch, seq):
    nsub = seq // CMP_STRIDE
    return pl.pallas_call(
        functools.partial(_compress_prompt_kernel, nsub=nsub),
        grid=(batch,),
        in_specs=[
            pl.BlockSpec((seq, LANES), lambda b: (b, C_KVC)),
            pl.BlockSpec((seq, LANES), lambda b: (b, C_KVC + 1)),
            pl.BlockSpec((seq, LANES), lambda b: (b, C_KVC + 2)),
            pl.BlockSpec((seq, LANES), lambda b: (b, C_KVC + 3)),
            pl.BlockSpec((2, CMP_STRIDE * HEAD_DIM, 2 * LANES), lambda b: (0, 0, 0)),
            pl.BlockSpec((2, 8, CMP_STRIDE * HEAD_DIM), lambda b: (0, 0, 0)),
            pl.BlockSpec((nsub, 3 * LANES), lambda b: (0, 0)),
        ],
        out_specs=pl.BlockSpec((nsub, 4 * LANES), lambda b: (b, 0)),
        out_shape=jax.ShapeDtypeStruct((batch * nsub, 4 * LANES), F32),
        compiler_params=pltpu.CompilerParams(dimension_semantics=("parallel",), vmem_limit_bytes=VMEM_LIMIT),
        name="compress_prompt",
    )(y, y, y, y, w_cat, pe_r, tab_c)


def _cover(n_rows, n_cols, n_cmp, n_slc):
    ci = _iota((n_rows, n_cols), 0) * CMP_STRIDE
    sj = _iota((n_rows, n_cols), 1) * SLC_BLOCK
    m = (ci < sj + SLC_BLOCK) & (ci + CMP_BLOCK > sj)
    m = m & (_iota((n_rows, n_cols), 0) < n_cmp) & (_iota((n_rows, n_cols), 1) < n_slc)
    return jnp.where(m, 1.0, 0.0).astype(BF16)


def _block_scores(p_s, qpos_col, n_slc):
    blk = _iota(p_s.shape, 1)
    cur = qpos_col >> 6
    forced = (blk == 0) | ((blk <= cur) & (blk > cur - SLC_LOCAL))
    score = jnp.where(blk <= cur, p_s + FORCE_BONUS * jnp.where(forced, 1.0, 0.0), NEG)
    return jnp.where(blk < n_slc, score, -jnp.inf)


def _select_blocks(p_sum, cover_t, qpos_row, n_slc, n_top):
    hi, mid, lo = _split3(p_sum)
    p_s = (_dot_nt(cover_t, hi) + _dot_nt(cover_t, mid) + _dot_nt(cover_t, lo))[0:n_slc]
    blk = _iota((n_slc, Q_BLOCK), 0)
    cur = qpos_row >> 6
    forced = (blk == 0) | ((blk <= cur) & (blk > cur - SLC_LOCAL))
    score = jnp.where(blk <= cur, p_s + FORCE_BONUS * jnp.where(forced, 1.0, 0.0), NEG)
    rank = jnp.zeros((n_slc, Q_BLOCK), F32)
    for i in range(n_slc):
        si = score[i:i + 1, :]
        rank = rank + jnp.where((si > score) | ((si == score) & (blk > i)), 1.0, 0.0)
    sel_t = jnp.where(rank < n_top, 1.0, 0.0)
    sel_t = jnp.concatenate([sel_t, jnp.zeros((LANES - n_slc, Q_BLOCK), F32)], axis=0)
    return sel_t.T


def _nsa_prompt_body(qb_ref, gn_ref, cmp_ref, kvs_ref, kvw_ref, o_ref, i, ext, seq):
    q0 = i * Q_BLOCK
    nsub = seq // CMP_STRIDE
    n_cmp = (seq - CMP_BLOCK) // CMP_STRIDE + 1
    n_slc = seq // SLC_BLOCK
    qpos = q0 + _iota((Q_BLOCK, 1), 0)
    qpos_row = q0 + _iota((1, Q_BLOCK), 1)
    c_end = _iota((1, nsub), 1) * CMP_STRIDE + (CMP_BLOCK - 1)
    cmask = c_end <= qpos
    cj = _iota((LANES, nsub), 0) * SLC_BLOCK
    cn = _iota((LANES, nsub), 1) * CMP_STRIDE
    cover_t = jnp.where((cn < cj + SLC_BLOCK) & (cn + CMP_BLOCK > cj) & (_iota((LANES, nsub), 1) < n_cmp)
                        & (_iota((LANES, nsub), 0) < n_slc), 1.0, 0.0).astype(BF16)
    kpos = _iota((1, ext), 1)
    expand = jnp.where((_iota((LANES, ext), 1) >> 6) == _iota((LANES, ext), 0), 1.0, 0.0).astype(BF16)
    gates = gn_ref[...]

    for g in range(B_KV_HEADS):
        qg = jnp.concatenate(
            [qb_ref[:, (g * GQA + r) * LANES:(g * GQA + r + 1) * LANES] for r in range(GQA)], axis=0).astype(BF16)
        kc = cmp_ref[:, (2 * g) * LANES:(2 * g + 1) * LANES].astype(BF16)
        vc = cmp_ref[:, (2 * g + 1) * LANES:(2 * g + 2) * LANES].astype(BF16)
        s = _dot_nt(qg, kc) * ATT_SCALE
        ps = []
        for r in range(GQA):
            p, l = _masked_softmax_parts(s[r * Q_BLOCK:(r + 1) * Q_BLOCK], cmask)
            ps.append(p / jnp.maximum(l, 1e-30))
        o_c = _dot(jnp.concatenate(ps, axis=0).astype(BF16), vc)
        sel = _select_blocks(ps[0] + ps[1] + ps[2] + ps[3], cover_t, qpos_row, n_slc, min(SLC_TOPN, n_slc))
        tok = (_dot(sel.astype(BF16), expand) > 0.5) & (kpos <= qpos)
        ks = kvs_ref[0:ext, (2 * g) * LANES:(2 * g + 1) * LANES].astype(BF16)
        vs = kvs_ref[0:ext, (2 * g + 1) * LANES:(2 * g + 2) * LANES].astype(BF16)
        o_s = _attend_heads(qg, ks, vs, tok)
        nwb = WINDOW // Q_BLOCK + 1
        k_parts, v_parts, m_parts = [], [], []
        for w in range(nwb):
            kb = i - (nwb - 1) + w
            start = pl.multiple_of(jnp.maximum(kb, 0) * Q_BLOCK, Q_BLOCK)
            k_parts.append(kvw_ref[pl.ds(start, Q_BLOCK), (2 * g) * LANES:(2 * g + 1) * LANES].astype(BF16))
            v_parts.append(kvw_ref[pl.ds(start, Q_BLOCK), (2 * g + 1) * LANES:(2 * g + 2) * LANES].astype(BF16))
            wpos = kb * Q_BLOCK + _iota((1, Q_BLOCK), 1)
            m_parts.append(jnp.where((wpos <= qpos) & (wpos > qpos - WINDOW) & (wpos >= 0), 1.0, 0.0))
        wmask = jnp.concatenate(m_parts, axis=1) > 0.5
        o_w = _attend_heads(qg, jnp.concatenate(k_parts, axis=0), jnp.concatenate(v_parts, axis=0), wmask)
        for r in range(GQA):
            h = g * GQA + r
            rs = slice(r * Q_BLOCK, (r + 1) * Q_BLOCK)
            o_ref[:, h * LANES:(h + 1) * LANES] = (gates[:, 3 * h:3 * h + 1] * o_c[rs]
                                                   + gates[:, 3 * h + 1:3 * h + 2] * o_s[rs]
                                                   + gates[:, 3 * h + 2:3 * h + 3] * o_w[rs])


def _nsa_prompt_kernel(qb_ref, gn_ref, cmp_ref, kvs_ref, kvw_ref, o_ref, *, seq):
    i = pl.program_id(1)
    _for_each_extent(i, seq, lambda ext: _nsa_prompt_body(
        qb_ref, gn_ref, cmp_ref, kvs_ref, kvw_ref, o_ref, i, ext, seq))


def _nsa_prompt(y, cmp, batch, seq):
    nqb = seq // Q_BLOCK
    nsub = seq // CMP_STRIDE
    return pl.pallas_call(
        functools.partial(_nsa_prompt_kernel, seq=seq),
        grid=(batch, nqb),
        in_specs=[
            pl.BlockSpec((Q_BLOCK, 8 * LANES), lambda b, i: (b * nqb + i, C_QB // 8)),
            pl.BlockSpec((Q_BLOCK, LANES), lambda b, i: (b * nqb + i, C_GN)),
            pl.BlockSpec((nsub, 4 * LANES), lambda b, i: (b, 0)),
            pl.BlockSpec((seq, 4 * LANES), lambda b, i: (b, C_KVS // 4)),
            pl.BlockSpec((seq, 4 * LANES), lambda b, i: (b, C_KVW // 4)),
        ],
        out_specs=pl.BlockSpec((Q_BLOCK, 8 * LANES), lambda b, i: (b * nqb + i, 0)),
        out_shape=jax.ShapeDtypeStruct((batch * seq, B_HEADS * HEAD_DIM), F32),
        compiler_params=pltpu.CompilerParams(
            dimension_semantics=("parallel", "arbitrary"), vmem_limit_bytes=VMEM_LIMIT),
        name="nsa_prompt",
    )(y, y, cmp, y, y)


def _merge_kernel(x_ref, ga_ref, gb_ref, oa_ref, ob_ref, woa_ref, wob_ref, wout_ref, o_ref):
    u = (ga_ref[...] * _dot(oa_ref[...].astype(BF16), woa_ref[...])
         + gb_ref[...] * _dot(ob_ref[...].astype(BF16), wob_ref[...]))
    o_ref[...] = x_ref[...] + _dot(u.astype(BF16), wout_ref[...])


def _merge(x2d, y, o_a, o_b, w_o_a, w_o_b, w_out, tm):
    t = x2d.shape[0]
    const = dict(pipeline_mode=pl.Buffered(1))
    return pl.pallas_call(
        _merge_kernel,
        grid=(t // tm,),
        in_specs=[
            pl.BlockSpec((tm, D_MODEL), lambda i: (i, 0)),
            pl.BlockSpec((tm, D_MODEL), lambda i: (i, 0)),
            pl.BlockSpec((tm, D_MODEL), lambda i: (i, 1)),
            pl.BlockSpec((tm, A_HEADS * HEAD_DIM), lambda i: (i, 0)),
            pl.BlockSpec((tm, B_HEADS * HEAD_DIM), lambda i: (i, 0)),
            pl.BlockSpec((A_HEADS * HEAD_DIM, D_MODEL), lambda i: (0, 0), **const),
            pl.BlockSpec((B_HEADS * HEAD_DIM, D_MODEL), lambda i: (0, 0), **const),
            pl.BlockSpec((D_MODEL, D_MODEL), lambda i: (0, 0), **const),
        ],
        out_specs=pl.BlockSpec((tm, D_MODEL), lambda i: (i, 0)),
        out_shape=jax.ShapeDtypeStruct((t, D_MODEL), F32),
        compiler_params=pltpu.CompilerParams(dimension_semantics=("parallel",), vmem_limit_bytes=VMEM_LIMIT),
        name="merge",
    )(x2d, y, y, o_a, o_b, w_o_a, w_o_b, w_out)


def _router_kernel(h_ref, g_ref, whi_ref, wlo_ref, b_ref, xn_ref, comb_ref):
    hf = h_ref[...]
    r = lax.rsqrt(jnp.mean(hf * hf, axis=-1, keepdims=True) + RMS_EPS)
    xn = (hf * r) * g_ref[...]
    xhi = xn.astype(BF16)
    xn_ref[...] = xhi
    xlo = (xn - xhi.astype(F32)).astype(BF16)
    logits = _dot(xhi, whi_ref[...]) + _dot(xlo, whi_ref[...]) + _dot(xhi, wlo_ref[...]) + b_ref[...]
    lane_i = _iota(logits.shape, 1)
    lane = lane_i.astype(F32)
    big = jnp.float32(1e9)
    gm = (lane_i >= N_EXPERTS) & (lane_i < N_EXPERTS + N_GROUPS)
    gmax = jnp.max(jnp.where(gm, logits, -jnp.inf), axis=-1, keepdims=True)
    gsum = jnp.sum(jnp.where(gm, jnp.exp(logits - gmax), 0.0), axis=-1, keepdims=True)
    p_group = 1.0 / gsum
    g_sel = jnp.min(jnp.where(gm & (logits == gmax), lane, big), axis=-1, keepdims=True) - N_EXPERTS
    em = (lane_i < N_EXPERTS) & ((lane_i >> 2).astype(F32) == g_sel)
    emax = jnp.max(jnp.where(em, logits, -jnp.inf), axis=-1, keepdims=True)
    pe = jnp.where(em, jnp.exp(logits - emax), 0.0)
    p_exp = pe / jnp.sum(pe, axis=-1, keepdims=True)
    p1 = jnp.max(jnp.where(em, p_exp, -1.0), axis=-1, keepdims=True)
    i1 = jnp.min(jnp.where(em & (p_exp == p1), lane, big), axis=-1, keepdims=True)
    em2 = em & (lane != i1)
    p2 = jnp.max(jnp.where(em2, p_exp, -1.0), axis=-1, keepdims=True)
    i2 = jnp.min(jnp.where(em2 & (p_exp == p2), lane, big), axis=-1, keepdims=True)
    tot = p1 + p2
    comb = jnp.where(lane == i1, p_group * p1 / tot, jnp.where(lane == i2, p_group * p2 / tot, 0.0))
    comb_ref[...] = jnp.where(lane_i == GROUP_LANE, g_sel, comb)


def _router(h2d, gain, w_hi, w_lo, bias, tm):
    t = h2d.shape[0]
    return pl.pallas_call(
        _router_kernel,
        grid=(t // tm,),
        in_specs=[
            pl.BlockSpec((tm, D_MODEL), lambda i: (i, 0)),
            pl.BlockSpec((1, D_MODEL), lambda i: (0, 0)),
            pl.BlockSpec((D_MODEL, LANES), lambda i: (0, 0)),
            pl.BlockSpec((D_MODEL, LANES), lambda i: (0, 0)),
            pl.BlockSpec((1, LANES), lambda i: (0, 0)),
        ],
        out_specs=[pl.BlockSpec((tm, D_MODEL), lambda i: (i, 0)), pl.BlockSpec((tm, LANES), lambda i: (i, 0))],
        out_shape=[jax.ShapeDtypeStruct((t, D_MODEL), BF16), jax.ShapeDtypeStruct((t, LANES), F32)],
        compiler_params=pltpu.CompilerParams(dimension_semantics=("parallel",), vmem_limit_bytes=VMEM_LIMIT),
        name="router",
    )(h2d, gain, w_hi, w_lo, bias)


MOE_SEGMENT = 512


def _moe_capacity(rows):
    want = rows / N_GROUPS + 4.5 * np.sqrt(rows * (N_GROUPS - 1)) / N_GROUPS
    return max(16, int(-(-want // 16)) * 16)


def _lane_col(x, lane_idx):
    return jnp.sum(jnp.where(_iota(x.shape, 1) == lane_idx, x, 0.0), axis=-1, keepdims=True)


def _expert_ffn(x, wg_ref, wu_ref, wd_ref, ce):
    gate = _dot(x, wg_ref[0])
    hid = (gate * (1.0 / (1.0 + jnp.exp(-gate)))) * _dot(x, wu_ref[0])
    return _dot((hid * ce).astype(BF16), wd_ref[0])


def _moe_kernel(xn_ref, comb_ref, h_ref, wg_ref, wu_ref, wd_ref, gf_ref, o_ref,
                xc_ref, cc_ref, oc_ref, st_ref, cnt_ref, *, cap, n_seg):
    e = pl.program_id(1)
    g = e // EXPERTS_PER_GROUP
    tm = xn_ref.shape[0]

    @pl.when(e == 0)
    def _():
        o_ref[...] = h_ref[...]

    seg = tm // n_seg

    @pl.when(e % EXPERTS_PER_GROUP == 0)
    def _():
        lower = jnp.where(_iota((seg, seg), 0) > _iota((seg, seg), 1), 1.0, 0.0).astype(BF16)
        worst = jnp.float32(0.0)
        for k in range(n_seg):
            rows, slots = slice(k * seg, (k + 1) * seg), slice(k * cap, (k + 1) * cap)
            comb = comb_ref[rows, :]
            member = comb[:, GROUP_LANE:GROUP_LANE + 1] == g.astype(F32)
            mb = jnp.broadcast_to(jnp.where(member, 1.0, 0.0), (seg, LANES))
            worst = jnp.maximum(worst, jnp.sum(mb[:, 0:1]))
            rank_b = _dot(lower, mb.astype(BF16))
            rank_row = rank_b.T[0:1, :]
            m_row = mb.T[0:1, :]
            sel = jnp.where((m_row > 0.5) & (rank_row == _iota((cap, seg), 0).astype(F32)), 1.0, 0.0).astype(BF16)
            st_ref[rows, :] = jnp.where(member & (rank_b[:, 0:1] == _iota((seg, cap), 1).astype(F32)),
                                        1.0, 0.0).astype(BF16)
            xc_ref[slots, :] = _dot(sel, xn_ref[rows, :]).astype(BF16)
            cc_ref[slots, :] = _dot3_exact_lhs(sel, comb)
        cnt_ref[0] = worst.astype(I32)
        oc_ref[...] = jnp.zeros_like(oc_ref)

    fits = cnt_ref[0] <= cap

    @pl.when(fits)
    def _():
        oc_ref[...] += _expert_ffn(xc_ref[...], wg_ref, wu_ref, wd_ref, _lane_col(cc_ref[...], e))

    @pl.when(jnp.logical_not(fits))
    def _():
        o_ref[...] += _expert_ffn(xn_ref[...], wg_ref, wu_ref, wd_ref, _lane_col(comb_ref[...], e))

    @pl.when(fits & (e % EXPERTS_PER_GROUP == EXPERTS_PER_GROUP - 1))
    def _():
        for k in range(n_seg):
            rows = slice(k * seg, (k + 1) * seg)
            oc = oc_ref[k * cap:(k + 1) * cap, :]
            hi = oc.astype(BF16)
            lo = (oc - hi.astype(F32)).astype(BF16)
            o_ref[rows, :] += _dot(st_ref[rows, :], hi) + _dot(st_ref[rows, :], lo)

    @pl.when(e == N_EXPERTS - 1)
    def _():
        out = o_ref[...]
        r = lax.rsqrt(jnp.mean(out * out, axis=-1, keepdims=True) + RMS_EPS)
        o_ref[...] = (out * r) * gf_ref[...]


def _moe(xn, comb, h2d, w_g, w_u, w_d, gain_final, tm):
    t = xn.shape[0]
    n_seg = max(1, tm // MOE_SEGMENT)
    cap = _moe_capacity(tm // n_seg)
    return pl.pallas_call(
        functools.partial(_moe_kernel, cap=cap, n_seg=n_seg),
        grid=(t // tm, N_EXPERTS),
        in_specs=[
            pl.BlockSpec((tm, D_MODEL), lambda i, e: (i, 0)),
            pl.BlockSpec((tm, LANES), lambda i, e: (i, 0)),
            pl.BlockSpec((tm, D_MODEL), lambda i, e: (i, 0), pipeline_mode=pl.Buffered(1)),
            pl.BlockSpec((1, D_MODEL, EXPERT_FF), lambda i, e: (e, 0, 0)),
            pl.BlockSpec((1, D_MODEL, EXPERT_FF), lambda i, e: (e, 0, 0)),
            pl.BlockSpec((1, EXPERT_FF, D_MODEL), lambda i, e: (e, 0, 0)),
            pl.BlockSpec((1, D_MODEL), lambda i, e: (0, 0)),
        ],
        out_specs=pl.BlockSpec((tm, D_MODEL), lambda i, e: (i, 0), pipeline_mode=pl.Buffered(1)),
        out_shape=jax.ShapeDtypeStruct((t, D_MODEL), F32),
        scratch_shapes=[pltpu.VMEM((n_seg * cap, D_MODEL), BF16),
                        pltpu.VMEM((n_seg * cap, LANES), F32), pltpu.VMEM((n_seg * cap, D_MODEL), F32),
                        pltpu.VMEM((tm, cap), BF16), pltpu.SMEM((1,), I32)],
        compiler_params=pltpu.CompilerParams(
            dimension_semantics=("parallel", "arbitrary"), vmem_limit_bytes=VMEM_LIMIT),
        name="moe",
    )(xn, comb, h2d, w_g, w_u, w_d, gain_final)


def _page_specs(block, n_chunk_pages, row_of):
    def make(k):
        def index_map(s, c, pt):
            return (pt[s, c * n_chunk_pages + k],) + row_of
        return pl.BlockSpec(block, index_map)
    return [make(k) for k in range(n_chunk_pages)]


def _idx_sample_kernel(pt_ref, q_ref, w_ref, knew_ref, *rest, past):
    pages, (o_ref, onew_ref) = rest[:IDX_PAGES], rest[IDX_PAGES:]
    c = pl.program_id(1)
    q = q_ref[...].astype(BF16)
    w = w_ref[...]

    def scores(kt):
        rel = jnp.maximum(_dot(q, kt.astype(BF16)), 0.0) * w
        return jnp.sum(rel.reshape(IDX_HEADS, 8, kt.shape[1]), axis=0) * IDX_SCALE

    kt = jnp.concatenate([jnp.concatenate([p[0], p[0]], axis=0) for p in pages], axis=1)
    o_ref[...] = scores(kt)

    @pl.when(c == 0)
    def _():
        t = _iota((8, LANES), 0) & 3
        j = _iota((8, LANES), 1)
        onew_ref[...] = jnp.where(j <= t, scores(knew_ref[...]), NEG)


def _idx_sample(pt, q_rows, w_col, knew_t, kidx_t):
    ns, n_pages = pt.shape
    n_chunks = n_pages // IDX_PAGES
    past = n_pages * PAGE
    width = IDX_PAGES * PAGE
    return pl.pallas_call(
        functools.partial(_idx_sample_kernel, past=past),
        grid_spec=pltpu.PrefetchScalarGridSpec(
            num_scalar_prefetch=1, grid=(ns, n_chunks),
            in_specs=[
                pl.BlockSpec((None, LANES, LANES), lambda s, c, pt: (s, 0, 0)),
                pl.BlockSpec((None, LANES, 1), lambda s, c, pt: (s, 0, 0)),
                pl.BlockSpec((None, LANES, LANES), lambda s, c, pt: (s, 0, 0)),
            ] + _page_specs((1, IDX_DIM, PAGE), IDX_PAGES, (0, 0)),
            out_specs=[pl.BlockSpec((None, 8, width), lambda s, c, pt: (s, 0, c)),
                       pl.BlockSpec((None, 8, LANES), lambda s, c, pt: (s, 0, 0))]),
        out_shape=[jax.ShapeDtypeStruct((ns, 8, past), F32), jax.ShapeDtypeStruct((ns, 8, LANES), F32)],
        compiler_params=pltpu.CompilerParams(
            dimension_semantics=("parallel", "arbitrary"), vmem_limit_bytes=VMEM_LIMIT),
        name="idx_sample",
    )(pt, q_rows, w_col, knew_t, *([kidx_t] * IDX_PAGES))


def _topk_sample_kernel(s_ref, snew_ref, m_ref, mnew_ref, *, past, k):
    n = s_ref.shape[0] * 8
    s = jnp.concatenate([s_ref[...].reshape(n, past), snew_ref[...].reshape(n, LANES)], axis=1)
    sel = _topk_mask(s, k, int(np.ceil(np.log2(past + LANES))))
    sel = jnp.where(sel, 1.0, 0.0)
    m_ref[...] = sel[:, :past].reshape(m_ref.shape)
    mnew_ref[...] = sel[:, past:].reshape(mnew_ref.shape)


def _topk_sample(scores, scores_new, k):
    ns, _, past = scores.shape
    sps = TOPK_SEQS_PER_STEP if ns % TOPK_SEQS_PER_STEP == 0 else 1
    return pl.pallas_call(
        functools.partial(_topk_sample_kernel, past=past, k=k),
        grid=(ns // sps,),
        in_specs=[pl.BlockSpec((sps, 8, past), lambda s: (s, 0, 0)),
                  pl.BlockSpec((sps, 8, LANES), lambda s: (s, 0, 0))],
        out_specs=[pl.BlockSpec((sps, 8, past), lambda s: (s, 0, 0)),
                   pl.BlockSpec((sps, 8, LANES), lambda s: (s, 0, 0))],
        out_shape=[jax.ShapeDtypeStruct((ns, 8, past), F32), jax.ShapeDtypeStruct((ns, 8, LANES), F32)],
        compiler_params=pltpu.CompilerParams(dimension_semantics=("parallel",), vmem_limit_bytes=VMEM_LIMIT),
        name="topk_sample",
    )(scores, scores_new)


def _rows16_from_rows8(g):
    return jnp.where(_iota((16, 8), 1) == (g * 4 + (_iota((16, 8), 0) >> 2)), 1.0, 0.0).astype(BF16)


def _paged_attn_kernel(pt_ref, q_ref, knew_ref, vnew_ref, m_ref, mnew_ref, *rest, block_mask):
    pages, (o_ref,), (m_sc, l_sc, acc_sc) = rest[:ATTN_PAGES], rest[ATTN_PAGES:ATTN_PAGES + 1], \
        rest[ATTN_PAGES + 1:]
    c = pl.program_id(1)
    width = ATTN_PAGES * PAGE

    @pl.when(c == 0)
    def _():
        m_sc[...] = jnp.full_like(m_sc, NEG)
        l_sc[...] = jnp.zeros_like(l_sc)
        acc_sc[...] = jnp.zeros_like(acc_sc)

    if block_mask:
        nb = m_ref.shape[1]
        first = c * (width // SLC_BLOCK)
        expand = jnp.where(_iota((nb, width), 0) == first + (_iota((nb, width), 1) >> 6), 1.0, 0.0).astype(BF16)
        mask8 = _dot(m_ref[...].astype(BF16), expand)
    else:
        mask8 = m_ref[...]
    mask8 = mask8.astype(BF16)

    def update(g, kg, vg, mask16):
        rs = slice(g * 16, (g + 1) * 16)
        s = _dot_nt(q_ref[g].astype(BF16), kg) * ATT_SCALE
        s = jnp.where(mask16, s, NEG)
        m_old = m_sc[rs]
        m_new = jnp.maximum(m_old, jnp.max(s, axis=-1, keepdims=True))
        p = jnp.where(mask16, jnp.exp(s - m_new), 0.0)
        alpha = jnp.exp(m_old - m_new)
        l_sc[rs] = alpha * l_sc[rs] + jnp.sum(p, axis=-1, keepdims=True)
        acc_sc[rs] = alpha * acc_sc[rs] + _dot(p.astype(BF16), vg)
        m_sc[rs] = m_new

    for g in range(2):
        kg = jnp.concatenate([p[0, pl.ds(2 * g, PAGE, stride=4), :] for p in pages], axis=0).astype(BF16)
        vg = jnp.concatenate([p[0, pl.ds(2 * g + 1, PAGE, stride=4), :] for p in pages], axis=0).astype(BF16)
        update(g, kg, vg, _dot(_rows16_from_rows8(g), mask8) > 0.5)

    @pl.when(c == pl.num_programs(1) - 1)
    def _():
        mnew8 = mnew_ref[...].astype(BF16)
        for g in range(2):
            update(g, knew_ref[g].astype(BF16), vnew_ref[g].astype(BF16), _dot(_rows16_from_rows8(g), mnew8) > 0.5)
        o_ref[...] = acc_sc[...] / jnp.maximum(l_sc[...], 1e-30)


def _paged_attn(pt, q_g, knew, vnew, mask, mask_new, pool, block_mask, name):
    ns, n_pages = pt.shape
    n_chunks = n_pages // ATTN_PAGES
    width = ATTN_PAGES * PAGE
    if block_mask:
        mspec = pl.BlockSpec((None, 8, mask.shape[2]), lambda s, c, pt: (s, 0, 0))
    else:
        mspec = pl.BlockSpec((None, 8, width), lambda s, c, pt: (s, 0, c))
    return pl.pallas_call(
        functools.partial(_paged_attn_kernel, block_mask=block_mask),
        grid_spec=pltpu.PrefetchScalarGridSpec(
            num_scalar_prefetch=1, grid=(ns, n_chunks),
            in_specs=[
                pl.BlockSpec((None, 2, 16, HEAD_DIM), lambda s, c, pt: (s, 0, 0, 0)),
                pl.BlockSpec((None, 2, LANES, HEAD_DIM), lambda s, c, pt: (s, 0, 0, 0)),
                pl.BlockSpec((None, 2, LANES, HEAD_DIM), lambda s, c, pt: (s, 0, 0, 0)),
                mspec,
                pl.BlockSpec((None, 8, LANES), lambda s, c, pt: (s, 0, 0)),
            ] + _page_specs((1, 4 * PAGE, HEAD_DIM), ATTN_PAGES, (0, 0)),
            out_specs=pl.BlockSpec((None, 32, HEAD_DIM), lambda s, c, pt: (s, 0, 0)),
            scratch_shapes=[pltpu.VMEM((32, 1), F32), pltpu.VMEM((32, 1), F32), pltpu.VMEM((32, HEAD_DIM), F32)]),
        out_shape=jax.ShapeDtypeStruct((ns, 32, HEAD_DIM), F32),
        compiler_params=pltpu.CompilerParams(
            dimension_semantics=("parallel", "arbitrary"), vmem_limit_bytes=VMEM_LIMIT),
        name=name,
    )(pt, q_g, knew, vnew, mask, mask_new, *([pool] * ATTN_PAGES))


def _compress_sample_kernel(pt_ref, w_ref, *rest):
    pages, (o_ref,) = rest[:CMP_PAGES], rest[CMP_PAGES:]
    sub = PAGE // CMP_STRIDE
    r = _iota((PAGE, PAGE), 0)
    perm = jnp.where(_iota((PAGE, PAGE), 1) == (r & (sub - 1)) * CMP_STRIDE + (r >> 3), 1.0, 0.0).astype(BF16)
    taps = []
    for p in pages:
        a = jnp.concatenate([p[0, pl.ds(gc, PAGE, stride=4), :] for gc in range(4)], axis=1)
        taps.append(_dot(perm, a.astype(BF16)))
    for c in range(2):
        xs = []
        for g in range(B_KV_HEADS):
            col = (2 * g + c) * LANES
            xs.append(jnp.concatenate(
                [jnp.concatenate([t[l * sub:(l + 1) * sub, col:col + LANES] for l in range(CMP_STRIDE)], axis=1)
                 for t in taps], axis=0))
        pq = _dot(jnp.concatenate(xs, axis=0).astype(BF16), w_ref[c])
        n = CMP_PAGES * sub
        for g in range(B_KV_HEADS):
            o_ref[:, (2 * g + c) * 2 * LANES:(2 * g + c + 1) * 2 * LANES] = pq[g * n:(g + 1) * n]


def _compress_sample(pt, w_cat, pool):
    ns, n_pages = pt.shape
    n_chunks = n_pages // CMP_PAGES
    rows = CMP_PAGES * (PAGE // CMP_STRIDE)
    return pl.pallas_call(
        _compress_sample_kernel,
        grid_spec=pltpu.PrefetchScalarGridSpec(
            num_scalar_prefetch=1, grid=(ns, n_chunks),
            in_specs=[pl.BlockSpec((2, CMP_STRIDE * HEAD_DIM, 2 * LANES), lambda s, c, pt: (0, 0, 0))]
            + _page_specs((1, 4 * PAGE, HEAD_DIM), CMP_PAGES, (0, 0)),
            out_specs=pl.BlockSpec((None, rows, 8 * LANES), lambda s, c, pt: (s, c, 0))),
        out_shape=jax.ShapeDtypeStruct((ns, n_chunks * rows, 8 * LANES), F32),
        compiler_params=pltpu.CompilerParams(
            dimension_semantics=("parallel", "arbitrary"), vmem_limit_bytes=VMEM_LIMIT),
        name="compress_sample",
    )(pt, w_cat, *([pool] * CMP_PAGES))


def _nsa_select_sample_kernel(pq_ref, w_ref, pe_ref, tab_ref, q_ref, oc_ref, bm_ref, *, past, n_blk_pad):
    nsub = past // CMP_STRIDE
    total = past + 4
    n_cmp = (total - CMP_BLOCK) // CMP_STRIDE + 1
    n_slc = -(-total // SLC_BLOCK)
    row_t = _iota((16, 1), 0) >> 2
    c_end = _iota((1, nsub), 1) * CMP_STRIDE + (CMP_BLOCK - 1)
    cmask = (c_end <= past + row_t) & (_iota((1, nsub), 1) < n_cmp)
    p_rows = []
    for g in range(B_KV_HEADS):
        comp = []
        for c in range(2):
            base = (2 * g + c) * 2 * LANES
            v = pq_ref[:, base:base + LANES] + pltpu.roll(pq_ref[:, base + LANES:base + 2 * LANES], nsub - 1, 0)
            v = v + _pe_bias(pe_ref, w_ref, c)
            if c == 0:
                v = _rope(v, tab_ref[:, 0:128], tab_ref[:, 128:256], tab_ref[:, 256:384], ROPE_DIM // 2)
            comp.append(v.astype(BF16))
        s = _dot_nt(q_ref[g].astype(BF16), comp[0]) * ATT_SCALE
        p, l = _masked_softmax_parts(s, cmask)
        p = p / jnp.maximum(l, 1e-30)
        oc_ref[g * 16:(g + 1) * 16, :] = _dot(p.astype(BF16), comp[1])
        p_rows.append(p)
    p_all = jnp.concatenate(p_rows, axis=0)
    gather = jnp.where(_iota((8, 32), 0) == (_iota((8, 32), 1) >> 2), 1.0, 0.0).astype(BF16)
    p_sum = _dot3_exact_lhs(gather, p_all)
    p_s = _dot3_exact_rhs(p_sum, _cover(nsub, n_blk_pad, n_cmp, n_slc))
    qpos = past + (_iota((8, 1), 0) & 3)
    sel = _topk_mask(_block_scores(p_s, qpos, n_slc), min(SLC_TOPN, n_slc), int(np.ceil(np.log2(n_blk_pad))))
    bm_ref[...] = jnp.where(sel, 1.0, 0.0)


def _nsa_select_sample(pq, w_cat, pe_r, tab_c, q_g, past):
    ns, nsub, _ = pq.shape
    n_slc = -(-(past + 4) // SLC_BLOCK)
    n_blk_pad = -(-n_slc // LANES) * LANES
    return pl.pallas_call(
        functools.partial(_nsa_select_sample_kernel, past=past, n_blk_pad=n_blk_pad),
        grid=(ns,),
        in_specs=[
            pl.BlockSpec((None, nsub, 8 * LANES), lambda s: (s, 0, 0)),
            pl.BlockSpec((2, CMP_STRIDE * HEAD_DIM, 2 * LANES), lambda s: (0, 0, 0)),
            pl.BlockSpec((2, 8, CMP_STRIDE * HEAD_DIM), lambda s: (0, 0, 0)),
            pl.BlockSpec((nsub, 3 * LANES), lambda s: (0, 0)),
            pl.BlockSpec((None, 2, 16, HEAD_DIM), lambda s: (s, 0, 0, 0)),
        ],
        out_specs=[pl.BlockSpec((None, 32, HEAD_DIM), lambda s: (s, 0, 0)),
                   pl.BlockSpec((None, 8, n_blk_pad), lambda s: (s, 0, 0))],
        out_shape=[jax.ShapeDtypeStruct((ns, 32, HEAD_DIM), F32), jax.ShapeDtypeStruct((ns, 8, n_blk_pad), F32)],
        compiler_params=pltpu.CompilerParams(dimension_semantics=("parallel",), vmem_limit_bytes=VMEM_LIMIT),
        name="nsa_select_sample",
    )(pq, w_cat, pe_r, tab_c, q_g)


def _window_sample_kernel(win_ref, knew_ref, vnew_ref, q_ref, oc_ref, os_ref, gate_ref, o_ref, *, wb):
    row_t = _iota((16, 1), 0) >> 2
    kpos = _iota((1, wb), 1)
    mask_old = kpos > row_t + (wb - WINDOW)
    mask_new = _iota((1, LANES), 1) <= row_t
    for g in range(B_KV_HEADS):
        rs = slice(g * 16, (g + 1) * 16)
        q = q_ref[g].astype(BF16)
        kw = win_ref[pl.ds(2 * g, wb, stride=4), :].astype(BF16)
        vw = win_ref[pl.ds(2 * g + 1, wb, stride=4), :].astype(BF16)
        s = jnp.concatenate([_dot_nt(q, kw), _dot_nt(q, knew_ref[g].astype(BF16))], axis=1) * ATT_SCALE
        mask = jnp.concatenate([jnp.where(mask_old, 1.0, 0.0), jnp.where(mask_new, 1.0, 0.0)], axis=1) > 0.5
        p, l = _masked_softmax_parts(s, mask)
        o_w = (_dot(p[:, :wb].astype(BF16), vw) + _dot(p[:, wb:].astype(BF16), vnew_ref[g].astype(BF16)))
        o_w = o_w / jnp.maximum(l, 1e-30)
        gt = gate_ref[rs]
        o_ref[rs] = gt[:, 0:1] * oc_ref[rs] + gt[:, 1:2] * os_ref[rs] + gt[:, 2:3] * o_w


def _window_sample(win_rows, knew, vnew, q_g, o_c, o_s, gates, wb):
    ns = q_g.shape[0]
    blk = lambda *shape: pl.BlockSpec((None,) + shape, lambda s: (s,) + (0,) * len(shape))
    return pl.pallas_call(
        functools.partial(_window_sample_kernel, wb=wb),
        grid=(ns,),
        in_specs=[blk(4 * wb, HEAD_DIM), blk(2, LANES, HEAD_DIM), blk(2, LANES, HEAD_DIM), blk(2, 16, HEAD_DIM),
                  blk(32, HEAD_DIM), blk(32, HEAD_DIM), blk(32, LANES)],
        out_specs=blk(32, HEAD_DIM),
        out_shape=jax.ShapeDtypeStruct((ns, 32, HEAD_DIM), F32),
        compiler_params=pltpu.CompilerParams(dimension_semantics=("parallel",), vmem_limit_bytes=VMEM_LIMIT),
        name="window_sample",
    )(win_rows, knew, vnew, q_g, o_c, o_s, gates)


def _rope_table(pos, half, period):
    inv = ROPE_THETA ** (-jnp.arange(half, dtype=F32) / half)
    ang = pos.astype(F32)[:, None] * inv[None, :]
    cos, sin = jnp.cos(ang), jnp.sin(ang)
    lane = np.arange(LANES) % period
    idx = lane % half
    lo = jnp.asarray(lane < half)[None, :]
    hi = jnp.asarray((lane >= half) & (lane < 2 * half))[None, :]
    c = jnp.where(lo | hi, cos[:, idx], 1.0)
    a = jnp.where(lo, -sin[:, idx], 0.0)
    b = jnp.where(hi, sin[:, idx], 0.0)
    return jnp.concatenate([c, a, b], axis=1)


def _proj_tables(pos):
    return jnp.stack([_rope_table(pos, ROPE_DIM // 2, LANES), _rope_table(pos, IDX_ROPE_DIM // 2, IDX_DIM)])


def _chunk_kinds():
    kinds = np.zeros((N_CHUNKS,), np.int32)
    kinds[C_GM:C_GM + 32] = K_SIGMOID
    kinds[C_QA:C_QA + 8] = K_ROPE_HEAD
    kinds[C_QI:C_QI + 8] = K_ROPE_IDX
    kinds[C_QB:C_QB + 8] = K_ROPE_HEAD
    for base in (C_KVA, C_KVS, C_KVW):
        kinds[base] = K_ROPE_HEAD
        kinds[base + 2] = K_ROPE_HEAD
    kinds[C_KIW] = K_ROPE_IDX_LOW
    kinds[C_GN] = K_SIGMOID
    return jnp.asarray(kinds)


def _pack_w_in(w):
    sizes = [("q_a", 1024), ("kv_a", 512), ("q_i", 1024), ("k_i", 64), ("w_i", 16), ("q_b", 1024), ("kv_c", 512),
             ("kv_s", 512), ("kv_w", 512), ("g_n", 24), ("g_m", 4096)]
    parts, off = {}, 0
    for name, n in sizes:
        parts[name] = w[:, off:off + n]
        off += n
    z = lambda n: jnp.zeros((w.shape[0], n), w.dtype)
    cols = [parts["g_m"], parts["q_a"], parts["q_i"], parts["q_b"], parts["kv_a"], parts["kv_c"], parts["kv_s"],
            parts["kv_w"], parts["k_i"], parts["w_i"], z(48), parts["g_n"], z(104), z(2 * LANES)]
    return jnp.concatenate(cols, axis=1).astype(BF16)


def _cols(y, chunk, n):
    return y[:, chunk * LANES:chunk * LANES + n]


def _group_rows(q):
    ns = q.shape[0]
    return q.reshape(ns, 4, 2, GQA, HEAD_DIM).transpose(0, 2, 1, 3, 4).reshape(ns, 2, 16, HEAD_DIM)


def _new_kv(kv):
    ns = kv.shape[0]
    kv = kv.reshape(ns, 4, 2, 2, HEAD_DIM).transpose(0, 2, 3, 1, 4)
    kv = jnp.pad(kv, ((0, 0), (0, 0), (0, 0), (0, LANES - 4), (0, 0)))
    return kv[:, :, 0], kv[:, :, 1]


def _ungroup_rows(o):
    ns = o.shape[0]
    return o.reshape(ns, 2, 4, GQA, HEAD_DIM).transpose(0, 2, 1, 3, 4).reshape(ns * 4, A_HEADS * HEAD_DIM)


def kernel(x_prompt, x_sample, cache_kv_a, cache_kidx, cache_kv_cmp, cache_kv_slc, state_kv_win, page_table,
           norm_mix, w_in, w_phi, pe_phi, w_o_a, w_o_b, w_out, norm_ffn, w_router_group, b_router_group,
           w_router_expert, b_router_expert, w_gate, w_up, w_down, norm_final):
    batch, seq, _ = x_prompt.shape
    ns, dec, _ = x_sample.shape
    assert dec == 4 and norm_mix.shape[0] == 1, "kernel is written for DEC_SEQ=4 and DEPTH=1"
    n_pool = cache_kv_a.shape[1]
    n_pages = page_table.shape[1]
    past = n_pages * PAGE
    wb = state_kv_win.shape[2]

    w_packed = _pack_w_in(w_in[0])
    kinds = _chunk_kinds()
    wp = w_phi[0].astype(BF16)
    w_cat = jnp.concatenate([wp[:, :CMP_STRIDE].reshape(2, CMP_STRIDE * HEAD_DIM, HEAD_DIM),
                             wp[:, CMP_STRIDE:].reshape(2, CMP_STRIDE * HEAD_DIM, HEAD_DIM)], axis=2)
    pe_r = pe_phi[0].transpose(1, 0, 2).reshape(2, 2, CMP_STRIDE * HEAD_DIM)
    pe_r = jnp.pad(pe_r, ((0, 0), (0, 6), (0, 0)))
    woa, wob, wout = w_o_a[0].astype(BF16), w_o_b[0].astype(BF16), w_out[0].astype(BF16)
    w_r = jnp.concatenate([w_router_expert[0], w_router_group[0],
                           jnp.zeros((D_MODEL, LANES - N_EXPERTS - N_GROUPS), F32)], axis=1)
    w_r_hi = w_r.astype(BF16)
    w_r_lo = (w_r - w_r_hi.astype(F32)).astype(BF16)
    b_r = jnp.concatenate([b_router_expert[0], b_router_group[0],
                           jnp.zeros((LANES - N_EXPERTS - N_GROUPS,), F32)])[None, :]
    w_g, w_u, w_d = w_gate[0].astype(BF16), w_up[0].astype(BF16), w_down[0].astype(BF16)
    g_mix, g_ffn, g_fin = norm_mix[0][None, :], norm_ffn[0][None, :], norm_final[None, :]

    xp = x_prompt.reshape(batch * seq, D_MODEL)
    tm_p = 1024
    y_p, kva_p, kvc_p, kvs_p, kvw_p = _project(xp, g_mix, w_packed, kinds, _proj_tables(jnp.arange(seq)), tm_p,
                                               seq // tm_p)
    o_a_p = _dsa_prompt(y_p, batch, seq)
    nsub_p = seq // CMP_STRIDE
    tab_cp = _rope_table(jnp.arange(nsub_p) * CMP_STRIDE + CMP_BLOCK - 1, ROPE_DIM // 2, LANES)
    cmp_p = _compress_prompt(y_p, w_cat, pe_r, tab_cp, batch, seq)
    o_b_p = _nsa_prompt(y_p, cmp_p, batch, seq)
    h_p = _merge(xp, y_p, o_a_p, o_b_p, woa, wob, wout, 256)
    xn_p, comb_p = _router(h_p, g_ffn, w_r_hi, w_r_lo, b_r, 512)
    y_prompt = _moe(xn_p, comb_p, h_p, w_g, w_u, w_d, g_fin, 1024).reshape(batch, seq, D_MODEL)

    xs = x_sample.reshape(ns * 4, D_MODEL)
    tm_s = ns * 4
    pos_s = past + (jnp.arange(ns * 4) % 4)
    y_s, kva_s, kvc_s, kvs_s, kvw_s = _project(xs, g_mix, w_packed, kinds, _proj_tables(pos_s), tm_s, 1)
    pt = page_table.astype(I32)
    qi = _cols(y_s, C_QI, 1024).reshape(ns, 4, IDX_HEADS, IDX_DIM).transpose(0, 2, 1, 3)
    qi = jnp.concatenate([qi, qi], axis=2)
    par = (jnp.arange(IDX_HEADS) % 2)[None, :, None, None, None]
    q_rows = jnp.where(par == jnp.arange(2)[None, None, None, :, None], qi[:, :, :, None, :], 0.0)
    q_rows = q_rows.reshape(ns, IDX_HEADS * 8, 2 * IDX_DIM)
    kiw = _cols(y_s, C_KIW, LANES).reshape(ns, 4, LANES)
    wi = kiw[:, :, IDX_DIM:IDX_DIM + IDX_HEADS].transpose(0, 2, 1)
    w_col = jnp.concatenate([wi, wi], axis=2).reshape(ns, IDX_HEADS * 8, 1)
    knew_t = jnp.pad(kiw[:, :, :IDX_DIM].transpose(0, 2, 1), ((0, 0), (0, 0), (0, LANES - 4)))
    knew_t = jnp.concatenate([knew_t, knew_t], axis=1)
    kidx_t = jnp.swapaxes(cache_kidx[0], 1, 2)
    sc, sc_new = _idx_sample(pt, q_rows, w_col, knew_t, kidx_t)
    m_a, m_a_new = _topk_sample(sc, sc_new, min(DSA_TOPK, (past + 4) // 4))
    q_a = _group_rows(_cols(y_s, C_QA, 1024).reshape(ns, 4, A_HEADS, HEAD_DIM))
    ka_new, va_new = _new_kv(_cols(y_s, C_KVA, 512).reshape(ns, 4, 512))
    o_a_s = _paged_attn(pt, q_a, ka_new, va_new, m_a, m_a_new, cache_kv_a.reshape(n_pool, 4 * PAGE, HEAD_DIM),
                        False, "dsa_sample")
    q_b = _group_rows(_cols(y_s, C_QB, 1024).reshape(ns, 4, B_HEADS, HEAD_DIM))
    pq = _compress_sample(pt, w_cat, cache_kv_cmp.reshape(n_pool, 4 * PAGE, HEAD_DIM))
    nsub_s = past // CMP_STRIDE
    tab_cs = _rope_table(jnp.arange(nsub_s) * CMP_STRIDE + CMP_BLOCK - 1, ROPE_DIM // 2, LANES)
    o_c, blk_mask = _nsa_select_sample(pq, w_cat, pe_r, tab_cs, q_b, past)
    ks_new, vs_new = _new_kv(_cols(y_s, C_KVS, 512).reshape(ns, 4, 512))
    first_new = past // SLC_BLOCK
    t_row = (jnp.arange(8) & 3)[None, :, None]
    m_s_new = jnp.where(jnp.arange(LANES)[None, None, :] <= t_row, blk_mask[:, :, first_new:first_new + 1], 0.0)
    o_s = _paged_attn(pt, q_b, ks_new, vs_new, blk_mask, m_s_new,
                      cache_kv_slc.reshape(n_pool, 4 * PAGE, HEAD_DIM), True, "slc_sample")
    kw_new, vw_new = _new_kv(_cols(y_s, C_KVW, 512).reshape(ns, 4, 512))
    gates = _group_rows(jnp.pad(_cols(y_s, C_GN, 24).reshape(ns, 4, B_HEADS, 3), ((0, 0),) * 3 + ((0, LANES - 3),)))
    gates = gates.reshape(ns, 32, LANES)
    o_b_s = _window_sample(state_kv_win[0].reshape(ns, 4 * wb, HEAD_DIM), kw_new, vw_new, q_b, o_c, o_s, gates, wb)
    h_s = _merge(xs, y_s, _ungroup_rows(o_a_s), _ungroup_rows(o_b_s), woa, wob, wout, tm_s)
    xn_s, comb_s = _router(h_s, g_ffn, w_r_hi, w_r_lo, b_r, tm_s)
    y_sample = _moe(xn_s, comb_s, h_s, w_g, w_u, w_d, g_fin, tm_s).reshape(ns, 4, D_MODEL)

    def kv_out(rows, lead):
        return rows.reshape((1,) + lead + (2, 2, HEAD_DIM))

    wb_p = min(WINDOW, seq)
    win_s = jnp.concatenate([state_kv_win[0][:, 4:], kvw_s.reshape(ns, 4, 2, 2, HEAD_DIM)], axis=1)
    return (y_prompt, y_sample,
            kv_out(kva_p, (batch, seq)), _cols(y_p, C_KIW, IDX_DIM).reshape(1, batch, seq, IDX_DIM),
            kv_out(kvc_p, (batch, seq)), kv_out(kvs_p, (batch, seq)), kv_out(kvw_p, (batch, seq))[:, :, seq - wb_p:],
            kv_out(kva_s, (ns, 4)), _cols(y_s, C_KIW, IDX_DIM).reshape(1, ns, 4, IDX_DIM),
            kv_out(kvc_s, (ns, 4)), kv_out(kvs_s, (ns, 4)), win_s[None])
```

```python
import functools

import numpy as np
import jax
import jax.numpy as jnp
from jax import lax
from jax.experimental import pallas as pl
from jax.experimental.pallas import tpu as pltpu

F32 = jnp.float32
BF16 = jnp.bfloat16
I32 = jnp.int32

LANES = 128
VMEM_LIMIT = 56 * 1024 * 1024

D_MODEL = 2048
HEAD_DIM = 128
ROPE_DIM = HEAD_DIM // 4
ROPE_THETA = 500000.0
A_HEADS = 8
A_KV_HEADS = 2
IDX_HEADS = 16
IDX_DIM = 64
IDX_ROPE_DIM = IDX_DIM // 4
DSA_TOPK = 256
B_HEADS = 8
B_KV_HEADS = 2
GQA = 4
CMP_BLOCK = 32
CMP_STRIDE = 16
SLC_BLOCK = 64
SLC_TOPN = 16
SLC_LOCAL = 2
WINDOW = 512
PAGE = 128
N_GROUPS = 4
EXPERTS_PER_GROUP = 4
N_EXPERTS = 16
GROUP_LANE = N_EXPERTS
EXPERT_FF = 512
Q_BLOCK = 128
RMS_EPS = 1e-6
NEG = -1e30
FORCE_BONUS = 1e4
ATT_SCALE = HEAD_DIM ** -0.5
IDX_SCALE = (IDX_DIM * IDX_HEADS) ** -0.5
IDX_PAGES = 64
ATTN_PAGES = 64
CMP_PAGES = 32
TOPK_SEQS_PER_STEP = 4
MAX_CAUSAL_EXTENTS = 4

C_GM, C_QA, C_QI, C_QB, C_KVA, C_KVC, C_KVS, C_KVW, C_KIW, C_GN = 0, 32, 40, 48, 56, 60, 64, 68, 72, 73
N_CHUNKS = 76
NP = N_CHUNKS * LANES
K_NONE, K_ROPE_HEAD, K_ROPE_IDX, K_SIGMOID, K_ROPE_IDX_LOW = 0, 1, 2, 3, 4

NT_DIMS = (((1,), (1,)), ((), ()))


def _dot(a, b):
    return jnp.dot(a, b, preferred_element_type=F32)


def _dot_nt(a, b):
    return lax.dot_general(a, b, NT_DIMS, preferred_element_type=F32)


def _split3(x):
    hi = x.astype(BF16)
    r1 = x - hi.astype(F32)
    mid = r1.astype(BF16)
    lo = (r1 - mid.astype(F32)).astype(BF16)
    return hi, mid, lo


def _dot3_exact_rhs(x, rhs_bf16):
    hi, mid, lo = _split3(x)
    return _dot(hi, rhs_bf16) + _dot(mid, rhs_bf16) + _dot(lo, rhs_bf16)


def _dot3_exact_lhs(lhs_bf16, x):
    hi, mid, lo = _split3(x)
    return _dot(lhs_bf16, hi) + _dot(lhs_bf16, mid) + _dot(lhs_bf16, lo)


def _iota(shape, dim):
    return lax.broadcasted_iota(I32, shape, dim)


def _count(mask):
    return jnp.sum(jnp.where(mask, 1.0, 0.0), axis=-1, keepdims=True)


def _topk_mask(score, k, idx_bits):
    rows = score.shape[0]
    imin = jnp.int32(-2 ** 31)
    kf = jnp.float32(k)
    n_part = 2 if rows % 16 == 0 else 1
    pr = rows // n_part
    parts = [score[h * pr:(h + 1) * pr] for h in range(n_part)]

    def float_of(u):
        key = u ^ imin
        return pltpu.bitcast(key ^ ((key >> 31) & jnp.int32(0x7FFFFFFF)), F32)

    def value_bit(i, tus):
        bit = lax.shift_left(jnp.int32(1), 31 - i)
        out = []
        for s_h, tu in zip(parts, tus):
            cand = tu | bit
            cnt = _count(s_h >= float_of(cand))
            out.append(jnp.where(cnt >= kf, cand, tu))
        return tuple(out)

    tus = lax.fori_loop(0, 32, value_bit, tuple(jnp.zeros((pr, 1), I32) for _ in range(n_part)))
    tu = jnp.concatenate(tus, axis=0) if n_part > 1 else tus[0]
    ge_hi = score >= float_of(tu + 1)
    bucket = (score >= float_of(tu)) & jnp.logical_not(ge_hi)
    want = kf - _count(ge_hi)

    def unsettled(c):
        return jnp.max(jnp.where(c[1] < want, 1.0, 0.0)) > 0.0

    def next_value(c):
        v, cnt = c
        below = jnp.max(jnp.where(bucket & (score < v), score, -jnp.inf), axis=-1, keepdims=True)
        v = jnp.where(cnt < want, below, v)
        return v, _count(bucket & (score >= v))

    thr, _ = lax.while_loop(unsettled, next_value, (jnp.full((rows, 1), jnp.inf, F32), jnp.zeros((rows, 1), F32)))
    gt = score > thr
    eq = score == thr
    need = kf - _count(gt)
    idx = _iota(score.shape, 1)

    def index_bit(i, j):
        cand = j | lax.shift_left(jnp.int32(1), idx_bits - 1 - i)
        cnt = _count(eq & (idx < cand))
        return jnp.where(cnt < need, cand, j)

    def tie_search():
        return lax.fori_loop(0, idx_bits, index_bit, jnp.zeros((rows, 1), I32))

    def take_all():
        return jnp.full((rows, 1), 2 ** idx_bits, I32)

    surplus = jnp.max(jnp.abs(_count(eq) - need))
    j = lax.cond(surplus == 0.0, take_all, tie_search)
    return gt | (eq & (idx <= j))


def _masked_softmax_parts(s, mask):
    s = jnp.where(mask, s, NEG)
    m = jnp.max(s, axis=-1, keepdims=True)
    p = jnp.where(mask, jnp.exp(s - m), 0.0)
    return p, jnp.sum(p, axis=-1, keepdims=True)


def _attend_heads(qg, k, v, mask):
    s = _dot_nt(qg, k) * ATT_SCALE
    ps, ls = [], []
    for r in range(GQA):
        p, l = _masked_softmax_parts(s[r * Q_BLOCK:(r + 1) * Q_BLOCK], mask)
        ps.append(p.astype(BF16))
        ls.append(l)
    o = _dot(jnp.concatenate(ps, axis=0), v)
    return o / jnp.maximum(jnp.concatenate(ls, axis=0), 1e-30)


def _rope(y, c, a, b, half):
    return y * c + pltpu.roll(y, LANES - half, 1) * a + pltpu.roll(y, half, 1) * b


def _proj_kernel(kind_ref, x_ref, g_ref, w_ref, tab_ref, o_ref, *rest, tn):
    kv_refs, xn_ref = rest[:4], rest[4]
    assert tn == 4 * LANES, "each K/V part must be exactly one column block"
    j = pl.program_id(1)

    @pl.when(j == 0)
    def _():
        xf = x_ref[...]
        r = lax.rsqrt(jnp.mean(xf * xf, axis=-1, keepdims=True) + RMS_EPS)
        xn_ref[...] = ((xf * r) * g_ref[...]).astype(BF16)

    lane = _iota((1, LANES), 1)
    n_gate_blocks = (C_QA - C_GM) // (tn // LANES)

    @pl.when(j < n_gate_blocks)
    def _():
        for n0 in range(0, tn, 2 * LANES):
            acc = _dot(xn_ref[...], w_ref[:, n0:n0 + 2 * LANES])
            o_ref[:, n0:n0 + 2 * LANES] = 1.0 / (1.0 + jnp.exp(-acc))

    @pl.when(j >= n_gate_blocks)
    def _():
        for n0 in range(0, tn, 2 * LANES):
            acc = _dot(xn_ref[...], w_ref[:, n0:n0 + 2 * LANES])
            for c in range(2):
                cs = slice(n0 + c * LANES, n0 + (c + 1) * LANES)
                y = acc[:, c * LANES:(c + 1) * LANES]
                k = kind_ref[j * (tn // LANES) + n0 // LANES + c]
                is_idx = (k == K_ROPE_IDX) | (k == K_ROPE_IDX_LOW)
                is_rope = (k == K_ROPE_HEAD) | is_idx
                t = jnp.where(is_idx, 1, 0)
                half = jnp.where(is_idx, IDX_ROPE_DIM // 2, ROPE_DIM // 2)
                roped = (y * tab_ref[t, :, 0:128] + pltpu.roll(y, LANES - half, 1) * tab_ref[t, :, 128:256]
                         + pltpu.roll(y, half, 1) * tab_ref[t, :, 256:384])
                rope_lanes = jnp.where(is_rope, jnp.where(k == K_ROPE_IDX_LOW, IDX_DIM, LANES), 0)
                o_ref[:, cs] = jnp.where(lane < rope_lanes, roped, y)

    tm = o_ref.shape[0]
    for chunk0, kv_ref in ((C_KVA, kv_refs[0]), (C_KVC, kv_refs[1]), (C_KVS, kv_refs[2]), (C_KVW, kv_refs[3])):
        @pl.when(j == chunk0 // (tn // LANES))
        def _(kv_ref=kv_ref):
            for c in range(4):
                kv_ref[pl.ds(c, tm, stride=4), :] = o_ref[:, c * LANES:(c + 1) * LANES]


def _project(x2d, gain, w_packed, kinds, tables, tm, n_tab_blocks):
    t = x2d.shape[0]
    tn = 512
    grid = (t // tm, NP // tn)
    return pl.pallas_call(
        functools.partial(_proj_kernel, tn=tn),
        grid_spec=pltpu.PrefetchScalarGridSpec(
            num_scalar_prefetch=1, grid=grid,
            in_specs=[
                pl.BlockSpec((tm, D_MODEL), lambda i, j, kr: (i, 0)),
                pl.BlockSpec((1, D_MODEL), lambda i, j, kr: (0, 0)),
                pl.BlockSpec((D_MODEL, tn), lambda i, j, kr: (0, j)),
                pl.BlockSpec((2, tm, 3 * LANES), lambda i, j, kr: (0, i % n_tab_blocks, 0)),
            ],
            out_specs=[pl.BlockSpec((tm, tn), lambda i, j, kr: (i, j))]
            + [pl.BlockSpec((4 * tm, LANES), lambda i, j, kr: (i, 0), pipeline_mode=pl.Buffered(1))] * 4,
            scratch_shapes=[pltpu.VMEM((tm, D_MODEL), BF16)]),
        out_shape=[jax.ShapeDtypeStruct((t, NP), F32)] + [jax.ShapeDtypeStruct((4 * t, LANES), F32)] * 4,
        compiler_params=pltpu.CompilerParams(
            dimension_semantics=("parallel", "arbitrary"), vmem_limit_bytes=VMEM_LIMIT),
        name="in_proj",
    )(kinds, x2d, gain, w_packed, tables)


def _dsa_prompt_body(qa_ref, qi_ref, kiwq_ref, kiwk_ref, kva_ref, o_ref, q0, ext, k_top):
    lane = _iota((1, LANES), 1)
    qpos = q0 + _iota((Q_BLOCK, 1), 0)
    causal = _iota((1, ext), 1) <= qpos
    if ext <= k_top:
        sel = causal
    else:
        kk = kiwk_ref[0:ext, :]
        k_dup = jnp.where(lane < IDX_DIM, kk, pltpu.roll(kk, IDX_DIM, 1)).astype(BF16)
        wi = kiwq_ref[...]
        score = jnp.zeros((Q_BLOCK, ext), F32)
        for pair in range(IDX_HEADS // 2):
            qp = qi_ref[:, pair * LANES:(pair + 1) * LANES]
            for half in range(2):
                h = 2 * pair + half
                keep = (lane < IDX_DIM) if half == 0 else (lane >= IDX_DIM)
                qm = jnp.where(keep, qp, 0.0).astype(BF16)
                rel = jnp.maximum(_dot_nt(qm, k_dup), 0.0)
                score = score + rel * wi[:, IDX_DIM + h:IDX_DIM + h + 1]
        score = jnp.where(causal, score * IDX_SCALE, NEG)
        sel = _topk_mask(score, k_top, int(np.ceil(np.log2(ext)))) & causal

    for g in range(A_KV_HEADS):
        kg = kva_ref[0:ext, (2 * g) * LANES:(2 * g + 1) * LANES].astype(BF16)
        vg = kva_ref[0:ext, (2 * g + 1) * LANES:(2 * g + 2) * LANES].astype(BF16)
        qg = jnp.concatenate(
            [qa_ref[:, (g * GQA + r) * LANES:(g * GQA + r + 1) * LANES] for r in range(GQA)], axis=0).astype(BF16)
        o = _attend_heads(qg, kg, vg, sel)
        for r in range(GQA):
            o_ref[:, (g * GQA + r) * LANES:(g * GQA + r + 1) * LANES] = o[r * Q_BLOCK:(r + 1) * Q_BLOCK]


def _causal_extents(seq):
    exts = [seq]
    while exts[0] // 2 >= 2 * Q_BLOCK and len(exts) < MAX_CAUSAL_EXTENTS:
        exts.insert(0, exts[0] // 2)
    return exts


def _for_each_extent(i, seq, body):
    lo = 0
    for ext in _causal_extents(seq):
        hi = ext // Q_BLOCK

        @pl.when((i >= lo) & (i < hi))
        def _(ext=ext):
            body(ext)
        lo = hi


def _dsa_prompt_kernel(qa_ref, qi_ref, kiwq_ref, kiwk_ref, kva_ref, o_ref, *, seq):
    i = pl.program_id(1)
    _for_each_extent(i, seq, lambda ext: _dsa_prompt_body(
        qa_ref, qi_ref, kiwq_ref, kiwk_ref, kva_ref, o_ref, i * Q_BLOCK, ext, min(DSA_TOPK, seq // 4)))


def _dsa_prompt(y, batch, seq):
    nqb = seq // Q_BLOCK
    return pl.pallas_call(
        functools.partial(_dsa_prompt_kernel, seq=seq),
        grid=(batch, nqb),
        in_specs=[
            pl.BlockSpec((Q_BLOCK, 8 * LANES), lambda b, i: (b * nqb + i, C_QA // 8)),
            pl.BlockSpec((Q_BLOCK, 8 * LANES), lambda b, i: (b * nqb + i, C_QI // 8)),
            pl.BlockSpec((Q_BLOCK, LANES), lambda b, i: (b * nqb + i, C_KIW)),
            pl.BlockSpec((seq, LANES), lambda b, i: (b, C_KIW)),
            pl.BlockSpec((seq, 4 * LANES), lambda b, i: (b, C_KVA // 4)),
        ],
        out_specs=pl.BlockSpec((Q_BLOCK, 8 * LANES), lambda b, i: (b * nqb + i, 0)),
        out_shape=jax.ShapeDtypeStruct((batch * seq, A_HEADS * HEAD_DIM), F32),
        compiler_params=pltpu.CompilerParams(
            dimension_semantics=("parallel", "arbitrary"), vmem_limit_bytes=VMEM_LIMIT),
        name="dsa_prompt",
    )(y, y, y, y, y)


def _pe_bias(pe_ref, w_ref, c):
    pb = _dot(pe_ref[c].astype(BF16), w_ref[c])
    return pb[0:1, 0:LANES] + pb[1:2, LANES:2 * LANES]


def _compress_prompt_kernel(kvc00_ref, kvc01_ref, kvc10_ref, kvc11_ref, w_ref, pe_ref, tab_ref, o_ref, *, nsub):
    kvc = ((kvc00_ref, kvc01_ref), (kvc10_ref, kvc11_ref))
    for c in range(2):
        xs = []
        for g in range(B_KV_HEADS):
            xs.append(jnp.concatenate(
                [kvc[g][c][pl.ds(l, nsub, stride=CMP_STRIDE), :] for l in range(CMP_STRIDE)], axis=1))
        pq = _dot(jnp.concatenate(xs, axis=0).astype(BF16), w_ref[c])
        bias = _pe_bias(pe_ref, w_ref, c)
        for g in range(B_KV_HEADS):
            top = pq[g * nsub:(g + 1) * nsub, 0:LANES]
            bot = pq[g * nsub:(g + 1) * nsub, LANES:2 * LANES]
            comp = top + pltpu.roll(bot, nsub - 1, 0) + bias
            if c == 0:
                comp = _rope(comp, tab_ref[:, 0:128], tab_ref[:, 128:256], tab_ref[:, 256:384], ROPE_DIM // 2)
            o_ref[:, (2 * g + c) * LANES:(2 * g + c + 1) * LANES] = comp


def _compress_prompt(y, w_cat, pe_r, tab_c, batch, seq):
    nsub = seq // CMP_STRIDE
    return pl.pallas_call(
        functools.partial(_compress_prompt_kernel, nsub=nsub),
        grid=(batch,),
        in_specs=[
            pl.BlockSpec((seq, LANES), lambda b: (b, C_KVC)),
            pl.BlockSpec((seq, LANES), lambda b: (b, C_KVC + 1)),
            pl.BlockSpec((seq, LANES), lambda b: (b, C_KVC + 2)),
            pl.BlockSpec((seq, LANES), lambda b: (b, C_KVC + 3)),
            pl.BlockSpec((2, CMP_STRIDE * HEAD_DIM, 2 * LANES), lambda b: (0, 0, 0)),
            pl.BlockSpec((2, 8, CMP_STRIDE * HEAD_DIM), lambda b: (0, 0, 0)),
            pl.BlockSpec((nsub, 3 * LANES), lambda b: (0, 0)),
        ],
        out_specs=pl.BlockSpec((nsub, 4 * LANES), lambda b: (b, 0)),
        out_shape=jax.ShapeDtypeStruct((batch * nsub, 4 * LANES), F32),
        compiler_params=pltpu.CompilerParams(dimension_semantics=("parallel",), vmem_limit_bytes=VMEM_LIMIT),
        name="compress_prompt",
    )(y, y, y, y, w_cat, pe_r, tab_c)


def _cover(n_rows, n_cols, n_cmp, n_slc):
    ci = _iota((n_rows, n_cols), 0) * CMP_STRIDE
    sj = _iota((n_rows, n_cols), 1) * SLC_BLOCK
    m = (ci < sj + SLC_BLOCK) & (ci + CMP_BLOCK > sj)
    m = m & (_iota((n_rows, n_cols), 0) < n_cmp) & (_iota((n_rows, n_cols), 1) < n_slc)
    return jnp.where(m, 1.0, 0.0).astype(BF16)


def _block_scores(p_s, qpos_col, n_slc):
    blk = _iota(p_s.shape, 1)
    cur = qpos_col >> 6
    forced = (blk == 0) | ((blk <= cur) & (blk > cur - SLC_LOCAL))
    score = jnp.where(blk <= cur, p_s + FORCE_BONUS * jnp.where(forced, 1.0, 0.0), NEG)
    return jnp.where(blk < n_slc, score, -jnp.inf)


def _select_blocks(p_sum, cover_t, qpos_row, n_slc, n_top):
    hi, mid, lo = _split3(p_sum)
    p_s = (_dot_nt(cover_t, hi) + _dot_nt(cover_t, mid) + _dot_nt(cover_t, lo))[0:n_slc]
    blk = _iota((n_slc, Q_BLOCK), 0)
    cur = qpos_row >> 6
    forced = (blk == 0) | ((blk <= cur) & (blk > cur - SLC_LOCAL))
    score = jnp.where(blk <= cur, p_s + FORCE_BONUS * jnp.where(forced, 1.0, 0.0), NEG)
    rank = jnp.zeros((n_slc, Q_BLOCK), F32)
    for i in range(n_slc):
        si = score[i:i + 1, :]
        rank = rank + jnp.where((si > score) | ((si == score) & (blk > i)), 1.0, 0.0)
    sel_t = jnp.where(rank < n_top, 1.0, 0.0)
    sel_t = jnp.concatenate([sel_t, jnp.zeros((LANES - n_slc, Q_BLOCK), F32)], axis=0)
    return sel_t.T


def _nsa_prompt_body(qb_ref, gn_ref, cmp_ref, kvs_ref, kvw_ref, o_ref, i, ext, seq):
    q0 = i * Q_BLOCK
    nsub = seq // CMP_STRIDE
    n_cmp = (seq - CMP_BLOCK) // CMP_STRIDE + 1
    n_slc = seq // SLC_BLOCK
    qpos = q0 + _iota((Q_BLOCK, 1), 0)
    qpos_row = q0 + _iota((1, Q_BLOCK), 1)
    c_end = _iota((1, nsub), 1) * CMP_STRIDE + (CMP_BLOCK - 1)
    cmask = c_end <= qpos
    cj = _iota((LANES, nsub), 0) * SLC_BLOCK
    cn = _iota((LANES, nsub), 1) * CMP_STRIDE
    cover_t = jnp.where((cn < cj + SLC_BLOCK) & (cn + CMP_BLOCK > cj) & (_iota((LANES, nsub), 1) < n_cmp)
                        & (_iota((LANES, nsub), 0) < n_slc), 1.0, 0.0).astype(BF16)
    kpos = _iota((1, ext), 1)
    expand = jnp.where((_iota((LANES, ext), 1) >> 6) == _iota((LANES, ext), 0), 1.0, 0.0).astype(BF16)
    gates = 1.0 / (1.0 + jnp.exp(-gn_ref[...]))

    for g in range(B_KV_HEADS):
        qg = jnp.concatenate(
            [qb_ref[:, (g * GQA + r) * LANES:(g * GQA + r + 1) * LANES] for r in range(GQA)], axis=0).astype(BF16)
        kc = cmp_ref[:, (2 * g) * LANES:(2 * g + 1) * LANES].astype(BF16)
        vc = cmp_ref[:, (2 * g + 1) * LANES:(2 * g + 2) * LANES].astype(BF16)
        s = _dot_nt(qg, kc) * ATT_SCALE
        ps = []
        for r in range(GQA):
            p, l = _masked_softmax_parts(s[r * Q_BLOCK:(r + 1) * Q_BLOCK], cmask)
            ps.append(p / jnp.maximum(l, 1e-30))
        o_c = _dot(jnp.concatenate(ps, axis=0).astype(BF16), vc)
        sel = _select_blocks(ps[0] + ps[1] + ps[2] + ps[3], cover_t, qpos_row, n_slc, min(SLC_TOPN, n_slc))
        tok = (_dot(sel.astype(BF16), expand) > 0.5) & (kpos <= qpos)
        ks = kvs_ref[0:ext, (2 * g) * LANES:(2 * g + 1) * LANES].astype(BF16)
        vs = kvs_ref[0:ext, (2 * g + 1) * LANES:(2 * g + 2) * LANES].astype(BF16)
        o_s = _attend_heads(qg, ks, vs, tok)
        nwb = WINDOW // Q_BLOCK + 1
        k_parts, v_parts, m_parts = [], [], []
        for w in range(nwb):
            kb = i - (nwb - 1) + w
            start = pl.multiple_of(jnp.maximum(kb, 0) * Q_BLOCK, Q_BLOCK)
            k_parts.append(kvw_ref[pl.ds(start, Q_BLOCK), (2 * g) * LANES:(2 * g + 1) * LANES].astype(BF16))
            v_parts.append(kvw_ref[pl.ds(start, Q_BLOCK), (2 * g + 1) * LANES:(2 * g + 2) * LANES].astype(BF16))
            wpos = kb * Q_BLOCK + _iota((1, Q_BLOCK), 1)
            m_parts.append(jnp.where((wpos <= qpos) & (wpos > qpos - WINDOW) & (wpos >= 0), 1.0, 0.0))
        wmask = jnp.concatenate(m_parts, axis=1) > 0.5
        o_w = _attend_heads(qg, jnp.concatenate(k_parts, axis=0), jnp.concatenate(v_parts, axis=0), wmask)
        for r in range(GQA):
            h = g * GQA + r
            rs = slice(r * Q_BLOCK, (r + 1) * Q_BLOCK)
            o_ref[:, h * LANES:(h + 1) * LANES] = (gates[:, 3 * h:3 * h + 1] * o_c[rs]
                                                   + gates[:, 3 * h + 1:3 * h + 2] * o_s[rs]
                                                   + gates[:, 3 * h + 2:3 * h + 3] * o_w[rs])


def _nsa_prompt_kernel(qb_ref, gn_ref, cmp_ref, kvs_ref, kvw_ref, o_ref, *, seq):
    i = pl.program_id(1)
    _for_each_extent(i, seq, lambda ext: _nsa_prompt_body(
        qb_ref, gn_ref, cmp_ref, kvs_ref, kvw_ref, o_ref, i, ext, seq))


def _nsa_prompt(y, cmp, batch, seq):
    nqb = seq // Q_BLOCK
    nsub = seq // CMP_STRIDE
    return pl.pallas_call(
        functools.partial(_nsa_prompt_kernel, seq=seq),
        grid=(batch, nqb),
        in_specs=[
            pl.BlockSpec((Q_BLOCK, 8 * LANES), lambda b, i: (b * nqb + i, C_QB // 8)),
            pl.BlockSpec((Q_BLOCK, LANES), lambda b, i: (b * nqb + i, C_GN)),
            pl.BlockSpec((nsub, 4 * LANES), lambda b, i: (b, 0)),
            pl.BlockSpec((seq, 4 * LANES), lambda b, i: (b, C_KVS // 4)),
            pl.BlockSpec((seq, 4 * LANES), lambda b, i: (b, C_KVW // 4)),
        ],
        out_specs=pl.BlockSpec((Q_BLOCK, 8 * LANES), lambda b, i: (b * nqb + i, 0)),
        out_shape=jax.ShapeDtypeStruct((batch * seq, B_HEADS * HEAD_DIM), F32),
        compiler_params=pltpu.CompilerParams(
            dimension_semantics=("parallel", "arbitrary"), vmem_limit_bytes=VMEM_LIMIT),
        name="nsa_prompt",
    )(y, y, cmp, y, y)


def _merge_kernel(x_ref, ga_ref, gb_ref, oa_ref, ob_ref, woa_ref, wob_ref, wout_ref, o_ref):
    u = (ga_ref[...] * _dot(oa_ref[...].astype(BF16), woa_ref[...])
         + gb_ref[...] * _dot(ob_ref[...].astype(BF16), wob_ref[...]))
    o_ref[...] = x_ref[...] + _dot(u.astype(BF16), wout_ref[...])


def _merge(x2d, y, o_a, o_b, w_o_a, w_o_b, w_out, tm):
    t = x2d.shape[0]
    const = dict(pipeline_mode=pl.Buffered(1))
    return pl.pallas_call(
        _merge_kernel,
        grid=(t // tm,),
        in_specs=[
            pl.BlockSpec((tm, D_MODEL), lambda i: (i, 0)),
            pl.BlockSpec((tm, D_MODEL), lambda i: (i, 0)),
            pl.BlockSpec((tm, D_MODEL), lambda i: (i, 1)),
            pl.BlockSpec((tm, A_HEADS * HEAD_DIM), lambda i: (i, 0)),
            pl.BlockSpec((tm, B_HEADS * HEAD_DIM), lambda i: (i, 0)),
            pl.BlockSpec((A_HEADS * HEAD_DIM, D_MODEL), lambda i: (0, 0), **const),
            pl.BlockSpec((B_HEADS * HEAD_DIM, D_MODEL), lambda i: (0, 0), **const),
            pl.BlockSpec((D_MODEL, D_MODEL), lambda i: (0, 0), **const),
        ],
        out_specs=pl.BlockSpec((tm, D_MODEL), lambda i: (i, 0)),
        out_shape=jax.ShapeDtypeStruct((t, D_MODEL), F32),
        compiler_params=pltpu.CompilerParams(dimension_semantics=("parallel",), vmem_limit_bytes=VMEM_LIMIT),
        name="merge",
    )(x2d, y, y, o_a, o_b, w_o_a, w_o_b, w_out)


def _router_kernel(h_ref, g_ref, whi_ref, wlo_ref, b_ref, xn_ref, comb_ref):
    hf = h_ref[...]
    r = lax.rsqrt(jnp.mean(hf * hf, axis=-1, keepdims=True) + RMS_EPS)
    xn = (hf * r) * g_ref[...]
    xhi = xn.astype(BF16)
    xn_ref[...] = xhi
    xlo = (xn - xhi.astype(F32)).astype(BF16)
    logits = _dot(xhi, whi_ref[...]) + _dot(xlo, whi_ref[...]) + _dot(xhi, wlo_ref[...]) + b_ref[...]
    lane_i = _iota(logits.shape, 1)
    lane = lane_i.astype(F32)
    big = jnp.float32(1e9)
    gm = (lane_i >= N_EXPERTS) & (lane_i < N_EXPERTS + N_GROUPS)
    gmax = jnp.max(jnp.where(gm, logits, -jnp.inf), axis=-1, keepdims=True)
    gsum = jnp.sum(jnp.where(gm, jnp.exp(logits - gmax), 0.0), axis=-1, keepdims=True)
    p_group = 1.0 / gsum
    g_sel = jnp.min(jnp.where(gm & (logits == gmax), lane, big), axis=-1, keepdims=True) - N_EXPERTS
    em = (lane_i < N_EXPERTS) & ((lane_i >> 2).astype(F32) == g_sel)
    emax = jnp.max(jnp.where(em, logits, -jnp.inf), axis=-1, keepdims=True)
    pe = jnp.where(em, jnp.exp(logits - emax), 0.0)
    p_exp = pe / jnp.sum(pe, axis=-1, keepdims=True)
    p1 = jnp.max(jnp.where(em, p_exp, -1.0), axis=-1, keepdims=True)
    i1 = jnp.min(jnp.where(em & (p_exp == p1), lane, big), axis=-1, keepdims=True)
    em2 = em & (lane != i1)
    p2 = jnp.max(jnp.where(em2, p_exp, -1.0), axis=-1, keepdims=True)
    i2 = jnp.min(jnp.where(em2 & (p_exp == p2), lane, big), axis=-1, keepdims=True)
    tot = p1 + p2
    comb = jnp.where(lane == i1, p_group * p1 / tot, jnp.where(lane == i2, p_group * p2 / tot, 0.0))
    comb_ref[...] = jnp.where(lane_i == GROUP_LANE, g_sel, comb)


def _router(h2d, gain, w_hi, w_lo, bias, tm):
    t = h2d.shape[0]
    return pl.pallas_call(
        _router_kernel,
        grid=(t // tm,),
        in_specs=[
            pl.BlockSpec((tm, D_MODEL), lambda i: (i, 0)),
            pl.BlockSpec((1, D_MODEL), lambda i: (0, 0)),
            pl.BlockSpec((D_MODEL, LANES), lambda i: (0, 0)),
            pl.BlockSpec((D_MODEL, LANES), lambda i: (0, 0)),
            pl.BlockSpec((1, LANES), lambda i: (0, 0)),
        ],
        out_specs=[pl.BlockSpec((tm, D_MODEL), lambda i: (i, 0)), pl.BlockSpec((tm, LANES), lambda i: (i, 0))],
        out_shape=[jax.ShapeDtypeStruct((t, D_MODEL), BF16), jax.ShapeDtypeStruct((t, LANES), F32)],
        compiler_params=pltpu.CompilerParams(dimension_semantics=("parallel",), vmem_limit_bytes=VMEM_LIMIT),
        name="router",
    )(h2d, gain, w_hi, w_lo, bias)


MOE_SEGMENT = 512


def _moe_capacity(rows):
    want = rows / N_GROUPS + 4.5 * np.sqrt(rows * (N_GROUPS - 1)) / N_GROUPS
    return max(16, int(-(-want // 16)) * 16)


def _lane_col(x, lane_idx):
    return jnp.sum(jnp.where(_iota(x.shape, 1) == lane_idx, x, 0.0), axis=-1, keepdims=True)


def _expert_ffn(x, wg_ref, wu_ref, wd_ref, ce):
    gate = _dot(x, wg_ref[0])
    hid = (gate * (1.0 / (1.0 + jnp.exp(-gate)))) * _dot(x, wu_ref[0])
    return _dot((hid * ce).astype(BF16), wd_ref[0])


def _moe_kernel(xn_ref, comb_ref, h_ref, wg_ref, wu_ref, wd_ref, gf_ref, o_ref,
                xc_ref, cc_ref, oc_ref, st_ref, cnt_ref, *, cap, n_seg):
    e = pl.program_id(1)
    g = e // EXPERTS_PER_GROUP
    tm = xn_ref.shape[0]

    @pl.when(e == 0)
    def _():
        o_ref[...] = h_ref[...]

    seg = tm // n_seg

    @pl.when(e % EXPERTS_PER_GROUP == 0)
    def _():
        lower = jnp.where(_iota((seg, seg), 0) > _iota((seg, seg), 1), 1.0, 0.0).astype(BF16)
        worst = jnp.float32(0.0)
        for k in range(n_seg):
            rows, slots = slice(k * seg, (k + 1) * seg), slice(k * cap, (k + 1) * cap)
            comb = comb_ref[rows, :]
            member = comb[:, GROUP_LANE:GROUP_LANE + 1] == g.astype(F32)
            mb = jnp.broadcast_to(jnp.where(member, 1.0, 0.0), (seg, LANES))
            worst = jnp.maximum(worst, jnp.sum(mb[:, 0:1]))
            rank_b = _dot(lower, mb.astype(BF16))
            rank_row = rank_b.T[0:1, :]
            m_row = mb.T[0:1, :]
            sel = jnp.where((m_row > 0.5) & (rank_row == _iota((cap, seg), 0).astype(F32)), 1.0, 0.0).astype(BF16)
            st_ref[rows, :] = jnp.where(member & (rank_b[:, 0:1] == _iota((seg, cap), 1).astype(F32)),
                                        1.0, 0.0).astype(BF16)
            xc_ref[slots, :] = _dot(sel, xn_ref[rows, :]).astype(BF16)
            cc_ref[slots, :] = _dot3_exact_lhs(sel, comb)
        cnt_ref[0] = worst.astype(I32)
        oc_ref[...] = jnp.zeros_like(oc_ref)

    fits = cnt_ref[0] <= cap

    @pl.when(fits)
    def _():
        oc_ref[...] += _expert_ffn(xc_ref[...], wg_ref, wu_ref, wd_ref, _lane_col(cc_ref[...], e))

    @pl.when(jnp.logical_not(fits))
    def _():
        o_ref[...] += _expert_ffn(xn_ref[...], wg_ref, wu_ref, wd_ref, _lane_col(comb_ref[...], e))

    @pl.when(fits & (e % EXPERTS_PER_GROUP == EXPERTS_PER_GROUP - 1))
    def _():
        for k in range(n_seg):
            rows = slice(k * seg, (k + 1) * seg)
            oc = oc_ref[k * cap:(k + 1) * cap, :]
            hi = oc.astype(BF16)
            lo = (oc - hi.astype(F32)).astype(BF16)
            o_ref[rows, :] += _dot(st_ref[rows, :], hi) + _dot(st_ref[rows, :], lo)

    @pl.when(e == N_EXPERTS - 1)
    def _():
        out = o_ref[...]
        r = lax.rsqrt(jnp.mean(out * out, axis=-1, keepdims=True) + RMS_EPS)
        o_ref[...] = (out * r) * gf_ref[...]


def _moe(xn, comb, h2d, w_g, w_u, w_d, gain_final, tm):
    t = xn.shape[0]
    n_seg = max(1, tm // MOE_SEGMENT)
    cap = _moe_capacity(tm // n_seg)
    return pl.pallas_call(
        functools.partial(_moe_kernel, cap=cap, n_seg=n_seg),
        grid=(t // tm, N_EXPERTS),
        in_specs=[
            pl.BlockSpec((tm, D_MODEL), lambda i, e: (i, 0)),
            pl.BlockSpec((tm, LANES), lambda i, e: (i, 0)),
            pl.BlockSpec((tm, D_MODEL), lambda i, e: (i, 0), pipeline_mode=pl.Buffered(1)),
            pl.BlockSpec((1, D_MODEL, EXPERT_FF), lambda i, e: (e, 0, 0)),
            pl.BlockSpec((1, D_MODEL, EXPERT_FF), lambda i, e: (e, 0, 0)),
            pl.BlockSpec((1, EXPERT_FF, D_MODEL), lambda i, e: (e, 0, 0)),
            pl.BlockSpec((1, D_MODEL), lambda i, e: (0, 0)),
        ],
        out_specs=pl.BlockSpec((tm, D_MODEL), lambda i, e: (i, 0), pipeline_mode=pl.Buffered(1)),
        out_shape=jax.ShapeDtypeStruct((t, D_MODEL), F32),
        scratch_shapes=[pltpu.VMEM((n_seg * cap, D_MODEL), BF16),
                        pltpu.VMEM((n_seg * cap, LANES), F32), pltpu.VMEM((n_seg * cap, D_MODEL), F32),
                        pltpu.VMEM((tm, cap), BF16), pltpu.SMEM((1,), I32)],
        compiler_params=pltpu.CompilerParams(
            dimension_semantics=("parallel", "arbitrary"), vmem_limit_bytes=VMEM_LIMIT),
        name="moe",
    )(xn, comb, h2d, w_g, w_u, w_d, gain_final)


def _page_specs(block, n_chunk_pages, row_of):
    def make(k):
        def index_map(s, c, pt):
            return (pt[s, c * n_chunk_pages + k],) + row_of
        return pl.BlockSpec(block, index_map)
    return [make(k) for k in range(n_chunk_pages)]


def _idx_sample_kernel(pt_ref, q_ref, w_ref, knew_ref, *rest, past):
    pages, (o_ref, onew_ref) = rest[:IDX_PAGES], rest[IDX_PAGES:]
    c = pl.program_id(1)
    q = q_ref[...].astype(BF16)
    w = w_ref[...]

    def scores(kt):
        rel = jnp.maximum(_dot(q, kt.astype(BF16)), 0.0) * w
        return jnp.sum(rel.reshape(IDX_HEADS, 8, kt.shape[1]), axis=0) * IDX_SCALE

    kt = jnp.concatenate([jnp.concatenate([p[0], p[0]], axis=0) for p in pages], axis=1)
    o_ref[...] = scores(kt)

    @pl.when(c == 0)
    def _():
        t = _iota((8, LANES), 0) & 3
        j = _iota((8, LANES), 1)
        onew_ref[...] = jnp.where(j <= t, scores(knew_ref[...]), NEG)


def _idx_sample(pt, q_rows, w_col, knew_t, kidx_t):
    ns, n_pages = pt.shape
    n_chunks = n_pages // IDX_PAGES
    past = n_pages * PAGE
    width = IDX_PAGES * PAGE
    return pl.pallas_call(
        functools.partial(_idx_sample_kernel, past=past),
        grid_spec=pltpu.PrefetchScalarGridSpec(
            num_scalar_prefetch=1, grid=(ns, n_chunks),
            in_specs=[
                pl.BlockSpec((None, LANES, LANES), lambda s, c, pt: (s, 0, 0)),
                pl.BlockSpec((None, LANES, 1), lambda s, c, pt: (s, 0, 0)),
                pl.BlockSpec((None, LANES, LANES), lambda s, c, pt: (s, 0, 0)),
            ] + _page_specs((1, IDX_DIM, PAGE), IDX_PAGES, (0, 0)),
            out_specs=[pl.BlockSpec((None, 8, width), lambda s, c, pt: (s, 0, c)),
                       pl.BlockSpec((None, 8, LANES), lambda s, c, pt: (s, 0, 0))]),
        out_shape=[jax.ShapeDtypeStruct((ns, 8, past), F32), jax.ShapeDtypeStruct((ns, 8, LANES), F32)],
        compiler_params=pltpu.CompilerParams(
            dimension_semantics=("parallel", "arbitrary"), vmem_limit_bytes=VMEM_LIMIT),
        name="idx_sample",
    )(pt, q_rows, w_col, knew_t, *([kidx_t] * IDX_PAGES))


def _topk_sample_kernel(s_ref, snew_ref, m_ref, mnew_ref, *, past, k):
    n = s_ref.shape[0] * 8
    s = jnp.concatenate([s_ref[...].reshape(n, past), snew_ref[...].reshape(n, LANES)], axis=1)
    sel = _topk_mask(s, k, int(np.ceil(np.log2(past + LANES))))
    sel = jnp.where(sel, 1.0, 0.0)
    m_ref[...] = sel[:, :past].reshape(m_ref.shape)
    mnew_ref[...] = sel[:, past:].reshape(mnew_ref.shape)


def _topk_sample(scores, scores_new, k):
    ns, _, past = scores.shape
    sps = TOPK_SEQS_PER_STEP if ns % TOPK_SEQS_PER_STEP == 0 else 1
    return pl.pallas_call(
        functools.partial(_topk_sample_kernel, past=past, k=k),
        grid=(ns // sps,),
        in_specs=[pl.BlockSpec((sps, 8, past), lambda s: (s, 0, 0)),
                  pl.BlockSpec((sps, 8, LANES), lambda s: (s, 0, 0))],
        out_specs=[pl.BlockSpec((sps, 8, past), lambda s: (s, 0, 0)),
                   pl.BlockSpec((sps, 8, LANES), lambda s: (s, 0, 0))],
        out_shape=[jax.ShapeDtypeStruct((ns, 8, past), F32), jax.ShapeDtypeStruct((ns, 8, LANES), F32)],
        compiler_params=pltpu.CompilerParams(dimension_semantics=("parallel",), vmem_limit_bytes=VMEM_LIMIT),
        name="topk_sample",
    )(scores, scores_new)


def _rows16_from_rows8(g):
    return jnp.where(_iota((16, 8), 1) == (g * 4 + (_iota((16, 8), 0) >> 2)), 1.0, 0.0).astype(BF16)


def _paged_attn_kernel(pt_ref, q_ref, knew_ref, vnew_ref, m_ref, mnew_ref, *rest, block_mask):
    pages, (o_ref,), (m_sc, l_sc, acc_sc) = rest[:ATTN_PAGES], rest[ATTN_PAGES:ATTN_PAGES + 1], \
        rest[ATTN_PAGES + 1:]
    c = pl.program_id(1)
    width = ATTN_PAGES * PAGE

    @pl.when(c == 0)
    def _():
        m_sc[...] = jnp.full_like(m_sc, NEG)
        l_sc[...] = jnp.zeros_like(l_sc)
        acc_sc[...] = jnp.zeros_like(acc_sc)

    if block_mask:
        nb = m_ref.shape[1]
        first = c * (width // SLC_BLOCK)
        expand = jnp.where(_iota((nb, width), 0) == first + (_iota((nb, width), 1) >> 6), 1.0, 0.0).astype(BF16)
        mask8 = _dot(m_ref[...].astype(BF16), expand)
    else:
        mask8 = m_ref[...]
    mask8 = mask8.astype(BF16)

    def update(g, kg, vg, mask16):
        rs = slice(g * 16, (g + 1) * 16)
        s = _dot_nt(q_ref[g].astype(BF16), kg) * ATT_SCALE
        s = jnp.where(mask16, s, NEG)
        m_old = m_sc[rs]
        m_new = jnp.maximum(m_old, jnp.max(s, axis=-1, keepdims=True))
        p = jnp.where(mask16, jnp.exp(s - m_new), 0.0)
        alpha = jnp.exp(m_old - m_new)
        l_sc[rs] = alpha * l_sc[rs] + jnp.sum(p, axis=-1, keepdims=True)
        acc_sc[rs] = alpha * acc_sc[rs] + _dot(p.astype(BF16), vg)
        m_sc[rs] = m_new

    for g in range(2):
        kg = jnp.concatenate([p[0, pl.ds(2 * g, PAGE, stride=4), :] for p in pages], axis=0).astype(BF16)
        vg = jnp.concatenate([p[0, pl.ds(2 * g + 1, PAGE, stride=4), :] for p in pages], axis=0).astype(BF16)
        update(g, kg, vg, _dot(_rows16_from_rows8(g), mask8) > 0.5)

    @pl.when(c == pl.num_programs(1) - 1)
    def _():
        mnew8 = mnew_ref[...].astype(BF16)
        for g in range(2):
            update(g, knew_ref[g].astype(BF16), vnew_ref[g].astype(BF16), _dot(_rows16_from_rows8(g), mnew8) > 0.5)
        o_ref[...] = acc_sc[...] / jnp.maximum(l_sc[...], 1e-30)


def _paged_attn(pt, q_g, knew, vnew, mask, mask_new, pool, block_mask, name):
    ns, n_pages = pt.shape
    n_chunks = n_pages // ATTN_PAGES
    width = ATTN_PAGES * PAGE
    if block_mask:
        mspec = pl.BlockSpec((None, 8, mask.shape[2]), lambda s, c, pt: (s, 0, 0))
    else:
        mspec = pl.BlockSpec((None, 8, width), lambda s, c, pt: (s, 0, c))
    return pl.pallas_call(
        functools.partial(_paged_attn_kernel, block_mask=block_mask),
        grid_spec=pltpu.PrefetchScalarGridSpec(
            num_scalar_prefetch=1, grid=(ns, n_chunks),
            in_specs=[
                pl.BlockSpec((None, 2, 16, HEAD_DIM), lambda s, c, pt: (s, 0, 0, 0)),
                pl.BlockSpec((None, 2, LANES, HEAD_DIM), lambda s, c, pt: (s, 0, 0, 0)),
                pl.BlockSpec((None, 2, LANES, HEAD_DIM), lambda s, c, pt: (s, 0, 0, 0)),
                mspec,
                pl.BlockSpec((None, 8, LANES), lambda s, c, pt: (s, 0, 0)),
            ] + _page_specs((1, 4 * PAGE, HEAD_DIM), ATTN_PAGES, (0, 0)),
            out_specs=pl.BlockSpec((None, 32, HEAD_DIM), lambda s, c, pt: (s, 0, 0)),
            scratch_shapes=[pltpu.VMEM((32, 1), F32), pltpu.VMEM((32, 1), F32), pltpu.VMEM((32, HEAD_DIM), F32)]),
        out_shape=jax.ShapeDtypeStruct((ns, 32, HEAD_DIM), F32),
        compiler_params=pltpu.CompilerParams(
            dimension_semantics=("parallel", "arbitrary"), vmem_limit_bytes=VMEM_LIMIT),
        name=name,
    )(pt, q_g, knew, vnew, mask, mask_new, *([pool] * ATTN_PAGES))


def _compress_sample_kernel(pt_ref, w_ref, *rest):
    pages, (o_ref,) = rest[:CMP_PAGES], rest[CMP_PAGES:]
    sub = PAGE // CMP_STRIDE
    r = _iota((PAGE, PAGE), 0)
    perm = jnp.where(_iota((PAGE, PAGE), 1) == (r & (sub - 1)) * CMP_STRIDE + (r >> 3), 1.0, 0.0).astype(BF16)
    taps = []
    for p in pages:
        a = jnp.concatenate([p[0, pl.ds(gc, PAGE, stride=4), :] for gc in range(4)], axis=1)
        taps.append(_dot(perm, a.astype(BF16)))
    for c in range(2):
        xs = []
        for g in range(B_KV_HEADS):
            col = (2 * g + c) * LANES
            xs.append(jnp.concatenate(
                [jnp.concatenate([t[l * sub:(l + 1) * sub, col:col + LANES] for l in range(CMP_STRIDE)], axis=1)
                 for t in taps], axis=0))
        pq = _dot(jnp.concatenate(xs, axis=0).astype(BF16), w_ref[c])
        n = CMP_PAGES * sub
        for g in range(B_KV_HEADS):
            o_ref[:, (2 * g + c) * 2 * LANES:(2 * g + c + 1) * 2 * LANES] = pq[g * n:(g + 1) * n]


def _compress_sample(pt, w_cat, pool):
    ns, n_pages = pt.shape
    n_chunks = n_pages // CMP_PAGES
    rows = CMP_PAGES * (PAGE // CMP_STRIDE)
    return pl.pallas_call(
        _compress_sample_kernel,
        grid_spec=pltpu.PrefetchScalarGridSpec(
            num_scalar_prefetch=1, grid=(ns, n_chunks),
            in_specs=[pl.BlockSpec((2, CMP_STRIDE * HEAD_DIM, 2 * LANES), lambda s, c, pt: (0, 0, 0))]
            + _page_specs((1, 4 * PAGE, HEAD_DIM), CMP_PAGES, (0, 0)),
            out_specs=pl.BlockSpec((None, rows, 8 * LANES), lambda s, c, pt: (s, c, 0))),
        out_shape=jax.ShapeDtypeStruct((ns, n_chunks * rows, 8 * LANES), F32),
        compiler_params=pltpu.CompilerParams(
            dimension_semantics=("parallel", "arbitrary"), vmem_limit_bytes=VMEM_LIMIT),
        name="compress_sample",
    )(pt, w_cat, *([pool] * CMP_PAGES))


def _nsa_select_sample_kernel(pq_ref, w_ref, pe_ref, tab_ref, q_ref, oc_ref, bm_ref, *, past, n_blk_pad):
    nsub = past // CMP_STRIDE
    total = past + 4
    n_cmp = (total - CMP_BLOCK) // CMP_STRIDE + 1
    n_slc = -(-total // SLC_BLOCK)
    row_t = _iota((16, 1), 0) >> 2
    c_end = _iota((1, nsub), 1) * CMP_STRIDE + (CMP_BLOCK - 1)
    cmask = (c_end <= past + row_t) & (_iota((1, nsub), 1) < n_cmp)
    p_rows = []
    for g in range(B_KV_HEADS):
        comp = []
        for c in range(2):
            base = (2 * g + c) * 2 * LANES
            v = pq_ref[:, base:base + LANES] + pltpu.roll(pq_ref[:, base + LANES:base + 2 * LANES], nsub - 1, 0)
            v = v + _pe_bias(pe_ref, w_ref, c)
            if c == 0:
                v = _rope(v, tab_ref[:, 0:128], tab_ref[:, 128:256], tab_ref[:, 256:384], ROPE_DIM // 2)
            comp.append(v.astype(BF16))
        s = _dot_nt(q_ref[g].astype(BF16), comp[0]) * ATT_SCALE
        p, l = _masked_softmax_parts(s, cmask)
        p = p / jnp.maximum(l, 1e-30)
        oc_ref[g * 16:(g + 1) * 16, :] = _dot(p.astype(BF16), comp[1])
        p_rows.append(p)
    p_all = jnp.concatenate(p_rows, axis=0)
    gather = jnp.where(_iota((8, 32), 0) == (_iota((8, 32), 1) >> 2), 1.0, 0.0).astype(BF16)
    p_sum = _dot3_exact_lhs(gather, p_all)
    p_s = _dot3_exact_rhs(p_sum, _cover(nsub, n_blk_pad, n_cmp, n_slc))
    qpos = past + (_iota((8, 1), 0) & 3)
    sel = _topk_mask(_block_scores(p_s, qpos, n_slc), min(SLC_TOPN, n_slc), int(np.ceil(np.log2(n_blk_pad))))
    bm_ref[...] = jnp.where(sel, 1.0, 0.0)


def _nsa_select_sample(pq, w_cat, pe_r, tab_c, q_g, past):
    ns, nsub, _ = pq.shape
    n_slc = -(-(past + 4) // SLC_BLOCK)
    n_blk_pad = -(-n_slc // LANES) * LANES
    return pl.pallas_call(
        functools.partial(_nsa_select_sample_kernel, past=past, n_blk_pad=n_blk_pad),
        grid=(ns,),
        in_specs=[
            pl.BlockSpec((None, nsub, 8 * LANES), lambda s: (s, 0, 0)),
            pl.BlockSpec((2, CMP_STRIDE * HEAD_DIM, 2 * LANES), lambda s: (0, 0, 0)),
            pl.BlockSpec((2, 8, CMP_STRIDE * HEAD_DIM), lambda s: (0, 0, 0)),
            pl.BlockSpec((nsub, 3 * LANES), lambda s: (0, 0)),
            pl.BlockSpec((None, 2, 16, HEAD_DIM), lambda s: (s, 0, 0, 0)),
        ],
        out_specs=[pl.BlockSpec((None, 32, HEAD_DIM), lambda s: (s, 0, 0)),
                   pl.BlockSpec((None, 8, n_blk_pad), lambda s: (s, 0, 0))],
        out_shape=[jax.ShapeDtypeStruct((ns, 32, HEAD_DIM), F32), jax.ShapeDtypeStruct((ns, 8, n_blk_pad), F32)],
        compiler_params=pltpu.CompilerParams(dimension_semantics=("parallel",), vmem_limit_bytes=VMEM_LIMIT),
        name="nsa_select_sample",
    )(pq, w_cat, pe_r, tab_c, q_g)


def _window_sample_kernel(win_ref, knew_ref, vnew_ref, q_ref, oc_ref, os_ref, gate_ref, o_ref, *, wb):
    row_t = _iota((16, 1), 0) >> 2
    kpos = _iota((1, wb), 1)
    mask_old = kpos > row_t + (wb - WINDOW)
    mask_new = _iota((1, LANES), 1) <= row_t
    for g in range(B_KV_HEADS):
        rs = slice(g * 16, (g + 1) * 16)
        q = q_ref[g].astype(BF16)
        kw = win_ref[pl.ds(2 * g, wb, stride=4), :].astype(BF16)
        vw = win_ref[pl.ds(2 * g + 1, wb, stride=4), :].astype(BF16)
        s = jnp.concatenate([_dot_nt(q, kw), _dot_nt(q, knew_ref[g].astype(BF16))], axis=1) * ATT_SCALE
        mask = jnp.concatenate([jnp.where(mask_old, 1.0, 0.0), jnp.where(mask_new, 1.0, 0.0)], axis=1) > 0.5
        p, l = _masked_softmax_parts(s, mask)
        o_w = (_dot(p[:, :wb].astype(BF16), vw) + _dot(p[:, wb:].astype(BF16), vnew_ref[g].astype(BF16)))
        o_w = o_w / jnp.maximum(l, 1e-30)
        gt = 1.0 / (1.0 + jnp.exp(-gate_ref[rs]))
        o_ref[rs] = gt[:, 0:1] * oc_ref[rs] + gt[:, 1:2] * os_ref[rs] + gt[:, 2:3] * o_w


def _window_sample(win_rows, knew, vnew, q_g, o_c, o_s, gates, wb):
    ns = q_g.shape[0]
    blk = lambda *shape: pl.BlockSpec((None,) + shape, lambda s: (s,) + (0,) * len(shape))
    return pl.pallas_call(
        functools.partial(_window_sample_kernel, wb=wb),
        grid=(ns,),
        in_specs=[blk(4 * wb, HEAD_DIM), blk(2, LANES, HEAD_DIM), blk(2, LANES, HEAD_DIM), blk(2, 16, HEAD_DIM),
                  blk(32, HEAD_DIM), blk(32, HEAD_DIM), blk(32, LANES)],
        out_specs=blk(32, HEAD_DIM),
        out_shape=jax.ShapeDtypeStruct((ns, 32, HEAD_DIM), F32),
        compiler_params=pltpu.CompilerParams(dimension_semantics=("parallel",), vmem_limit_bytes=VMEM_LIMIT),
        name="window_sample",
    )(win_rows, knew, vnew, q_g, o_c, o_s, gates)


def _rope_table(pos, half, period):
    inv = ROPE_THETA ** (-jnp.arange(half, dtype=F32) / half)
    ang = pos.astype(F32)[:, None] * inv[None, :]
    cos, sin = jnp.cos(ang), jnp.sin(ang)
    lane = np.arange(LANES) % period
    idx = lane % half
    lo = jnp.asarray(lane < half)[None, :]
    hi = jnp.asarray((lane >= half) & (lane < 2 * half))[None, :]
    c = jnp.where(lo | hi, cos[:, idx], 1.0)
    a = jnp.where(lo, -sin[:, idx], 0.0)
    b = jnp.where(hi, sin[:, idx], 0.0)
    return jnp.concatenate([c, a, b], axis=1)


def _proj_tables(pos):
    return jnp.stack([_rope_table(pos, ROPE_DIM // 2, LANES), _rope_table(pos, IDX_ROPE_DIM // 2, IDX_DIM)])


def _chunk_kinds():
    kinds = np.zeros((N_CHUNKS,), np.int32)
    kinds[C_QA:C_QA + 8] = K_ROPE_HEAD
    kinds[C_QI:C_QI + 8] = K_ROPE_IDX
    kinds[C_QB:C_QB + 8] = K_ROPE_HEAD
    for base in (C_KVA, C_KVS, C_KVW):
        kinds[base] = K_ROPE_HEAD
        kinds[base + 2] = K_ROPE_HEAD
    kinds[C_KIW] = K_ROPE_IDX_LOW
    return jnp.asarray(kinds)


def _pack_w_in(w):
    sizes = [("q_a", 1024), ("kv_a", 512), ("q_i", 1024), ("k_i", 64), ("w_i", 16), ("q_b", 1024), ("kv_c", 512),
             ("kv_s", 512), ("kv_w", 512), ("g_n", 24), ("g_m", 4096)]
    parts, off = {}, 0
    for name, n in sizes:
        parts[name] = w[:, off:off + n]
        off += n
    z = lambda n: jnp.zeros((w.shape[0], n), w.dtype)
    cols = [parts["g_m"], parts["q_a"], parts["q_i"], parts["q_b"], parts["kv_a"], parts["kv_c"], parts["kv_s"],
            parts["kv_w"], parts["k_i"], parts["w_i"], z(48), parts["g_n"], z(104), z(2 * LANES)]
    return jnp.concatenate(cols, axis=1).astype(BF16)


def _cols(y, chunk, n):
    return y[:, chunk * LANES:chunk * LANES + n]


def _group_rows(q):
    ns = q.shape[0]
    return q.reshape(ns, 4, 2, GQA, HEAD_DIM).transpose(0, 2, 1, 3, 4).reshape(ns, 2, 16, HEAD_DIM)


def _new_kv(kv):
    ns = kv.shape[0]
    kv = kv.reshape(ns, 4, 2, 2, HEAD_DIM).transpose(0, 2, 3, 1, 4)
    kv = jnp.pad(kv, ((0, 0), (0, 0), (0, 0), (0, LANES - 4), (0, 0)))
    return kv[:, :, 0], kv[:, :, 1]


def _ungroup_rows(o):
    ns = o.shape[0]
    return o.reshape(ns, 2, 4, GQA, HEAD_DIM).transpose(0, 2, 1, 3, 4).reshape(ns * 4, A_HEADS * HEAD_DIM)


def kernel(x_prompt, x_sample, cache_kv_a, cache_kidx, cache_kv_cmp, cache_kv_slc, state_kv_win, page_table,
           norm_mix, w_in, w_phi, pe_phi, w_o_a, w_o_b, w_out, norm_ffn, w_router_group, b_router_group,
           w_router_expert, b_router_expert, w_gate, w_up, w_down, norm_final):
    batch, seq, _ = x_prompt.shape
    ns, dec, _ = x_sample.shape
    assert dec == 4 and norm_mix.shape[0] == 1, "kernel is written for DEC_SEQ=4 and DEPTH=1"
    n_pool = cache_kv_a.shape[1]
    n_pages = page_table.shape[1]
    past = n_pages * PAGE
    wb = state_kv_win.shape[2]

    w_packed = _pack_w_in(w_in[0])
    kinds = _chunk_kinds()
    wp = w_phi[0].astype(BF16)
    w_cat = jnp.concatenate([wp[:, :CMP_STRIDE].reshape(2, CMP_STRIDE * HEAD_DIM, HEAD_DIM),
                             wp[:, CMP_STRIDE:].reshape(2, CMP_STRIDE * HEAD_DIM, HEAD_DIM)], axis=2)
    pe_r = pe_phi[0].transpose(1, 0, 2).reshape(2, 2, CMP_STRIDE * HEAD_DIM)
    pe_r = jnp.pad(pe_r, ((0, 0), (0, 6), (0, 0)))
    woa, wob, wout = w_o_a[0].astype(BF16), w_o_b[0].astype(BF16), w_out[0].astype(BF16)
    w_r = jnp.concatenate([w_router_expert[0], w_router_group[0],
                           jnp.zeros((D_MODEL, LANES - N_EXPERTS - N_GROUPS), F32)], axis=1)
    w_r_hi = w_r.astype(BF16)
    w_r_lo = (w_r - w_r_hi.astype(F32)).astype(BF16)
    b_r = jnp.concatenate([b_router_expert[0], b_router_group[0],
                           jnp.zeros((LANES - N_EXPERTS - N_GROUPS,), F32)])[None, :]
    w_g, w_u, w_d = w_gate[0].astype(BF16), w_up[0].astype(BF16), w_down[0].astype(BF16)
    g_mix, g_ffn, g_fin = norm_mix[0][None, :], norm_ffn[0][None, :], norm_final[None, :]

    xp = x_prompt.reshape(batch * seq, D_MODEL)
    tm_p = 1024
    y_p, kva_p, kvc_p, kvs_p, kvw_p = _project(xp, g_mix, w_packed, kinds, _proj_tables(jnp.arange(seq)), tm_p,
                                               seq // tm_p)
    o_a_p = _dsa_prompt(y_p, batch, seq)
    nsub_p = seq // CMP_STRIDE
    tab_cp = _rope_table(jnp.arange(nsub_p) * CMP_STRIDE + CMP_BLOCK - 1, ROPE_DIM // 2, LANES)
    cmp_p = _compress_prompt(y_p, w_cat, pe_r, tab_cp, batch, seq)
    o_b_p = _nsa_prompt(y_p, cmp_p, batch, seq)
    h_p = _merge(xp, y_p, o_a_p, o_b_p, woa, wob, wout, 256)
    xn_p, comb_p = _router(h_p, g_ffn, w_r_hi, w_r_lo, b_r, 512)
    y_prompt = _moe(xn_p, comb_p, h_p, w_g, w_u, w_d, g_fin, 1024).reshape(batch, seq, D_MODEL)

    xs = x_sample.reshape(ns * 4, D_MODEL)
    tm_s = ns * 4
    pos_s = past + (jnp.arange(ns * 4) % 4)
    y_s, kva_s, kvc_s, kvs_s, kvw_s = _project(xs, g_mix, w_packed, kinds, _proj_tables(pos_s), tm_s, 1)
    pt = page_table.astype(I32)
    qi = _cols(y_s, C_QI, 1024).reshape(ns, 4, IDX_HEADS, IDX_DIM).transpose(0, 2, 1, 3)
    qi = jnp.concatenate([qi, qi], axis=2)
    par = (jnp.arange(IDX_HEADS) % 2)[None, :, None, None, None]
    q_rows = jnp.where(par == jnp.arange(2)[None, None, None, :, None], qi[:, :, :, None, :], 0.0)
    q_rows = q_rows.reshape(ns, IDX_HEADS * 8, 2 * IDX_DIM)
    kiw = _cols(y_s, C_KIW, LANES).reshape(ns, 4, LANES)
    wi = kiw[:, :, IDX_DIM:IDX_DIM + IDX_HEADS].transpose(0, 2, 1)
    w_col = jnp.concatenate([wi, wi], axis=2).reshape(ns, IDX_HEADS * 8, 1)
    knew_t = jnp.pad(kiw[:, :, :IDX_DIM].transpose(0, 2, 1), ((0, 0), (0, 0), (0, LANES - 4)))
    knew_t = jnp.concatenate([knew_t, knew_t], axis=1)
    kidx_t = jnp.swapaxes(cache_kidx[0], 1, 2)
    sc, sc_new = _idx_sample(pt, q_rows, w_col, knew_t, kidx_t)
    m_a, m_a_new = _topk_sample(sc, sc_new, min(DSA_TOPK, (past + 4) // 4))
    q_a = _group_rows(_cols(y_s, C_QA, 1024).reshape(ns, 4, A_HEADS, HEAD_DIM))
    ka_new, va_new = _new_kv(_cols(y_s, C_KVA, 512).reshape(ns, 4, 512))
    o_a_s = _paged_attn(pt, q_a, ka_new, va_new, m_a, m_a_new, cache_kv_a.reshape(n_pool, 4 * PAGE, HEAD_DIM),
                        False, "dsa_sample")
    q_b = _group_rows(_cols(y_s, C_QB, 1024).reshape(ns, 4, B_HEADS, HEAD_DIM))
    pq = _compress_sample(pt, w_cat, cache_kv_cmp.reshape(n_pool, 4 * PAGE, HEAD_DIM))
    nsub_s = past // CMP_STRIDE
    tab_cs = _rope_table(jnp.arange(nsub_s) * CMP_STRIDE + CMP_BLOCK - 1, ROPE_DIM // 2, LANES)
    o_c, blk_mask = _nsa_select_sample(pq, w_cat, pe_r, tab_cs, q_b, past)
    ks_new, vs_new = _new_kv(_cols(y_s, C_KVS, 512).reshape(ns, 4, 512))
    first_new = past // SLC_BLOCK
    t_row = (jnp.arange(8) & 3)[None, :, None]
    m_s_new = jnp.where(jnp.arange(LANES)[None, None, :] <= t_row, blk_mask[:, :, first_new:first_new + 1], 0.0)
    o_s = _paged_attn(pt, q_b, ks_new, vs_new, blk_mask, m_s_new,
                      cache_kv_slc.reshape(n_pool, 4 * PAGE, HEAD_DIM), True, "slc_sample")
    kw_new, vw_new = _new_kv(_cols(y_s, C_KVW, 512).reshape(ns, 4, 512))
    gates = _group_rows(jnp.pad(_cols(y_s, C_GN, 24).reshape(ns, 4, B_HEADS, 3), ((0, 0),) * 3 + ((0, LANES - 3),)))
    gates = gates.reshape(ns, 32, LANES)
    o_b_s = _window_sample(state_kv_win[0].reshape(ns, 4 * wb, HEAD_DIM), kw_new, vw_new, q_b, o_c, o_s, gates, wb)
    h_s = _merge(xs, y_s, _ungroup_rows(o_a_s), _ungroup_rows(o_b_s), woa, wob, wout, tm_s)
    xn_s, comb_s = _router(h_s, g_ffn, w_r_hi, w_r_lo, b_r, tm_s)
    y_sample = _moe(xn_s, comb_s, h_s, w_g, w_u, w_d, g_fin, tm_s).reshape(ns, 4, D_MODEL)

    def kv_out(rows, lead):
        return rows.reshape((1,) + lead + (2, 2, HEAD_DIM))

    wb_p = min(WINDOW, seq)
    win_s = jnp.concatenate([state_kv_win[0][:, 4:], kvw_s.reshape(ns, 4, 2, 2, HEAD_DIM)], axis=1)
    return (y_prompt, y_sample,
            kv_out(kva_p, (batch, seq)), _cols(y_p, C_KIW, IDX_DIM).reshape(1, batch, seq, IDX_DIM),
            kv_out(kvc_p, (batch, seq)), kv_out(kvs_p, (batch, seq)), kv_out(kvw_p, (batch, seq))[:, :, seq - wb_p:],
            kv_out(kva_s, (ns, 4)), _cols(y_s, C_KIW, IDX_DIM).reshape(1, ns, 4, IDX_DIM),
            kv_out(kvc_s, (ns, 4)), kv_out(kvs_s, (ns, 4)), win_s[None])
```

```python
import functools

import numpy as np
import jax
import jax.numpy as jnp
from jax import lax
from jax.experimental import pallas as pl
from jax.experimental.pallas import tpu as pltpu

F32 = jnp.float32
BF16 = jnp.bfloat16
I32 = jnp.int32

LANES = 128
VMEM_LIMIT = 56 * 1024 * 1024

D_MODEL = 2048
HEAD_DIM = 128
ROPE_DIM = HEAD_DIM // 4
ROPE_THETA = 500000.0
A_HEADS = 8
A_KV_HEADS = 2
IDX_HEADS = 16
IDX_DIM = 64
IDX_ROPE_DIM = IDX_DIM // 4
DSA_TOPK = 256
B_HEADS = 8
B_KV_HEADS = 2
GQA = 4
CMP_BLOCK = 32
CMP_STRIDE = 16
SLC_BLOCK = 64
SLC_TOPN = 16
SLC_LOCAL = 2
WINDOW = 512
PAGE = 128
N_GROUPS = 4
EXPERTS_PER_GROUP = 4
N_EXPERTS = 16
GROUP_LANE = N_EXPERTS
EXPERT_FF = 512
Q_BLOCK = 128
RMS_EPS = 1e-6
NEG = -1e30
FORCE_BONUS = 1e4
ATT_SCALE = HEAD_DIM ** -0.5
IDX_SCALE = (IDX_DIM * IDX_HEADS) ** -0.5
IDX_PAGES = 128
ATTN_PAGES = 64
CMP_PAGES = 32
TOPK_SEQS_PER_STEP = 8
MAX_CAUSAL_EXTENTS = 4

C_GM, C_QA, C_QI, C_QB, C_KVA, C_KVC, C_KVS, C_KVW, C_KIW, C_GN = 0, 32, 40, 48, 56, 60, 64, 68, 72, 73
N_CHUNKS = 76
NP = N_CHUNKS * LANES
K_NONE, K_ROPE_HEAD, K_ROPE_IDX, K_SIGMOID, K_ROPE_IDX_LOW = 0, 1, 2, 3, 4

NT_DIMS = (((1,), (1,)), ((), ()))


def _dot(a, b):
    return jnp.dot(a, b, preferred_element_type=F32)


def _dot_nt(a, b):
    return lax.dot_general(a, b, NT_DIMS, preferred_element_type=F32)


def _split3(x):
    hi = x.astype(BF16)
    r1 = x - hi.astype(F32)
    mid = r1.astype(BF16)
    lo = (r1 - mid.astype(F32)).astype(BF16)
    return hi, mid, lo


def _dot3_exact_rhs(x, rhs_bf16):
    hi, mid, lo = _split3(x)
    return _dot(hi, rhs_bf16) + _dot(mid, rhs_bf16) + _dot(lo, rhs_bf16)


def _dot3_exact_lhs(lhs_bf16, x):
    hi, mid, lo = _split3(x)
    return _dot(lhs_bf16, hi) + _dot(lhs_bf16, mid) + _dot(lhs_bf16, lo)


def _iota(shape, dim):
    return lax.broadcasted_iota(I32, shape, dim)


def _count(mask):
    return jnp.sum(jnp.where(mask, 1.0, 0.0), axis=-1, keepdims=True)


def _topk_mask(score, k, idx_bits):
    rows = score.shape[0]
    imin = jnp.int32(-2 ** 31)
    kf = jnp.float32(k)
    n_part = 2 if rows % 16 == 0 else 1
    pr = rows // n_part
    parts = [score[h * pr:(h + 1) * pr] for h in range(n_part)]

    def float_of(u):
        key = u ^ imin
        return pltpu.bitcast(key ^ ((key >> 31) & jnp.int32(0x7FFFFFFF)), F32)

    def value_bit(i, tus):
        bit = lax.shift_left(jnp.int32(1), 31 - i)
        out = []
        for s_h, tu in zip(parts, tus):
            cand = tu | bit
            cnt = _count(s_h >= float_of(cand))
            out.append(jnp.where(cnt >= kf, cand, tu))
        return tuple(out)

    tus = lax.fori_loop(0, 32, value_bit, tuple(jnp.zeros((pr, 1), I32) for _ in range(n_part)))
    tu = jnp.concatenate(tus, axis=0) if n_part > 1 else tus[0]
    ge_hi = score >= float_of(tu + 1)
    bucket = (score >= float_of(tu)) & jnp.logical_not(ge_hi)
    want = kf - _count(ge_hi)

    def unsettled(c):
        return jnp.max(jnp.where(c[1] < want, 1.0, 0.0)) > 0.0

    def next_value(c):
        v, cnt = c
        below = jnp.max(jnp.where(bucket & (score < v), score, -jnp.inf), axis=-1, keepdims=True)
        v = jnp.where(cnt < want, below, v)
        return v, _count(bucket & (score >= v))

    thr, _ = lax.while_loop(unsettled, next_value, (jnp.full((rows, 1), jnp.inf, F32), jnp.zeros((rows, 1), F32)))
    gt = score > thr
    eq = score == thr
    need = kf - _count(gt)
    idx = _iota(score.shape, 1)

    def index_bit(i, j):
        cand = j | lax.shift_left(jnp.int32(1), idx_bits - 1 - i)
        cnt = _count(eq & (idx < cand))
        return jnp.where(cnt < need, cand, j)

    def tie_search():
        return lax.fori_loop(0, idx_bits, index_bit, jnp.zeros((rows, 1), I32))

    def take_all():
        return jnp.full((rows, 1), 2 ** idx_bits, I32)

    surplus = jnp.max(jnp.abs(_count(eq) - need))
    j = lax.cond(surplus == 0.0, take_all, tie_search)
    return gt | (eq & (idx <= j))


def _masked_softmax_parts(s, mask):
    s = jnp.where(mask, s, NEG)
    m = jnp.max(s, axis=-1, keepdims=True)
    p = jnp.where(mask, jnp.exp(s - m), 0.0)
    return p, jnp.sum(p, axis=-1, keepdims=True)


def _attend_heads(qg, k, v, mask):
    s = _dot_nt(qg, k) * ATT_SCALE
    ps, ls = [], []
    for r in range(GQA):
        p, l = _masked_softmax_parts(s[r * Q_BLOCK:(r + 1) * Q_BLOCK], mask)
        ps.append(p.astype(BF16))
        ls.append(l)
    o = _dot(jnp.concatenate(ps, axis=0), v)
    return o / jnp.maximum(jnp.concatenate(ls, axis=0), 1e-30)


def _rope(y, c, a, b, half):
    return y * c + pltpu.roll(y, LANES - half, 1) * a + pltpu.roll(y, half, 1) * b


def _proj_kernel(kind_ref, x_ref, g_ref, w_ref, tab_ref, o_ref, *rest, tn):
    kv_refs, xn_ref = rest[:4], rest[4]
    assert tn == 4 * LANES, "each K/V part must be exactly one column block"
    j = pl.program_id(1)

    @pl.when(j == 0)
    def _():
        xf = x_ref[...]
        r = lax.rsqrt(jnp.mean(xf * xf, axis=-1, keepdims=True) + RMS_EPS)
        xn_ref[...] = ((xf * r) * g_ref[...]).astype(BF16)

    lane = _iota((1, LANES), 1)
    n_gate_blocks = (C_QA - C_GM) // (tn // LANES)

    @pl.when(j < n_gate_blocks)
    def _():
        for n0 in range(0, tn, 2 * LANES):
            acc = _dot(xn_ref[...], w_ref[:, n0:n0 + 2 * LANES])
            o_ref[:, n0:n0 + 2 * LANES] = 1.0 / (1.0 + jnp.exp(-acc))

    @pl.when(j >= n_gate_blocks)
    def _():
        for n0 in range(0, tn, 2 * LANES):
            acc = _dot(xn_ref[...], w_ref[:, n0:n0 + 2 * LANES])
            for c in range(2):
                cs = slice(n0 + c * LANES, n0 + (c + 1) * LANES)
                y = acc[:, c * LANES:(c + 1) * LANES]
                k = kind_ref[j * (tn // LANES) + n0 // LANES + c]
                is_idx = (k == K_ROPE_IDX) | (k == K_ROPE_IDX_LOW)
                is_rope = (k == K_ROPE_HEAD) | is_idx
                t = jnp.where(is_idx, 1, 0)
                half = jnp.where(is_idx, IDX_ROPE_DIM // 2, ROPE_DIM // 2)
                roped = (y * tab_ref[t, :, 0:128] + pltpu.roll(y, LANES - half, 1) * tab_ref[t, :, 128:256]
                         + pltpu.roll(y, half, 1) * tab_ref[t, :, 256:384])
                rope_lanes = jnp.where(is_rope, jnp.where(k == K_ROPE_IDX_LOW, IDX_DIM, LANES), 0)
                o_ref[:, cs] = jnp.where(lane < rope_lanes, roped, y)

    tm = o_ref.shape[0]
    for chunk0, kv_ref in ((C_KVA, kv_refs[0]), (C_KVC, kv_refs[1]), (C_KVS, kv_refs[2]), (C_KVW, kv_refs[3])):
        @pl.when(j == chunk0 // (tn // LANES))
        def _(kv_ref=kv_ref):
            for c in range(4):
                kv_ref[pl.ds(c, tm, stride=4), :] = o_ref[:, c * LANES:(c + 1) * LANES]


def _project(x2d, gain, w_packed, kinds, tables, tm, n_tab_blocks):
    t = x2d.shape[0]
    tn = 512
    grid = (t // tm, NP // tn)
    return pl.pallas_call(
        functools.partial(_proj_kernel, tn=tn),
        grid_spec=pltpu.PrefetchScalarGridSpec(
            num_scalar_prefetch=1, grid=grid,
            in_specs=[
                pl.BlockSpec((tm, D_MODEL), lambda i, j, kr: (i, 0)),
                pl.BlockSpec((1, D_MODEL), lambda i, j, kr: (0, 0)),
                pl.BlockSpec((D_MODEL, tn), lambda i, j, kr: (0, j)),
                pl.BlockSpec((2, tm, 3 * LANES), lambda i, j, kr: (0, i % n_tab_blocks, 0)),
            ],
            out_specs=[pl.BlockSpec((tm, tn), lambda i, j, kr: (i, j))]
            + [pl.BlockSpec((4 * tm, LANES), lambda i, j, kr: (i, 0), pipeline_mode=pl.Buffered(1))] * 4,
            scratch_shapes=[pltpu.VMEM((tm, D_MODEL), BF16)]),
        out_shape=[jax.ShapeDtypeStruct((t, NP), F32)] + [jax.ShapeDtypeStruct((4 * t, LANES), F32)] * 4,
        compiler_params=pltpu.CompilerParams(
            dimension_semantics=("parallel", "arbitrary"), vmem_limit_bytes=VMEM_LIMIT),
        name="in_proj",
    )(kinds, x2d, gain, w_packed, tables)


def _dsa_prompt_body(qa_ref, qi_ref, kiwq_ref, kiwk_ref, kva_ref, o_ref, q0, ext, k_top):
    lane = _iota((1, LANES), 1)
    qpos = q0 + _iota((Q_BLOCK, 1), 0)
    causal = _iota((1, ext), 1) <= qpos
    if ext <= k_top:
        sel = causal
    else:
        kk = kiwk_ref[0:ext, :]
        k_dup = jnp.where(lane < IDX_DIM, kk, pltpu.roll(kk, IDX_DIM, 1)).astype(BF16)
        wi = kiwq_ref[...]
        score = jnp.zeros((Q_BLOCK, ext), F32)
        for pair in range(IDX_HEADS // 2):
            qp = qi_ref[:, pair * LANES:(pair + 1) * LANES]
            for half in range(2):
                h = 2 * pair + half
                keep = (lane < IDX_DIM) if half == 0 else (lane >= IDX_DIM)
                qm = jnp.where(keep, qp, 0.0).astype(BF16)
                rel = jnp.maximum(_dot_nt(qm, k_dup), 0.0)
                score = score + rel * wi[:, IDX_DIM + h:IDX_DIM + h + 1]
        score = jnp.where(causal, score * IDX_SCALE, NEG)
        sel = _topk_mask(score, k_top, int(np.ceil(np.log2(ext)))) & causal

    for g in range(A_KV_HEADS):
        kg = kva_ref[0:ext, (2 * g) * LANES:(2 * g + 1) * LANES].astype(BF16)
        vg = kva_ref[0:ext, (2 * g + 1) * LANES:(2 * g + 2) * LANES].astype(BF16)
        qg = jnp.concatenate(
            [qa_ref[:, (g * GQA + r) * LANES:(g * GQA + r + 1) * LANES] for r in range(GQA)], axis=0).astype(BF16)
        o = _attend_heads(qg, kg, vg, sel)
        for r in range(GQA):
            o_ref[:, (g * GQA + r) * LANES:(g * GQA + r + 1) * LANES] = o[r * Q_BLOCK:(r + 1) * Q_BLOCK]


def _causal_extents(seq):
    exts = [seq]
    while exts[0] // 2 >= 2 * Q_BLOCK and len(exts) < MAX_CAUSAL_EXTENTS:
        exts.insert(0, exts[0] // 2)
    return exts


def _for_each_extent(i, seq, body):
    lo = 0
    for ext in _causal_extents(seq):
        hi = ext // Q_BLOCK

        @pl.when((i >= lo) & (i < hi))
        def _(ext=ext):
            body(ext)
        lo = hi


def _dsa_prompt_kernel(qa_ref, qi_ref, kiwq_ref, kiwk_ref, kva_ref, o_ref, *, seq):
    i = pl.program_id(1)
    _for_each_extent(i, seq, lambda ext: _dsa_prompt_body(
        qa_ref, qi_ref, kiwq_ref, kiwk_ref, kva_ref, o_ref, i * Q_BLOCK, ext, min(DSA_TOPK, seq // 4)))


def _dsa_prompt(y, batch, seq):
    nqb = seq // Q_BLOCK
    return pl.pallas_call(
        functools.partial(_dsa_prompt_kernel, seq=seq),
        grid=(batch, nqb),
        in_specs=[
            pl.BlockSpec((Q_BLOCK, 8 * LANES), lambda b, i: (b * nqb + i, C_QA // 8)),
            pl.BlockSpec((Q_BLOCK, 8 * LANES), lambda b, i: (b * nqb + i, C_QI // 8)),
            pl.BlockSpec((Q_BLOCK, LANES), lambda b, i: (b * nqb + i, C_KIW)),
            pl.BlockSpec((seq, LANES), lambda b, i: (b, C_KIW)),
            pl.BlockSpec((seq, 4 * LANES), lambda b, i: (b, C_KVA // 4)),
        ],
        out_specs=pl.BlockSpec((Q_BLOCK, 8 * LANES), lambda b, i: (b * nqb + i, 0)),
        out_shape=jax.ShapeDtypeStruct((batch * seq, A_HEADS * HEAD_DIM), F32),
        compiler_params=pltpu.CompilerParams(
            dimension_semantics=("parallel", "arbitrary"), vmem_limit_bytes=VMEM_LIMIT),
        name="dsa_prompt",
    )(y, y, y, y, y)


def _pe_bias(pe_ref, w_ref, c):
    pb = _dot(pe_ref[c].astype(BF16), w_ref[c])
    return pb[0:1, 0:LANES] + pb[1:2, LANES:2 * LANES]


def _compress_prompt_kernel(kvc00_ref, kvc01_ref, kvc10_ref, kvc11_ref, w_ref, pe_ref, tab_ref, o_ref, *, nsub):
    kvc = ((kvc00_ref, kvc01_ref), (kvc10_ref, kvc11_ref))
    for c in range(2):
        xs = []
        for g in range(B_KV_HEADS):
            xs.append(jnp.concatenate(
                [kvc[g][c][pl.ds(l, nsub, stride=CMP_STRIDE), :] for l in range(CMP_STRIDE)], axis=1))
        pq = _dot(jnp.concatenate(xs, axis=0).astype(BF16), w_ref[c])
        bias = _pe_bias(pe_ref, w_ref, c)
        for g in range(B_KV_HEADS):
            top = pq[g * nsub:(g + 1) * nsub, 0:LANES]
            bot = pq[g * nsub:(g + 1) * nsub, LANES:2 * LANES]
            comp = top + pltpu.roll(bot, nsub - 1, 0) + bias
            if c == 0:
                comp = _rope(comp, tab_ref[:, 0:128], tab_ref[:, 128:256], tab_ref[:, 256:384], ROPE_DIM // 2)
            o_ref[:, (2 * g + c) * LANES:(2 * g + c + 1) * LANES] = comp


def _compress_prompt(y, w_cat, pe_r, tab_c, batch, seq):
    nsub = seq // CMP_STRIDE
    return pl.pallas_call(
        functools.partial(_compress_prompt_kernel, nsub=nsub),
        grid=(batch,),
        in_specs=[
            pl.BlockSpec((seq, LANES), lambda b: (b, C_KVC)),
            pl.BlockSpec((seq, LANES), lambda b: (b, C_KVC + 1)),
            pl.BlockSpec((seq, LANES), lambda b: (b, C_KVC + 2)),
            pl.BlockSpec((seq, LANES), lambda b: (b, C_KVC + 3)),
            pl.BlockSpec((2, CMP_STRIDE * HEAD_DIM, 2 * LANES), lambda b: (0, 0, 0)),
            pl.BlockSpec((2, 8, CMP_STRIDE * HEAD_DIM), lambda b: (0, 0, 0)),
            pl.BlockSpec((nsub, 3 * LANES), lambda b: (0, 0)),
        ],
        out_specs=pl.BlockSpec((nsub, 4 * LANES), lambda b: (b, 0)),
        out_shape=jax.ShapeDtypeStruct((batch * nsub, 4 * LANES), F32),
        compiler_params=pltpu.CompilerParams(dimension_semantics=("parallel",), vmem_limit_bytes=VMEM_LIMIT),
        name="compress_prompt",
    )(y, y, y, y, w_cat, pe_r, tab_c)


def _cover(n_rows, n_cols, n_cmp, n_slc):
    ci = _iota((n_rows, n_cols), 0) * CMP_STRIDE
    sj = _iota((n_rows, n_cols), 1) * SLC_BLOCK
    m = (ci < sj + SLC_BLOCK) & (ci + CMP_BLOCK > sj)
    m = m & (_iota((n_rows, n_cols), 0) < n_cmp) & (_iota((n_rows, n_cols), 1) < n_slc)
    return jnp.where(m, 1.0, 0.0).astype(BF16)


def _block_scores(p_s, qpos_col, n_slc):
    blk = _iota(p_s.shape, 1)
    cur = qpos_col >> 6
    forced = (blk == 0) | ((blk <= cur) & (blk > cur - SLC_LOCAL))
    score = jnp.where(blk <= cur, p_s + FORCE_BONUS * jnp.where(forced, 1.0, 0.0), NEG)
    return jnp.where(blk < n_slc, score, -jnp.inf)


def _select_blocks(p_sum, cover_t, qpos_row, n_slc, n_top):
    hi, mid, lo = _split3(p_sum)
    p_s = (_dot_nt(cover_t, hi) + _dot_nt(cover_t, mid) + _dot_nt(cover_t, lo))[0:n_slc]
    blk = _iota((n_slc, Q_BLOCK), 0)
    cur = qpos_row >> 6
    forced = (blk == 0) | ((blk <= cur) & (blk > cur - SLC_LOCAL))
    score = jnp.where(blk <= cur, p_s + FORCE_BONUS * jnp.where(forced, 1.0, 0.0), NEG)
    rank = jnp.zeros((n_slc, Q_BLOCK), F32)
    for i in range(n_slc):
        si = score[i:i + 1, :]
        rank = rank + jnp.where((si > score) | ((si == score) & (blk > i)), 1.0, 0.0)
    sel_t = jnp.where(rank < n_top, 1.0, 0.0)
    sel_t = jnp.concatenate([sel_t, jnp.zeros((LANES - n_slc, Q_BLOCK), F32)], axis=0)
    return sel_t.T


def _nsa_prompt_body(qb_ref, gn_ref, cmp_ref, kvs_ref, kvw_ref, o_ref, i, ext, seq):
    q0 = i * Q_BLOCK
    nsub = seq // CMP_STRIDE
    n_cmp = (seq - CMP_BLOCK) // CMP_STRIDE + 1
    n_slc = seq // SLC_BLOCK
    qpos = q0 + _iota((Q_BLOCK, 1), 0)
    qpos_row = q0 + _iota((1, Q_BLOCK), 1)
    c_end = _iota((1, nsub), 1) * CMP_STRIDE + (CMP_BLOCK - 1)
    cmask = c_end <= qpos
    cj = _iota((LANES, nsub), 0) * SLC_BLOCK
    cn = _iota((LANES, nsub), 1) * CMP_STRIDE
    cover_t = jnp.where((cn < cj + SLC_BLOCK) & (cn + CMP_BLOCK > cj) & (_iota((LANES, nsub), 1) < n_cmp)
                        & (_iota((LANES, nsub), 0) < n_slc), 1.0, 0.0).astype(BF16)
    kpos = _iota((1, ext), 1)
    expand = jnp.where((_iota((LANES, ext), 1) >> 6) == _iota((LANES, ext), 0), 1.0, 0.0).astype(BF16)
    gates = 1.0 / (1.0 + jnp.exp(-gn_ref[...]))

    for g in range(B_KV_HEADS):
        qg = jnp.concatenate(
            [qb_ref[:, (g * GQA + r) * LANES:(g * GQA + r + 1) * LANES] for r in range(GQA)], axis=0).astype(BF16)
        kc = cmp_ref[:, (2 * g) * LANES:(2 * g + 1) * LANES].astype(BF16)
        vc = cmp_ref[:, (2 * g + 1) * LANES:(2 * g + 2) * LANES].astype(BF16)
        s = _dot_nt(qg, kc) * ATT_SCALE
        ps = []
        for r in range(GQA):
            p, l = _masked_softmax_parts(s[r * Q_BLOCK:(r + 1) * Q_BLOCK], cmask)
            ps.append(p / jnp.maximum(l, 1e-30))
        o_c = _dot(jnp.concatenate(ps, axis=0).astype(BF16), vc)
        sel = _select_blocks(ps[0] + ps[1] + ps[2] + ps[3], cover_t, qpos_row, n_slc, min(SLC_TOPN, n_slc))
        tok = (_dot(sel.astype(BF16), expand) > 0.5) & (kpos <= qpos)
        ks = kvs_ref[0:ext, (2 * g) * LANES:(2 * g + 1) * LANES].astype(BF16)
        vs = kvs_ref[0:ext, (2 * g + 1) * LANES:(2 * g + 2) * LANES].astype(BF16)
        o_s = _attend_heads(qg, ks, vs, tok)
        nwb = WINDOW // Q_BLOCK + 1
        k_parts, v_parts, m_parts = [], [], []
        for w in range(nwb):
            kb = i - (nwb - 1) + w
            start = pl.multiple_of(jnp.maximum(kb, 0) * Q_BLOCK, Q_BLOCK)
            k_parts.append(kvw_ref[pl.ds(start, Q_BLOCK), (2 * g) * LANES:(2 * g + 1) * LANES].astype(BF16))
            v_parts.append(kvw_ref[pl.ds(start, Q_BLOCK), (2 * g + 1) * LANES:(2 * g + 2) * LANES].astype(BF16))
            wpos = kb * Q_BLOCK + _iota((1, Q_BLOCK), 1)
            m_parts.append(jnp.where((wpos <= qpos) & (wpos > qpos - WINDOW) & (wpos >= 0), 1.0, 0.0))
        wmask = jnp.concatenate(m_parts, axis=1) > 0.5
        o_w = _attend_heads(qg, jnp.concatenate(k_parts, axis=0), jnp.concatenate(v_parts, axis=0), wmask)
        for r in range(GQA):
            h = g * GQA + r
            rs = slice(r * Q_BLOCK, (r + 1) * Q_BLOCK)
            o_ref[:, h * LANES:(h + 1) * LANES] = (gates[:, 3 * h:3 * h + 1] * o_c[rs]
                                                   + gates[:, 3 * h + 1:3 * h + 2] * o_s[rs]
                                                   + gates[:, 3 * h + 2:3 * h + 3] * o_w[rs])


def _nsa_prompt_kernel(qb_ref, gn_ref, cmp_ref, kvs_ref, kvw_ref, o_ref, *, seq):
    i = pl.program_id(1)
    _for_each_extent(i, seq, lambda ext: _nsa_prompt_body(
        qb_ref, gn_ref, cmp_ref, kvs_ref, kvw_ref, o_ref, i, ext, seq))


def _nsa_prompt(y, cmp, batch, seq):
    nqb = seq // Q_BLOCK
    nsub = seq // CMP_STRIDE
    return pl.pallas_call(
        functools.partial(_nsa_prompt_kernel, seq=seq),
        grid=(batch, nqb),
        in_specs=[
            pl.BlockSpec((Q_BLOCK, 8 * LANES), lambda b, i: (b * nqb + i, C_QB // 8)),
            pl.BlockSpec((Q_BLOCK, LANES), lambda b, i: (b * nqb + i, C_GN)),
            pl.BlockSpec((nsub, 4 * LANES), lambda b, i: (b, 0)),
            pl.BlockSpec((seq, 4 * LANES), lambda b, i: (b, C_KVS // 4)),
            pl.BlockSpec((seq, 4 * LANES), lambda b, i: (b, C_KVW // 4)),
        ],
        out_specs=pl.BlockSpec((Q_BLOCK, 8 * LANES), lambda b, i: (b * nqb + i, 0)),
        out_shape=jax.ShapeDtypeStruct((batch * seq, B_HEADS * HEAD_DIM), F32),
        compiler_params=pltpu.CompilerParams(
            dimension_semantics=("parallel", "arbitrary"), vmem_limit_bytes=VMEM_LIMIT),
        name="nsa_prompt",
    )(y, y, cmp, y, y)


def _merge_kernel(x_ref, ga_ref, gb_ref, oa_ref, ob_ref, woa_ref, wob_ref, wout_ref, o_ref):
    u = (ga_ref[...] * _dot(oa_ref[...].astype(BF16), woa_ref[...])
         + gb_ref[...] * _dot(ob_ref[...].astype(BF16), wob_ref[...]))
    o_ref[...] = x_ref[...] + _dot(u.astype(BF16), wout_ref[...])


def _merge(x2d, y, o_a, o_b, w_o_a, w_o_b, w_out, tm):
    t = x2d.shape[0]
    const = dict(pipeline_mode=pl.Buffered(1))
    return pl.pallas_call(
        _merge_kernel,
        grid=(t // tm,),
        in_specs=[
            pl.BlockSpec((tm, D_MODEL), lambda i: (i, 0)),
            pl.BlockSpec((tm, D_MODEL), lambda i: (i, 0)),
            pl.BlockSpec((tm, D_MODEL), lambda i: (i, 1)),
            pl.BlockSpec((tm, A_HEADS * HEAD_DIM), lambda i: (i, 0)),
            pl.BlockSpec((tm, B_HEADS * HEAD_DIM), lambda i: (i, 0)),
            pl.BlockSpec((A_HEADS * HEAD_DIM, D_MODEL), lambda i: (0, 0), **const),
            pl.BlockSpec((B_HEADS * HEAD_DIM, D_MODEL), lambda i: (0, 0), **const),
            pl.BlockSpec((D_MODEL, D_MODEL), lambda i: (0, 0), **const),
        ],
        out_specs=pl.BlockSpec((tm, D_MODEL), lambda i: (i, 0)),
        out_shape=jax.ShapeDtypeStruct((t, D_MODEL), F32),
        compiler_params=pltpu.CompilerParams(dimension_semantics=("parallel",), vmem_limit_bytes=VMEM_LIMIT),
        name="merge",
    )(x2d, y, y, o_a, o_b, w_o_a, w_o_b, w_out)


def _router_kernel(h_ref, g_ref, whi_ref, wlo_ref, b_ref, xn_ref, comb_ref):
    hf = h_ref[...]
    r = lax.rsqrt(jnp.mean(hf * hf, axis=-1, keepdims=True) + RMS_EPS)
    xn = (hf * r) * g_ref[...]
    xhi = xn.astype(BF16)
    xn_ref[...] = xhi
    xlo = (xn - xhi.astype(F32)).astype(BF16)
    logits = _dot(xhi, whi_ref[...]) + _dot(xlo, whi_ref[...]) + _dot(xhi, wlo_ref[...]) + b_ref[...]
    lane_i = _iota(logits.shape, 1)
    lane = lane_i.astype(F32)
    big = jnp.float32(1e9)
    gm = (lane_i >= N_EXPERTS) & (lane_i < N_EXPERTS + N_GROUPS)
    gmax = jnp.max(jnp.where(gm, logits, -jnp.inf), axis=-1, keepdims=True)
    gsum = jnp.sum(jnp.where(gm, jnp.exp(logits - gmax), 0.0), axis=-1, keepdims=True)
    p_group = 1.0 / gsum
    g_sel = jnp.min(jnp.where(gm & (logits == gmax), lane, big), axis=-1, keepdims=True) - N_EXPERTS
    em = (lane_i < N_EXPERTS) & ((lane_i >> 2).astype(F32) == g_sel)
    emax = jnp.max(jnp.where(em, logits, -jnp.inf), axis=-1, keepdims=True)
    pe = jnp.where(em, jnp.exp(logits - emax), 0.0)
    p_exp = pe / jnp.sum(pe, axis=-1, keepdims=True)
    p1 = jnp.max(jnp.where(em, p_exp, -1.0), axis=-1, keepdims=True)
    i1 = jnp.min(jnp.where(em & (p_exp == p1), lane, big), axis=-1, keepdims=True)
    em2 = em & (lane != i1)
    p2 = jnp.max(jnp.where(em2, p_exp, -1.0), axis=-1, keepdims=True)
    i2 = jnp.min(jnp.where(em2 & (p_exp == p2), lane, big), axis=-1, keepdims=True)
    tot = p1 + p2
    comb = jnp.where(lane == i1, p_group * p1 / tot, jnp.where(lane == i2, p_group * p2 / tot, 0.0))
    comb_ref[...] = jnp.where(lane_i == GROUP_LANE, g_sel, comb)


def _router(h2d, gain, w_hi, w_lo, bias, tm):
    t = h2d.shape[0]
    return pl.pallas_call(
        _router_kernel,
        grid=(t // tm,),
        in_specs=[
            pl.BlockSpec((tm, D_MODEL), lambda i: (i, 0)),
            pl.BlockSpec((1, D_MODEL), lambda i: (0, 0)),
            pl.BlockSpec((D_MODEL, LANES), lambda i: (0, 0)),
            pl.BlockSpec((D_MODEL, LANES), lambda i: (0, 0)),
            pl.BlockSpec((1, LANES), lambda i: (0, 0)),
        ],
        out_specs=[pl.BlockSpec((tm, D_MODEL), lambda i: (i, 0)), pl.BlockSpec((tm, LANES), lambda i: (i, 0))],
        out_shape=[jax.ShapeDtypeStruct((t, D_MODEL), BF16), jax.ShapeDtypeStruct((t, LANES), F32)],
        compiler_params=pltpu.CompilerParams(dimension_semantics=("parallel",), vmem_limit_bytes=VMEM_LIMIT),
        name="router",
    )(h2d, gain, w_hi, w_lo, bias)


MOE_SEGMENT = 512


def _moe_capacity(rows):
    want = rows / N_GROUPS + 4.5 * np.sqrt(rows * (N_GROUPS - 1)) / N_GROUPS
    return max(16, int(-(-want // 16)) * 16)


def _lane_col(x, lane_idx):
    return jnp.sum(jnp.where(_iota(x.shape, 1) == lane_idx, x, 0.0), axis=-1, keepdims=True)


def _expert_ffn(x, wg_ref, wu_ref, wd_ref, ce):
    gate = _dot(x, wg_ref[0])
    hid = (gate * (1.0 / (1.0 + jnp.exp(-gate)))) * _dot(x, wu_ref[0])
    return _dot((hid * ce).astype(BF16), wd_ref[0])


def _moe_kernel(xn_ref, comb_ref, h_ref, wg_ref, wu_ref, wd_ref, gf_ref, o_ref,
                xc_ref, cc_ref, oc_ref, st_ref, cnt_ref, *, cap, n_seg):
    e = pl.program_id(1)
    g = e // EXPERTS_PER_GROUP
    tm = xn_ref.shape[0]

    @pl.when(e == 0)
    def _():
        o_ref[...] = h_ref[...]

    seg = tm // n_seg

    @pl.when(e % EXPERTS_PER_GROUP == 0)
    def _():
        lower = jnp.where(_iota((seg, seg), 0) > _iota((seg, seg), 1), 1.0, 0.0).astype(BF16)
        worst = jnp.float32(0.0)
        for k in range(n_seg):
            rows, slots = slice(k * seg, (k + 1) * seg), slice(k * cap, (k + 1) * cap)
            comb = comb_ref[rows, :]
            member = comb[:, GROUP_LANE:GROUP_LANE + 1] == g.astype(F32)
            mb = jnp.broadcast_to(jnp.where(member, 1.0, 0.0), (seg, LANES))
            worst = jnp.maximum(worst, jnp.sum(mb[:, 0:1]))
            rank_b = _dot(lower, mb.astype(BF16))
            rank_row = rank_b.T[0:1, :]
            m_row = mb.T[0:1, :]
            sel = jnp.where((m_row > 0.5) & (rank_row == _iota((cap, seg), 0).astype(F32)), 1.0, 0.0).astype(BF16)
            st_ref[rows, :] = jnp.where(member & (rank_b[:, 0:1] == _iota((seg, cap), 1).astype(F32)),
                                        1.0, 0.0).astype(BF16)
            xc_ref[slots, :] = _dot(sel, xn_ref[rows, :]).astype(BF16)
            cc_ref[slots, :] = _dot3_exact_lhs(sel, comb)
        cnt_ref[0] = worst.astype(I32)
        oc_ref[...] = jnp.zeros_like(oc_ref)

    fits = cnt_ref[0] <= cap

    @pl.when(fits)
    def _():
        oc_ref[...] += _expert_ffn(xc_ref[...], wg_ref, wu_ref, wd_ref, _lane_col(cc_ref[...], e))

    @pl.when(jnp.logical_not(fits))
    def _():
        o_ref[...] += _expert_ffn(xn_ref[...], wg_ref, wu_ref, wd_ref, _lane_col(comb_ref[...], e))

    @pl.when(fits & (e % EXPERTS_PER_GROUP == EXPERTS_PER_GROUP - 1))
    def _():
        for k in range(n_seg):
            rows = slice(k * seg, (k + 1) * seg)
            oc = oc_ref[k * cap:(k + 1) * cap, :]
            hi = oc.astype(BF16)
            lo = (oc - hi.astype(F32)).astype(BF16)
            o_ref[rows, :] += _dot(st_ref[rows, :], hi) + _dot(st_ref[rows, :], lo)

    @pl.when(e == N_EXPERTS - 1)
    def _():
        out = o_ref[...]
        r = lax.rsqrt(jnp.mean(out * out, axis=-1, keepdims=True) + RMS_EPS)
        o_ref[...] = (out * r) * gf_ref[...]


def _moe(xn, comb, h2d, w_g, w_u, w_d, gain_final, tm):
    t = xn.shape[0]
    n_seg = max(1, tm // MOE_SEGMENT)
    cap = _moe_capacity(tm // n_seg)
    return pl.pallas_call(
        functools.partial(_moe_kernel, cap=cap, n_seg=n_seg),
        grid=(t // tm, N_EXPERTS),
        in_specs=[
            pl.BlockSpec((tm, D_MODEL), lambda i, e: (i, 0)),
            pl.BlockSpec((tm, LANES), lambda i, e: (i, 0)),
            pl.BlockSpec((tm, D_MODEL), lambda i, e: (i, 0), pipeline_mode=pl.Buffered(1)),
            pl.BlockSpec((1, D_MODEL, EXPERT_FF), lambda i, e: (e, 0, 0)),
            pl.BlockSpec((1, D_MODEL, EXPERT_FF), lambda i, e: (e, 0, 0)),
            pl.BlockSpec((1, EXPERT_FF, D_MODEL), lambda i, e: (e, 0, 0)),
            pl.BlockSpec((1, D_MODEL), lambda i, e: (0, 0)),
        ],
        out_specs=pl.BlockSpec((tm, D_MODEL), lambda i, e: (i, 0), pipeline_mode=pl.Buffered(1)),
        out_shape=jax.ShapeDtypeStruct((t, D_MODEL), F32),
        scratch_shapes=[pltpu.VMEM((n_seg * cap, D_MODEL), BF16),
                        pltpu.VMEM((n_seg * cap, LANES), F32), pltpu.VMEM((n_seg * cap, D_MODEL), F32),
                        pltpu.VMEM((tm, cap), BF16), pltpu.SMEM((1,), I32)],
        compiler_params=pltpu.CompilerParams(
            dimension_semantics=("parallel", "arbitrary"), vmem_limit_bytes=VMEM_LIMIT),
        name="moe",
    )(xn, comb, h2d, w_g, w_u, w_d, gain_final)


def _page_specs(block, n_chunk_pages, row_of):
    def make(k):
        def index_map(s, c, pt):
            return (pt[s, c * n_chunk_pages + k],) + row_of
        return pl.BlockSpec(block, index_map)
    return [make(k) for k in range(n_chunk_pages)]


def _idx_sample_kernel(pt_ref, q_ref, w_ref, knew_ref, *rest, past):
    pages, (o_ref, onew_ref) = rest[:IDX_PAGES], rest[IDX_PAGES:]
    c = pl.program_id(1)
    q = q_ref[...].astype(BF16)
    w = w_ref[...]

    def scores(kt):
        rel = jnp.maximum(_dot(q, kt.astype(BF16)), 0.0) * w
        return jnp.sum(rel.reshape(IDX_HEADS, 8, kt.shape[1]), axis=0) * IDX_SCALE

    kt = jnp.concatenate([jnp.concatenate([p[0], p[0]], axis=0) for p in pages], axis=1)
    o_ref[...] = scores(kt)

    @pl.when(c == 0)
    def _():
        t = _iota((8, LANES), 0) & 3
        j = _iota((8, LANES), 1)
        onew_ref[...] = jnp.where(j <= t, scores(knew_ref[...]), NEG)


def _idx_sample(pt, q_rows, w_col, knew_t, kidx_t):
    ns, n_pages = pt.shape
    n_chunks = n_pages // IDX_PAGES
    past = n_pages * PAGE
    width = IDX_PAGES * PAGE
    return pl.pallas_call(
        functools.partial(_idx_sample_kernel, past=past),
        grid_spec=pltpu.PrefetchScalarGridSpec(
            num_scalar_prefetch=1, grid=(ns, n_chunks),
            in_specs=[
                pl.BlockSpec((None, LANES, LANES), lambda s, c, pt: (s, 0, 0)),
                pl.BlockSpec((None, LANES, 1), lambda s, c, pt: (s, 0, 0)),
                pl.BlockSpec((None, LANES, LANES), lambda s, c, pt: (s, 0, 0)),
            ] + _page_specs((1, IDX_DIM, PAGE), IDX_PAGES, (0, 0)),
            out_specs=[pl.BlockSpec((None, 8, width), lambda s, c, pt: (s, 0, c)),
                       pl.BlockSpec((None, 8, LANES), lambda s, c, pt: (s, 0, 0))]),
        out_shape=[jax.ShapeDtypeStruct((ns, 8, past), F32), jax.ShapeDtypeStruct((ns, 8, LANES), F32)],
        compiler_params=pltpu.CompilerParams(
            dimension_semantics=("parallel", "arbitrary"), vmem_limit_bytes=VMEM_LIMIT),
        name="idx_sample",
    )(pt, q_rows, w_col, knew_t, *([kidx_t] * IDX_PAGES))


def _topk_sample_kernel(s_ref, snew_ref, m_ref, mnew_ref, *, past, k):
    n = s_ref.shape[0] * 8
    s = jnp.concatenate([s_ref[...].reshape(n, past), snew_ref[...].reshape(n, LANES)], axis=1)
    sel = _topk_mask(s, k, int(np.ceil(np.log2(past + LANES))))
    sel = jnp.where(sel, 1.0, 0.0)
    m_ref[...] = sel[:, :past].reshape(m_ref.shape)
    mnew_ref[...] = sel[:, past:].reshape(mnew_ref.shape)


def _topk_sample(scores, scores_new, k):
    ns, _, past = scores.shape
    sps = TOPK_SEQS_PER_STEP if ns % TOPK_SEQS_PER_STEP == 0 else 1
    return pl.pallas_call(
        functools.partial(_topk_sample_kernel, past=past, k=k),
        grid=(ns // sps,),
        in_specs=[pl.BlockSpec((sps, 8, past), lambda s: (s, 0, 0)),
                  pl.BlockSpec((sps, 8, LANES), lambda s: (s, 0, 0))],
        out_specs=[pl.BlockSpec((sps, 8, past), lambda s: (s, 0, 0)),
                   pl.BlockSpec((sps, 8, LANES), lambda s: (s, 0, 0))],
        out_shape=[jax.ShapeDtypeStruct((ns, 8, past), F32), jax.ShapeDtypeStruct((ns, 8, LANES), F32)],
        compiler_params=pltpu.CompilerParams(dimension_semantics=("parallel",), vmem_limit_bytes=VMEM_LIMIT),
        name="topk_sample",
    )(scores, scores_new)


def _rows16_from_rows8(g):
    return jnp.where(_iota((16, 8), 1) == (g * 4 + (_iota((16, 8), 0) >> 2)), 1.0, 0.0).astype(BF16)


def _paged_attn_kernel(pt_ref, q_ref, knew_ref, vnew_ref, m_ref, mnew_ref, *rest, block_mask):
    pages, (o_ref,), (m_sc, l_sc, acc_sc) = rest[:ATTN_PAGES], rest[ATTN_PAGES:ATTN_PAGES + 1], \
        rest[ATTN_PAGES + 1:]
    c = pl.program_id(1)
    width = ATTN_PAGES * PAGE

    @pl.when(c == 0)
    def _():
        m_sc[...] = jnp.full_like(m_sc, NEG)
        l_sc[...] = jnp.zeros_like(l_sc)
        acc_sc[...] = jnp.zeros_like(acc_sc)

    if block_mask:
        nb = m_ref.shape[1]
        first = c * (width // SLC_BLOCK)
        expand = jnp.where(_iota((nb, width), 0) == first + (_iota((nb, width), 1) >> 6), 1.0, 0.0).astype(BF16)
        mask8 = _dot(m_ref[...].astype(BF16), expand)
    else:
        mask8 = m_ref[...]
    mask8 = mask8.astype(BF16)

    def update(g, kg, vg, mask16):
        rs = slice(g * 16, (g + 1) * 16)
        s = _dot_nt(q_ref[g].astype(BF16), kg) * ATT_SCALE
        s = jnp.where(mask16, s, NEG)
        m_old = m_sc[rs]
        m_new = jnp.maximum(m_old, jnp.max(s, axis=-1, keepdims=True))
        p = jnp.where(mask16, jnp.exp(s - m_new), 0.0)
        alpha = jnp.exp(m_old - m_new)
        l_sc[rs] = alpha * l_sc[rs] + jnp.sum(p, axis=-1, keepdims=True)
        acc_sc[rs] = alpha * acc_sc[rs] + _dot(p.astype(BF16), vg)
        m_sc[rs] = m_new

    for g in range(2):
        kg = jnp.concatenate([p[0, pl.ds(2 * g, PAGE, stride=4), :] for p in pages], axis=0).astype(BF16)
        vg = jnp.concatenate([p[0, pl.ds(2 * g + 1, PAGE, stride=4), :] for p in pages], axis=0).astype(BF16)
        update(g, kg, vg, _dot(_rows16_from_rows8(g), mask8) > 0.5)

    @pl.when(c == pl.num_programs(1) - 1)
    def _():
        mnew8 = mnew_ref[...].astype(BF16)
        for g in range(2):
            update(g, knew_ref[g].astype(BF16), vnew_ref[g].astype(BF16), _dot(_rows16_from_rows8(g), mnew8) > 0.5)
        o_ref[...] = acc_sc[...] / jnp.maximum(l_sc[...], 1e-30)


def _paged_attn(pt, q_g, knew, vnew, mask, mask_new, pool, block_mask, name):
    ns, n_pages = pt.shape
    n_chunks = n_pages // ATTN_PAGES
    width = ATTN_PAGES * PAGE
    if block_mask:
        mspec = pl.BlockSpec((None, 8, mask.shape[2]), lambda s, c, pt: (s, 0, 0))
    else:
        mspec = pl.BlockSpec((None, 8, width), lambda s, c, pt: (s, 0, c))
    return pl.pallas_call(
        functools.partial(_paged_attn_kernel, block_mask=block_mask),
        grid_spec=pltpu.PrefetchScalarGridSpec(
            num_scalar_prefetch=1, grid=(ns, n_chunks),
            in_specs=[
                pl.BlockSpec((None, 2, 16, HEAD_DIM), lambda s, c, pt: (s, 0, 0, 0)),
                pl.BlockSpec((None, 2, LANES, HEAD_DIM), lambda s, c, pt: (s, 0, 0, 0)),
                pl.BlockSpec((None, 2, LANES, HEAD_DIM), lambda s, c, pt: (s, 0, 0, 0)),
                mspec,
                pl.BlockSpec((None, 8, LANES), lambda s, c, pt: (s, 0, 0)),
            ] + _page_specs((1, 4 * PAGE, HEAD_DIM), ATTN_PAGES, (0, 0)),
            out_specs=pl.BlockSpec((None, 32, HEAD_DIM), lambda s, c, pt: (s, 0, 0)),
            scratch_shapes=[pltpu.VMEM((32, 1), F32), pltpu.VMEM((32, 1), F32), pltpu.VMEM((32, HEAD_DIM), F32)]),
        out_shape=jax.ShapeDtypeStruct((ns, 32, HEAD_DIM), F32),
        compiler_params=pltpu.CompilerParams(
            dimension_semantics=("parallel", "arbitrary"), vmem_limit_bytes=VMEM_LIMIT),
        name=name,
    )(pt, q_g, knew, vnew, mask, mask_new, *([pool] * ATTN_PAGES))


def _compress_sample_kernel(pt_ref, w_ref, *rest):
    pages, (o_ref,) = rest[:CMP_PAGES], rest[CMP_PAGES:]
    sub = PAGE // CMP_STRIDE
    r = _iota((PAGE, PAGE), 0)
    perm = jnp.where(_iota((PAGE, PAGE), 1) == (r & (sub - 1)) * CMP_STRIDE + (r >> 3), 1.0, 0.0).astype(BF16)
    taps = []
    for p in pages:
        a = jnp.concatenate([p[0, pl.ds(gc, PAGE, stride=4), :] for gc in range(4)], axis=1)
        taps.append(_dot(perm, a.astype(BF16)))
    for c in range(2):
        xs = []
        for g in range(B_KV_HEADS):
            col = (2 * g + c) * LANES
            xs.append(jnp.concatenate(
                [jnp.concatenate([t[l * sub:(l + 1) * sub, col:col + LANES] for l in range(CMP_STRIDE)], axis=1)
                 for t in taps], axis=0))
        pq = _dot(jnp.concatenate(xs, axis=0).astype(BF16), w_ref[c])
        n = CMP_PAGES * sub
        for g in range(B_KV_HEADS):
            o_ref[:, (2 * g + c) * 2 * LANES:(2 * g + c + 1) * 2 * LANES] = pq[g * n:(g + 1) * n]


def _compress_sample(pt, w_cat, pool):
    ns, n_pages = pt.shape
    n_chunks = n_pages // CMP_PAGES
    rows = CMP_PAGES * (PAGE // CMP_STRIDE)
    return pl.pallas_call(
        _compress_sample_kernel,
        grid_spec=pltpu.PrefetchScalarGridSpec(
            num_scalar_prefetch=1, grid=(ns, n_chunks),
            in_specs=[pl.BlockSpec((2, CMP_STRIDE * HEAD_DIM, 2 * LANES), lambda s, c, pt: (0, 0, 0))]
            + _page_specs((1, 4 * PAGE, HEAD_DIM), CMP_PAGES, (0, 0)),
            out_specs=pl.BlockSpec((None, rows, 8 * LANES), lambda s, c, pt: (s, c, 0))),
        out_shape=jax.ShapeDtypeStruct((ns, n_chunks * rows, 8 * LANES), F32),
        compiler_params=pltpu.CompilerParams(
            dimension_semantics=("parallel", "arbitrary"), vmem_limit_bytes=VMEM_LIMIT),
        name="compress_sample",
    )(pt, w_cat, *([pool] * CMP_PAGES))


def _nsa_select_sample_kernel(pq_ref, w_ref, pe_ref, tab_ref, q_ref, oc_ref, bm_ref, *, past, n_blk_pad):
    nsub = past // CMP_STRIDE
    total = past + 4
    n_cmp = (total - CMP_BLOCK) // CMP_STRIDE + 1
    n_slc = -(-total // SLC_BLOCK)
    row_t = _iota((16, 1), 0) >> 2
    c_end = _iota((1, nsub), 1) * CMP_STRIDE + (CMP_BLOCK - 1)
    cmask = (c_end <= past + row_t) & (_iota((1, nsub), 1) < n_cmp)
    p_rows = []
    for g in range(B_KV_HEADS):
        comp = []
        for c in range(2):
            base = (2 * g + c) * 2 * LANES
            v = pq_ref[:, base:base + LANES] + pltpu.roll(pq_ref[:, base + LANES:base + 2 * LANES], nsub - 1, 0)
            v = v + _pe_bias(pe_ref, w_ref, c)
            if c == 0:
                v = _rope(v, tab_ref[:, 0:128], tab_ref[:, 128:256], tab_ref[:, 256:384], ROPE_DIM // 2)
            comp.append(v.astype(BF16))
        s = _dot_nt(q_ref[g].astype(BF16), comp[0]) * ATT_SCALE
        p, l = _masked_softmax_parts(s, cmask)
        p = p / jnp.maximum(l, 1e-30)
        oc_ref[g * 16:(g + 1) * 16, :] = _dot(p.astype(BF16), comp[1])
        p_rows.append(p)
    p_all = jnp.concatenate(p_rows, axis=0)
    gather = jnp.where(_iota((8, 32), 0) == (_iota((8, 32), 1) >> 2), 1.0, 0.0).astype(BF16)
    p_sum = _dot3_exact_lhs(gather, p_all)
    p_s = _dot3_exact_rhs(p_sum, _cover(nsub, n_blk_pad, n_cmp, n_slc))
    qpos = past + (_iota((8, 1), 0) & 3)
    sel = _topk_mask(_block_scores(p_s, qpos, n_slc), min(SLC_TOPN, n_slc), int(np.ceil(np.log2(n_blk_pad))))
    bm_ref[...] = jnp.where(sel, 1.0, 0.0)


def _nsa_select_sample(pq, w_cat, pe_r, tab_c, q_g, past):
    ns, nsub, _ = pq.shape
    n_slc = -(-(past + 4) // SLC_BLOCK)
    n_blk_pad = -(-n_slc // LANES) * LANES
    return pl.pallas_call(
        functools.partial(_nsa_select_sample_kernel, past=past, n_blk_pad=n_blk_pad),
        grid=(ns,),
        in_specs=[
            pl.BlockSpec((None, nsub, 8 * LANES), lambda s: (s, 0, 0)),
            pl.BlockSpec((2, CMP_STRIDE * HEAD_DIM, 2 * LANES), lambda s: (0, 0, 0)),
            pl.BlockSpec((2, 8, CMP_STRIDE * HEAD_DIM), lambda s: (0, 0, 0)),
            pl.BlockSpec((nsub, 3 * LANES), lambda s: (0, 0)),
            pl.BlockSpec((None, 2, 16, HEAD_DIM), lambda s: (s, 0, 0, 0)),
        ],
        out_specs=[pl.BlockSpec((None, 32, HEAD_DIM), lambda s: (s, 0, 0)),
                   pl.BlockSpec((None, 8, n_blk_pad), lambda s: (s, 0, 0))],
        out_shape=[jax.ShapeDtypeStruct((ns, 32, HEAD_DIM), F32), jax.ShapeDtypeStruct((ns, 8, n_blk_pad), F32)],
        compiler_params=pltpu.CompilerParams(dimension_semantics=("parallel",), vmem_limit_bytes=VMEM_LIMIT),
        name="nsa_select_sample",
    )(pq, w_cat, pe_r, tab_c, q_g)


def _window_sample_kernel(win_ref, knew_ref, vnew_ref, q_ref, oc_ref, os_ref, gate_ref, o_ref, *, wb):
    row_t = _iota((16, 1), 0) >> 2
    kpos = _iota((1, wb), 1)
    mask_old = kpos > row_t + (wb - WINDOW)
    mask_new = _iota((1, LANES), 1) <= row_t
    for g in range(B_KV_HEADS):
        rs = slice(g * 16, (g + 1) * 16)
        q = q_ref[g].astype(BF16)
        kw = win_ref[pl.ds(2 * g, wb, stride=4), :].astype(BF16)
        vw = win_ref[pl.ds(2 * g + 1, wb, stride=4), :].astype(BF16)
        s = jnp.concatenate([_dot_nt(q, kw), _dot_nt(q, knew_ref[g].astype(BF16))], axis=1) * ATT_SCALE
        mask = jnp.concatenate([jnp.where(mask_old, 1.0, 0.0), jnp.where(mask_new, 1.0, 0.0)], axis=1) > 0.5
        p, l = _masked_softmax_parts(s, mask)
        o_w = (_dot(p[:, :wb].astype(BF16), vw) + _dot(p[:, wb:].astype(BF16), vnew_ref[g].astype(BF16)))
        o_w = o_w / jnp.maximum(l, 1e-30)
        gt = 1.0 / (1.0 + jnp.exp(-gate_ref[rs]))
        o_ref[rs] = gt[:, 0:1] * oc_ref[rs] + gt[:, 1:2] * os_ref[rs] + gt[:, 2:3] * o_w


def _window_sample(win_rows, knew, vnew, q_g, o_c, o_s, gates, wb):
    ns = q_g.shape[0]
    blk = lambda *shape: pl.BlockSpec((None,) + shape, lambda s: (s,) + (0,) * len(shape))
    return pl.pallas_call(
        functools.partial(_window_sample_kernel, wb=wb),
        grid=(ns,),
        in_specs=[blk(4 * wb, HEAD_DIM), blk(2, LANES, HEAD_DIM), blk(2, LANES, HEAD_DIM), blk(2, 16, HEAD_DIM),
                  blk(32, HEAD_DIM), blk(32, HEAD_DIM), blk(32, LANES)],
        out_specs=blk(32, HEAD_DIM),
        out_shape=jax.ShapeDtypeStruct((ns, 32, HEAD_DIM), F32),
        compiler_params=pltpu.CompilerParams(dimension_semantics=("parallel",), vmem_limit_bytes=VMEM_LIMIT),
        name="window_sample",
    )(win_rows, knew, vnew, q_g, o_c, o_s, gates)


def _rope_table(pos, half, period):
    inv = ROPE_THETA ** (-jnp.arange(half, dtype=F32) / half)
    ang = pos.astype(F32)[:, None] * inv[None, :]
    cos, sin = jnp.cos(ang), jnp.sin(ang)
    lane = np.arange(LANES) % period
    idx = lane % half
    lo = jnp.asarray(lane < half)[None, :]
    hi = jnp.asarray((lane >= half) & (lane < 2 * half))[None, :]
    c = jnp.where(lo | hi, cos[:, idx], 1.0)
    a = jnp.where(lo, -sin[:, idx], 0.0)
    b = jnp.where(hi, sin[:, idx], 0.0)
    return jnp.concatenate([c, a, b], axis=1)


def _proj_tables(pos):
    return jnp.stack([_rope_table(pos, ROPE_DIM // 2, LANES), _rope_table(pos, IDX_ROPE_DIM // 2, IDX_DIM)])


def _chunk_kinds():
    kinds = np.zeros((N_CHUNKS,), np.int32)
    kinds[C_QA:C_QA + 8] = K_ROPE_HEAD
    kinds[C_QI:C_QI + 8] = K_ROPE_IDX
    kinds[C_QB:C_QB + 8] = K_ROPE_HEAD
    for base in (C_KVA, C_KVS, C_KVW):
        kinds[base] = K_ROPE_HEAD
        kinds[base + 2] = K_ROPE_HEAD
    kinds[C_KIW] = K_ROPE_IDX_LOW
    return jnp.asarray(kinds)


def _pack_w_in(w):
    sizes = [("q_a", 1024), ("kv_a", 512), ("q_i", 1024), ("k_i", 64), ("w_i", 16), ("q_b", 1024), ("kv_c", 512),
             ("kv_s", 512), ("kv_w", 512), ("g_n", 24), ("g_m", 4096)]
    parts, off = {}, 0
    for name, n in sizes:
        parts[name] = w[:, off:off + n]
        off += n
    z = lambda n: jnp.zeros((w.shape[0], n), w.dtype)
    cols = [parts["g_m"], parts["q_a"], parts["q_i"], parts["q_b"], parts["kv_a"], parts["kv_c"], parts["kv_s"],
            parts["kv_w"], parts["k_i"], parts["w_i"], z(48), parts["g_n"], z(104), z(2 * LANES)]
    return jnp.concatenate(cols, axis=1).astype(BF16)


def _cols(y, chunk, n):
    return y[:, chunk * LANES:chunk * LANES + n]


def _group_rows(q):
    ns = q.shape[0]
    return q.reshape(ns, 4, 2, GQA, HEAD_DIM).transpose(0, 2, 1, 3, 4).reshape(ns, 2, 16, HEAD_DIM)


def _new_kv(kv):
    ns = kv.shape[0]
    kv = kv.reshape(ns, 4, 2, 2, HEAD_DIM).transpose(0, 2, 3, 1, 4)
    kv = jnp.pad(kv, ((0, 0), (0, 0), (0, 0), (0, LANES - 4), (0, 0)))
    return kv[:, :, 0], kv[:, :, 1]


def _ungroup_rows(o):
    ns = o.shape[0]
    return o.reshape(ns, 2, 4, GQA, HEAD_DIM).transpose(0, 2, 1, 3, 4).reshape(ns * 4, A_HEADS * HEAD_DIM)


def kernel(x_prompt, x_sample, cache_kv_a, cache_kidx, cache_kv_cmp, cache_kv_slc, state_kv_win, page_table,
           norm_mix, w_in, w_phi, pe_phi, w_o_a, w_o_b, w_out, norm_ffn, w_router_group, b_router_group,
           w_router_expert, b_router_expert, w_gate, w_up, w_down, norm_final):
    batch, seq, _ = x_prompt.shape
    ns, dec, _ = x_sample.shape
    assert dec == 4 and norm_mix.shape[0] == 1, "kernel is written for DEC_SEQ=4 and DEPTH=1"
    n_pool = cache_kv_a.shape[1]
    n_pages = page_table.shape[1]
    past = n_pages * PAGE
    wb = state_kv_win.shape[2]

    w_packed = _pack_w_in(w_in[0])
    kinds = _chunk_kinds()
    wp = w_phi[0].astype(BF16)
    w_cat = jnp.concatenate([wp[:, :CMP_STRIDE].reshape(2, CMP_STRIDE * HEAD_DIM, HEAD_DIM),
                             wp[:, CMP_STRIDE:].reshape(2, CMP_STRIDE * HEAD_DIM, HEAD_DIM)], axis=2)
    pe_r = pe_phi[0].transpose(1, 0, 2).reshape(2, 2, CMP_STRIDE * HEAD_DIM)
    pe_r = jnp.pad(pe_r, ((0, 0), (0, 6), (0, 0)))
    woa, wob, wout = w_o_a[0].astype(BF16), w_o_b[0].astype(BF16), w_out[0].astype(BF16)
    w_r = jnp.concatenate([w_router_expert[0], w_router_group[0],
                           jnp.zeros((D_MODEL, LANES - N_EXPERTS - N_GROUPS), F32)], axis=1)
    w_r_hi = w_r.astype(BF16)
    w_r_lo = (w_r - w_r_hi.astype(F32)).astype(BF16)
    b_r = jnp.concatenate([b_router_expert[0], b_router_group[0],
                           jnp.zeros((LANES - N_EXPERTS - N_GROUPS,), F32)])[None, :]
    w_g, w_u, w_d = w_gate[0].astype(BF16), w_up[0].astype(BF16), w_down[0].astype(BF16)
    g_mix, g_ffn, g_fin = norm_mix[0][None, :], norm_ffn[0][None, :], norm_final[None, :]

    xp = x_prompt.reshape(batch * seq, D_MODEL)
    tm_p = 1024
    y_p, kva_p, kvc_p, kvs_p, kvw_p = _project(xp, g_mix, w_packed, kinds, _proj_tables(jnp.arange(seq)), tm_p,
                                               seq // tm_p)
    o_a_p = _dsa_prompt(y_p, batch, seq)
    nsub_p = seq // CMP_STRIDE
    tab_cp = _rope_table(jnp.arange(nsub_p) * CMP_STRIDE + CMP_BLOCK - 1, ROPE_DIM // 2, LANES)
    cmp_p = _compress_prompt(y_p, w_cat, pe_r, tab_cp, batch, seq)
    o_b_p = _nsa_prompt(y_p, cmp_p, batch, seq)
    h_p = _merge(xp, y_p, o_a_p, o_b_p, woa, wob, wout, 256)
    xn_p, comb_p = _router(h_p, g_ffn, w_r_hi, w_r_lo, b_r, 512)
    y_prompt = _moe(xn_p, comb_p, h_p, w_g, w_u, w_d, g_fin, 1024).reshape(batch, seq, D_MODEL)

    xs = x_sample.reshape(ns * 4, D_MODEL)
    tm_s = ns * 4
    pos_s = past + (jnp.arange(ns * 4) % 4)
    y_s, kva_s, kvc_s, kvs_s, kvw_s = _project(xs, g_mix, w_packed, kinds, _proj_tables(pos_s), tm_s, 1)
    pt = page_table.astype(I32)
    qi = _cols(y_s, C_QI, 1024).reshape(ns, 4, IDX_HEADS, IDX_DIM).transpose(0, 2, 1, 3)
    qi = jnp.concatenate([qi, qi], axis=2)
    par = (jnp.arange(IDX_HEADS) % 2)[None, :, None, None, None]
    q_rows = jnp.where(par == jnp.arange(2)[None, None, None, :, None], qi[:, :, :, None, :], 0.0)
    q_rows = q_rows.reshape(ns, IDX_HEADS * 8, 2 * IDX_DIM)
    kiw = _cols(y_s, C_KIW, LANES).reshape(ns, 4, LANES)
    wi = kiw[:, :, IDX_DIM:IDX_DIM + IDX_HEADS].transpose(0, 2, 1)
    w_col = jnp.concatenate([wi, wi], axis=2).reshape(ns, IDX_HEADS * 8, 1)
    knew_t = jnp.pad(kiw[:, :, :IDX_DIM].transpose(0, 2, 1), ((0, 0), (0, 0), (0, LANES - 4)))
    knew_t = jnp.concatenate([knew_t, knew_t], axis=1)
    kidx_t = jnp.swapaxes(cache_kidx[0], 1, 2)
    sc, sc_new = _idx_sample(pt, q_rows, w_col, knew_t, kidx_t)
    m_a, m_a_new = _topk_sample(sc, sc_new, min(DSA_TOPK, (past + 4) // 4))
    q_a = _group_rows(_cols(y_s, C_QA, 1024).reshape(ns, 4, A_HEADS, HEAD_DIM))
    ka_new, va_new = _new_kv(_cols(y_s, C_KVA, 512).reshape(ns, 4, 512))
    o_a_s = _paged_attn(pt, q_a, ka_new, va_new, m_a, m_a_new, cache_kv_a.reshape(n_pool, 4 * PAGE, HEAD_DIM),
                        False, "dsa_sample")
    q_b = _group_rows(_cols(y_s, C_QB, 1024).reshape(ns, 4, B_HEADS, HEAD_DIM))
    pq = _compress_sample(pt, w_cat, cache_kv_cmp.reshape(n_pool, 4 * PAGE, HEAD_DIM))
    nsub_s = past // CMP_STRIDE
    tab_cs = _rope_table(jnp.arange(nsub_s) * CMP_STRIDE + CMP_BLOCK - 1, ROPE_DIM // 2, LANES)
    o_c, blk_mask = _nsa_select_sample(pq, w_cat, pe_r, tab_cs, q_b, past)
    ks_new, vs_new = _new_kv(_cols(y_s, C_KVS, 512).reshape(ns, 4, 512))
    first_new = past // SLC_BLOCK
    t_row = (jnp.arange(8) & 3)[None, :, None]
    m_s_new = jnp.where(jnp.arange(LANES)[None, None, :] <= t_row, blk_mask[:, :, first_new:first_new + 1], 0.0)
    o_s = _paged_attn(pt, q_b, ks_new, vs_new, blk_mask, m_s_new,
                      cache_kv_slc.reshape(n_pool, 4 * PAGE, HEAD_DIM), True, "slc_sample")
    kw_new, vw_new = _new_kv(_cols(y_s, C_KVW, 512).reshape(ns, 4, 512))
    gates = _group_rows(jnp.pad(_cols(y_s, C_GN, 24).reshape(ns, 4, B_HEADS, 3), ((0, 0),) * 3 + ((0, LANES - 3),)))
    gates = gates.reshape(ns, 32, LANES)
    o_b_s = _window_sample(state_kv_win[0].reshape(ns, 4 * wb, HEAD_DIM), kw_new, vw_new, q_b, o_c, o_s, gates, wb)
    h_s = _merge(xs, y_s, _ungroup_rows(o_a_s), _ungroup_rows(o_b_s), woa, wob, wout, tm_s)
    xn_s, comb_s = _router(h_s, g_ffn, w_r_hi, w_r_lo, b_r, tm_s)
    y_sample = _moe(xn_s, comb_s, h_s, w_g, w_u, w_d, g_fin, tm_s).reshape(ns, 4, D_MODEL)

    def kv_out(rows, lead):
        return rows.reshape((1,) + lead + (2, 2, HEAD_DIM))

    wb_p = min(WINDOW, seq)
    win_s = jnp.concatenate([state_kv_win[0][:, 4:], kvw_s.reshape(ns, 4, 2, 2, HEAD_DIM)], axis=1)
    return (y_prompt, y_sample,
            kv_out(kva_p, (batch, seq)), _cols(y_p, C_KIW, IDX_DIM).reshape(1, batch, seq, IDX_DIM),
            kv_out(kvc_p, (batch, seq)), kv_out(kvs_p, (batch, seq)), kv_out(kvw_p, (batch, seq))[:, :, seq - wb_p:],
            kv_out(kva_s, (ns, 4)), _cols(y_s, C_KIW, IDX_DIM).reshape(1, ns, 4, IDX_DIM),
            kv_out(kvc_s, (ns, 4)), kv_out(kvs_s, (ns, 4)), win_s[None])
```
